```python
import jax, jax.numpy as jnp
from jax import lax
import numpy as np

D_MODEL = 2048
BATCH = 32
SEQ = 256
DEPTH = 2
DEC_BATCH = 4
DEC_SEQ = 1024
PAST_LEN = 256

GRID_W = 64
HEAD_DIM = 128
ATTN_W = D_MODEL // 2
N_HEADS = ATTN_W // HEAD_DIM
N_KV_HEADS = N_HEADS // 4
Q_PER_KV = N_HEADS // N_KV_HEADS
KV_W = N_KV_HEADS * HEAD_DIM
WINDOW = 128
BLOCK = 128
SCALE = HEAD_DIM ** -0.5
ROPE_BASE = 10000.0
ROPE_PAIRS = HEAD_DIM // 4
FOURIER_W = D_MODEL // 4
FOURIER_GROUPS = 4
FOURIER_GROUP_W = FOURIER_W // FOURIER_GROUPS
CONV_W = D_MODEL // 4
CONV_K = 3
MIX_W = ATTN_W + FOURIER_W + CONV_W
IN_W = ATTN_W + 2 * KV_W + FOURIER_W + 3 * CONV_W
SPLITS = (ATTN_W, ATTN_W + KV_W, ATTN_W + 2 * KV_W, ATTN_W + 2 * KV_W + FOURIER_W,
          ATTN_W + 2 * KV_W + FOURIER_W + CONV_W, ATTN_W + 2 * KV_W + FOURIER_W + 2 * CONV_W)
N_EXPERTS = 64
TOP_K = 8
N_GROUPS = 8
EXPERTS_PER_GROUP = N_EXPERTS // N_GROUPS
TOPK_GROUPS = 4
EXPERT_FF = 512
SHARED_FF = 512
ROUTE_SCALE = 2.5
N_MOD = 6
EPS = 1e-6
NEG = -1e30

kernel_name = 'hybrid_diffusion_prefix_step'


def rmsnorm(x, g):
    xf = x.astype(jnp.float32)
    y = xf * lax.rsqrt(jnp.mean(xf * xf, axis=-1, keepdims=True) + EPS)
    return (y * g.astype(jnp.float32)).astype(x.dtype)


def axial_rope_tables(S):
    rows = S // GRID_W
    row = jnp.repeat(jnp.arange(rows, dtype=jnp.float32), GRID_W)
    col = jnp.tile(jnp.arange(GRID_W, dtype=jnp.float32), rows)
    inv = ROPE_BASE ** (-jnp.arange(ROPE_PAIRS, dtype=jnp.float32) / ROPE_PAIRS)
    ang = jnp.stack([row[:, None] * inv, col[:, None] * inv], axis=1)
    return jnp.cos(ang), jnp.sin(ang)


def apply_axial_rope(x, cos, sin):
    xf = x.astype(jnp.float32).reshape(*x.shape[:-1], 2, 2, ROPE_PAIRS)
    x1, x2 = xf[..., 0, :], xf[..., 1, :]
    c, s = cos[None, :, None], sin[None, :, None]
    out = jnp.stack([x1 * c - x2 * s, x2 * c + x1 * s], axis=-2)
    return out.reshape(x.shape).astype(x.dtype)


def softmax_with_sink(logits, sink):
    m = jnp.maximum(jnp.max(logits, axis=-1, keepdims=True), sink)
    e = jnp.exp(logits - m)
    return e / (jnp.sum(e, axis=-1, keepdims=True) + jnp.exp(sink - m))


def context_attention(q, k, v, sink):
    B, L = q.shape[:2]
    nb = L // BLOCK
    qb = q.astype(jnp.float32).reshape(B, nb, BLOCK, N_KV_HEADS, Q_PER_KV, HEAD_DIM).transpose(1, 0, 2, 3, 4, 5)
    kf, vf = k.astype(jnp.float32), v.astype(jnp.float32)
    s_sink = sink.astype(jnp.float32).reshape(1, N_KV_HEADS, Q_PER_KV, 1, 1)

    def one_block(qblk):
        s = jnp.einsum('bqhgd,bkhd->bhgqk', qblk, kf) * SCALE
        p = softmax_with_sink(s, s_sink)
        return jnp.einsum('bhgqk,bkhd->bqhgd', p, vf)

    o = lax.map(one_block, qb)
    return o.transpose(1, 0, 2, 3, 4, 5).reshape(B, L, ATTN_W).astype(q.dtype)


def latent_attention(q, k, v, k_ctx, v_ctx, sink):
    B, S = q.shape[:2]
    nb = S // BLOCK
    qb = q.astype(jnp.float32).reshape(B, nb, BLOCK, N_KV_HEADS, Q_PER_KV, HEAD_DIM)

    def band(t):
        tp = jnp.pad(t.astype(jnp.float32), ((0, 0), (BLOCK, BLOCK), (0, 0), (0, 0)))
        return jnp.concatenate([tp[:, i * BLOCK:i * BLOCK + S].reshape(B, nb, BLOCK, N_KV_HEADS, HEAD_DIM)
                                for i in range(3)], axis=2)

    kb, vb = band(k), band(v)
    kc, vc = k_ctx.astype(jnp.float32), v_ctx.astype(jnp.float32)
    qpos = jnp.arange(S).reshape(nb, BLOCK)
    kpos = jnp.arange(nb)[:, None] * BLOCK - BLOCK + jnp.arange(3 * BLOCK)[None, :]
    valid = ((jnp.abs(qpos[:, :, None] - kpos[:, None, :]) <= WINDOW)
             & (kpos >= 0)[:, None, :] & (kpos < S)[:, None, :])
    s_loc = jnp.einsum('bnqhgd,bnkhd->bnhgqk', qb, kb) * SCALE
    s_loc = jnp.where(valid[None, :, None, None], s_loc, NEG)
    s_ctx = jnp.einsum('bnqhgd,bchd->bnhgqc', qb, kc) * SCALE
    s_sink = sink.astype(jnp.float32).reshape(1, 1, N_KV_HEADS, Q_PER_KV, 1, 1)
    p = softmax_with_sink(jnp.concatenate([s_loc, s_ctx], axis=-1), s_sink)
    o = (jnp.einsum('bnhgqk,bnkhd->bnqhgd', p[..., :3 * BLOCK], vb)
         + jnp.einsum('bnhgqc,bchd->bnqhgd', p[..., 3 * BLOCK:], vc))
    return o.reshape(B, S, ATTN_W).astype(q.dtype)


def fourier_mix(f):
    B, L, _ = f.shape
    fg = f.astype(jnp.float32).reshape(B, L, FOURIER_GROUPS, FOURIER_GROUP_W)
    out = jnp.fft.fft2(fg, axes=(1, 3), norm='ortho').real
    return out.reshape(B, L, FOURIER_W).astype(f.dtype)


def short_conv_mix(xc, bg, cg, conv_w):
    u = cg * xc
    L = u.shape[1]
    up = jnp.pad(u, ((0, 0), (1, 1), (0, 0)))
    y = up[:, :L] * conv_w[0] + up[:, 1:L + 1] * conv_w[1] + up[:, 2:] * conv_w[2]
    return bg * y


def moe_ffn(h, w_router, router_bias, w_gate_e, w_up_e, w_down_e, w_gate_s, w_up_s, w_down_s):
    B, L, D = h.shape
    t = h.reshape(B * L, D)
    scores = jax.nn.sigmoid(jnp.dot(t.astype(jnp.float32), w_router.astype(jnp.float32)))
    biased = scores + router_bias.astype(jnp.float32)
    grp = biased.reshape(-1, N_GROUPS, EXPERTS_PER_GROUP)
    grp_score = jnp.sum(lax.top_k(grp, 2)[0], axis=-1)
    _, gidx = lax.top_k(grp_score, TOPK_GROUPS)
    gmask = jnp.any(gidx[:, :, None] == jnp.arange(N_GROUPS)[None, None, :], axis=1)
    emask = jnp.repeat(gmask, EXPERTS_PER_GROUP, axis=-1)
    _, eidx = lax.top_k(jnp.where(emask, biased, -jnp.inf), TOP_K)
    sel = jnp.take_along_axis(scores, eidx, axis=-1)
    wts = sel / jnp.sum(sel, axis=-1, keepdims=True) * ROUTE_SCALE
    gates = jnp.einsum('tk,tke->te', wts, jax.nn.one_hot(eidx, N_EXPERTS, dtype=jnp.float32)).astype(t.dtype)
    shared = (jax.nn.silu(t @ w_gate_s) * (t @ w_up_s)) @ w_down_s

    def add_expert(acc, ew):
        wg, wu, wd, g = ew
        return acc + ((jax.nn.silu(t @ wg) * (t @ wu)) @ wd) * g[:, None], None

    out, _ = lax.scan(add_expert, shared, (w_gate_e, w_up_e, w_down_e, gates.T))
    return out.reshape(B, L, D)


def trunk_layer(x, cond, attend, w_mod, b_mod, g_mix, g_ffn, w_in, w_out, conv_w,
                w_router, router_bias, w_gate_e, w_up_e, w_down_e, w_gate_s, w_up_s, w_down_s):
    B, L, _ = x.shape
    mods = (jax.nn.silu(cond) @ w_mod + b_mod).reshape(*cond.shape[:-1], N_MOD, D_MODEL)[..., None, :, :]
    shift1, scale1, gate1, shift2, scale2, gate2 = (mods[..., i, :] for i in range(N_MOD))
    h = rmsnorm(x, g_mix) * (1 + scale1) + shift1
    q, k, v, f, xc, bg, cg = jnp.split(h @ w_in, SPLITS, axis=-1)
    k = k.reshape(B, L, N_KV_HEADS, HEAD_DIM)
    v = v.reshape(B, L, N_KV_HEADS, HEAD_DIM)
    a = attend(q.reshape(B, L, N_HEADS, HEAD_DIM), k, v)
    mix = jnp.concatenate([a, fourier_mix(f), short_conv_mix(xc, bg, cg, conv_w)], axis=-1) @ w_out
    x = x + gate1 * mix
    h2 = rmsnorm(x, g_ffn) * (1 + scale2) + shift2
    x = x + gate2 * moe_ffn(h2, w_router, router_bias, w_gate_e, w_up_e, w_down_e, w_gate_s, w_up_s, w_down_s)
    return x, k, v


def setup_inputs(seed: int = 0) -> dict:
    key = jax.random.key(seed)
    ks = jax.random.split(key, 24)
    D = D_MODEL

    def nrm(k, shape, s):
        return jax.random.normal(k, shape, jnp.float32) * s

    return {
        'x_prompt': nrm(ks[0], (BATCH, SEQ, D), 1.0),
        'x_sample': nrm(ks[1], (DEC_BATCH, DEC_SEQ, D), 1.0),
        'cache_k': nrm(ks[2], (DEC_BATCH, DEPTH, PAST_LEN, N_KV_HEADS, HEAD_DIM), 1.0),
        'cache_v': nrm(ks[3], (DEC_BATCH, DEPTH, PAST_LEN, N_KV_HEADS, HEAD_DIM), 1.0),
        'c': nrm(ks[4], (DEC_BATCH, D), 1.0),
        'c_ctx': nrm(ks[5], (D,), 1.0),
        'w_mod': nrm(ks[6], (DEPTH, D, N_MOD * D), 0.5 * D ** -0.5),
        'b_mod': nrm(ks[7], (DEPTH, N_MOD * D), 0.02),
        'g_mix': 1.0 + nrm(ks[8], (DEPTH, D), 0.02),
        'g_ffn': 1.0 + nrm(ks[9], (DEPTH, D), 0.02),
        'w_in': nrm(ks[10], (DEPTH, D, IN_W), D ** -0.5),
        'w_out': nrm(ks[11], (DEPTH, MIX_W, D), MIX_W ** -0.5),
        'conv_w': nrm(ks[12], (DEPTH, CONV_K, CONV_W), CONV_K ** -0.5),
        'attn_sink': nrm(ks[13], (DEPTH, N_HEADS), 0.5),
        'w_router': nrm(ks[14], (DEPTH, D, N_EXPERTS), D ** -0.5),
        'router_bias': nrm(ks[15], (DEPTH, N_EXPERTS), 0.01),
        'w_gate_e': nrm(ks[16], (DEPTH, N_EXPERTS, D, EXPERT_FF), D ** -0.5),
        'w_up_e': nrm(ks[17], (DEPTH, N_EXPERTS, D, EXPERT_FF), D ** -0.5),
        'w_down_e': nrm(ks[18], (DEPTH, N_EXPERTS, EXPERT_FF, D), EXPERT_FF ** -0.5),
        'w_gate_s': nrm(ks[19], (DEPTH, D, SHARED_FF), D ** -0.5),
        'w_up_s': nrm(ks[20], (DEPTH, D, SHARED_FF), D ** -0.5),
        'w_down_s': nrm(ks[21], (DEPTH, SHARED_FF, D), SHARED_FF ** -0.5),
        'final_norm': 1.0 + nrm(ks[22], (D,), 0.02),
    }


def reference(x_prompt, x_sample, cache_k, cache_v, c, c_ctx, w_mod, b_mod, g_mix, g_ffn, w_in, w_out,
              conv_w, attn_sink, w_router, router_bias, w_gate_e, w_up_e, w_down_e,
              w_gate_s, w_up_s, w_down_s, final_norm):
    xp = x_prompt
    new_ks, new_vs = [], []
    for l in range(DEPTH):
        attend_ctx = lambda q, k, v, l=l: context_attention(q, k, v, attn_sink[l])
        xp, k_l, v_l = trunk_layer(xp, c_ctx, attend_ctx, w_mod[l], b_mod[l], g_mix[l], g_ffn[l], w_in[l],
                                   w_out[l], conv_w[l], w_router[l], router_bias[l], w_gate_e[l], w_up_e[l],
                                   w_down_e[l], w_gate_s[l], w_up_s[l], w_down_s[l])
        new_ks.append(k_l)
        new_vs.append(v_l)
    y_prompt = rmsnorm(xp, final_norm)
    new_k = jnp.stack(new_ks, axis=1)
    new_v = jnp.stack(new_vs, axis=1)

    cos, sin = axial_rope_tables(x_sample.shape[1])
    xs = x_sample
    for l in range(DEPTH):
        attend_lat = lambda q, k, v, l=l: latent_attention(apply_axial_rope(q, cos, sin), apply_axial_rope(k, cos, sin),
                                                           v, cache_k[:, l], cache_v[:, l], attn_sink[l])
        xs, _, _ = trunk_layer(xs, c, attend_lat, w_mod[l], b_mod[l], g_mix[l], g_ffn[l], w_in[l], w_out[l],
                               conv_w[l], w_router[l], router_bias[l], w_gate_e[l], w_up_e[l], w_down_e[l],
                               w_gate_s[l], w_up_s[l], w_down_s[l])
    y_sample = rmsnorm(xs, final_norm)
    return (y_prompt, y_sample, new_k, new_v)
```

```python
import functools

import numpy as np
import jax
import jax.numpy as jnp
from jax import lax
from jax.experimental import pallas as pl
from jax.experimental.pallas import tpu as pltpu

F32 = jnp.float32
BF16 = jnp.bfloat16
I32 = jnp.int32

D_MODEL = 2048
BATCH, SEQ = 32, 256
DEC_BATCH, DEC_SEQ = 4, 1024
DEPTH = 2
GRID_W = 64
HEAD_DIM = 128
N_HEADS, N_KV_HEADS, Q_PER_KV = 8, 2, 4
ATTN_W, KV_W = 1024, 256
WINDOW, BLOCK = 128, 128
SCALE = HEAD_DIM ** -0.5
ROPE_BASE = 10000.0
ROPE_PAIRS = 32
FOURIER_W, FOURIER_GROUPS, FOURIER_GROUP_W = 512, 4, 128
CONV_W = 512
IN_W = 3584
N_EXPERTS, TOP_K, N_GROUPS, EXPERTS_PER_GROUP, TOPK_GROUPS = 64, 8, 8, 8, 4
EXPERT_FF = 512
ROUTE_SCALE = 2.5
N_MOD = 6
EPS = 1e-6
NEG = -1e30

T_CTX = BATCH * SEQ
T_LAT = DEC_BATCH * DEC_SEQ
T_ALL = T_CTX + T_LAT
TM = 256
N_CTX_TILES = T_CTX // TM
LAT_TILES_PER_SEQ = DEC_SEQ // TM
N_TILES = T_ALL // TM
N_COND = 8
MOD_TN = 1024
TC_ROWS = 128
N_SORT_TILES = T_ALL * TOP_K // TM + N_EXPERTS
P_SORT = N_SORT_TILES * TM
META_LANES = 128
TE_LANES = 512
VMEM_LIMIT = 56 * 1024 * 1024


def _cparams(sem=None):
    return pltpu.CompilerParams(dimension_semantics=sem, vmem_limit_bytes=VMEM_LIMIT)


def _silu(x):
    return x / (1.0 + jnp.exp(-x))


def _sigmoid(x):
    return 1.0 / (1.0 + jnp.exp(-x))


def _rmsnorm(x, g):
    return x * lax.rsqrt(jnp.mean(x * x, axis=-1, keepdims=True) + EPS) * g


def _cond_row(i):
    return jnp.where(i < N_CTX_TILES, 0, 1 + (i - N_CTX_TILES) // LAT_TILES_PER_SEQ)


def _dft_tables(L):
    j = np.arange(L, dtype=np.int64)
    ang = 2.0 * np.pi * ((j[:, None] * j[None, :]) % L).astype(np.float64) / L
    c, s = np.cos(ang) / np.sqrt(L), np.sin(ang) / np.sqrt(L)
    return np.concatenate([c, -s], axis=1)


def _channel_dft():
    n = FOURIER_GROUP_W
    j = np.arange(n, dtype=np.int64)
    ang = 2.0 * np.pi * ((j[:, None] * j[None, :]) % n).astype(np.float64) / n
    return np.concatenate([np.cos(ang), np.sin(ang)], axis=1) / np.sqrt(n)


def _rope_tables(S):
    pos = np.arange(S)
    row, col = (pos // GRID_W).astype(np.float64), (pos % GRID_W).astype(np.float64)
    inv = ROPE_BASE ** (-np.arange(ROPE_PAIRS, dtype=np.float64) / ROPE_PAIRS)
    ar, ac = row[:, None] * inv, col[:, None] * inv
    z = np.zeros_like(ar)
    cos = np.concatenate([np.cos(ar), np.cos(ar), np.cos(ac), np.cos(ac)], axis=1)
    sin_hi = np.concatenate([-np.sin(ar), z, -np.sin(ac), z], axis=1)
    sin_lo = np.concatenate([z, np.sin(ar), z, np.sin(ac)], axis=1)
    return cos, sin_hi, sin_lo


def _mods_kernel(cond_ref, w_ref, b_ref, o_ref):
    s = _silu(cond_ref[...]).astype(BF16)
    o_ref[0] = jnp.dot(s, w_ref[0].astype(BF16), preferred_element_type=F32) + b_ref[0]


def _mods(cond, w_mod, b_mod):
    n = N_MOD * D_MODEL
    return pl.pallas_call(
        _mods_kernel,
        grid=(DEPTH, n // MOD_TN),
        in_specs=[pl.BlockSpec((N_COND, D_MODEL), lambda l, j: (0, 0)),
                  pl.BlockSpec((1, D_MODEL, MOD_TN), lambda l, j: (l, 0, j)),
                  pl.BlockSpec((1, 1, MOD_TN), lambda l, j: (l, 0, j))],
        out_specs=pl.BlockSpec((1, N_COND, MOD_TN), lambda l, j: (l, 0, j)),
        out_shape=jax.ShapeDtypeStruct((DEPTH, N_COND, n), F32),
        compiler_params=_cparams(("arbitrary", "arbitrary")),
        name="mods",
    )(cond, w_mod, b_mod.reshape(DEPTH, 1, n))


def _inproj_kernel(x_ref, mod_ref, g_ref, w_ref, cos_ref, shi_ref, slo_ref, q_ref, kv_ref, fm_ref):
    i = pl.program_id(0)
    shift, scale = mod_ref[0, 0:1, :], mod_ref[0, 1:2, :]
    h = (_rmsnorm(x_ref[...], g_ref[...]) * (1.0 + scale) + shift).astype(BF16)
    fm_ref[...] = jnp.dot(h, w_ref[:, ATTN_W + 2 * KV_W:], preferred_element_type=F32).astype(BF16)
    qkv = jnp.dot(h, w_ref[:, :ATTN_W + 2 * KV_W], preferred_element_type=F32)
    kv_ref[:, KV_W:] = qkv[:, ATTN_W + KV_W:]

    @pl.when(i < N_CTX_TILES)
    def _():
        q_ref[...] = qkv[:, :ATTN_W].astype(BF16)
        kv_ref[:, :KV_W] = qkv[:, ATTN_W:ATTN_W + KV_W]

    @pl.when(i >= N_CTX_TILES)
    def _():
        cos, shi, slo = cos_ref[...], shi_ref[...], slo_ref[...]
        for hd in range(N_HEADS + N_KV_HEADS):
            xh = qkv[:, hd * HEAD_DIM:(hd + 1) * HEAD_DIM]
            r = xh * cos + pltpu.roll(xh, HEAD_DIM - 32, 1) * shi + pltpu.roll(xh, 32, 1) * slo
            if hd < N_HEADS:
                q_ref[:, hd * HEAD_DIM:(hd + 1) * HEAD_DIM] = r.astype(BF16)
            else:
                kv_ref[:, (hd - N_HEADS) * HEAD_DIM:(hd - N_HEADS + 1) * HEAD_DIM] = r


def _inproj(x, mods_l, g_mix_l, w_in_bf, rope):
    lat_blk = lambda i: (jnp.maximum(i - N_CTX_TILES, 0) % LAT_TILES_PER_SEQ, 0)
    row = lambda i: (i, 0)
    return pl.pallas_call(
        _inproj_kernel,
        grid=(N_TILES,),
        in_specs=[pl.BlockSpec((TM, D_MODEL), row),
                  pl.BlockSpec((1, N_MOD, D_MODEL), lambda i: (_cond_row(i), 0, 0)),
                  pl.BlockSpec((1, D_MODEL), lambda i: (0, 0)),
                  pl.BlockSpec((D_MODEL, IN_W), lambda i: (0, 0), pipeline_mode=pl.Buffered(1)),
                  pl.BlockSpec((TM, HEAD_DIM), lat_blk),
                  pl.BlockSpec((TM, HEAD_DIM), lat_blk),
                  pl.BlockSpec((TM, HEAD_DIM), lat_blk)],
        out_specs=[pl.BlockSpec((TM, ATTN_W), row),
                   pl.BlockSpec((TM, 2 * KV_W), row),
                   pl.BlockSpec((TM, FOURIER_W + 3 * CONV_W), row)],
        out_shape=[jax.ShapeDtypeStruct((T_ALL, ATTN_W), BF16),
                   jax.ShapeDtypeStruct((T_ALL, 2 * KV_W), F32),
                   jax.ShapeDtypeStruct((T_ALL, FOURIER_W + 3 * CONV_W), BF16)],
        compiler_params=_cparams(("arbitrary",)),
        name="inproj",
    )(x, mods_l, g_mix_l.reshape(1, D_MODEL), w_in_bf, *rope)


def _fourier_conv(fm_ref, ccsc_ref, csl_ref, cw_ref, o_ref):
    L = fm_ref.shape[0]
    for g in range(FOURIER_GROUPS):
        xg = fm_ref[:, g * FOURIER_GROUP_W:(g + 1) * FOURIER_GROUP_W]
        z = jnp.dot(xg, ccsc_ref[...], preferred_element_type=F32)
        zz = jnp.concatenate([z[:, :FOURIER_GROUP_W], z[:, FOURIER_GROUP_W:]], axis=0).astype(BF16)
        f = jnp.dot(csl_ref[...], zz, preferred_element_type=F32)
        o_ref[:, ATTN_W + g * FOURIER_GROUP_W:ATTN_W + (g + 1) * FOURIER_GROUP_W] = f.astype(BF16)
    xc = fm_ref[:, FOURIER_W:FOURIER_W + CONV_W].astype(F32)
    bg = fm_ref[:, FOURIER_W + CONV_W:FOURIER_W + 2 * CONV_W].astype(F32)
    cg = fm_ref[:, FOURIER_W + 2 * CONV_W:].astype(F32)
    u = cg * xc
    rows = lax.broadcasted_iota(I32, u.shape, 0)
    prev = jnp.where(rows == 0, 0.0, pltpu.roll(u, 1, 0))
    nxt = jnp.where(rows == L - 1, 0.0, pltpu.roll(u, L - 1, 0))
    y = prev * cw_ref[0:1, :] + u * cw_ref[1:2, :] + nxt * cw_ref[2:3, :]
    o_ref[:, ATTN_W + FOURIER_W:] = (bg * y).astype(BF16)


def _ctx_mix_kernel(sink_ref, q_ref, kv_ref, fm_ref, ccsc_ref, csl_ref, cw_ref, o_ref):
    for h in range(N_HEADS):
        g = h // Q_PER_KV
        q = q_ref[:, h * HEAD_DIM:(h + 1) * HEAD_DIM]
        k = kv_ref[:, g * HEAD_DIM:(g + 1) * HEAD_DIM].astype(BF16)
        v = kv_ref[:, KV_W + g * HEAD_DIM:KV_W + (g + 1) * HEAD_DIM].astype(BF16)
        s = lax.dot_general(q, k, (((1,), (1,)), ((), ())), preferred_element_type=F32) * SCALE
        sink = sink_ref[h]
        m = jnp.maximum(jnp.max(s, axis=-1, keepdims=True), sink)
        e = jnp.exp(s - m)
        den = jnp.sum(e, axis=-1, keepdims=True) + jnp.exp(sink - m)
        o = jnp.dot(e.astype(BF16), v, preferred_element_type=F32) / den
        o_ref[:, h * HEAD_DIM:(h + 1) * HEAD_DIM] = o.astype(BF16)
    _fourier_conv(fm_ref, ccsc_ref, csl_ref, cw_ref, o_ref)


def _ctx_mix(sink_l, q, kv, fm, ccsc, csl, conv_w_l):
    row = lambda b: (b, 0)
    const = lambda b: (0, 0)
    return pl.pallas_call(
        _ctx_mix_kernel,
        grid=(BATCH,),
        in_specs=[pl.BlockSpec(memory_space=pltpu.SMEM),
                  pl.BlockSpec((SEQ, ATTN_W), row),
                  pl.BlockSpec((SEQ, 2 * KV_W), row),
                  pl.BlockSpec((SEQ, FOURIER_W + 3 * CONV_W), row),
                  pl.BlockSpec((FOURIER_GROUP_W, 2 * FOURIER_GROUP_W), const),
                  pl.BlockSpec((SEQ, 2 * SEQ), const),
                  pl.BlockSpec((3, CONV_W), const)],
        out_specs=pl.BlockSpec((SEQ, D_MODEL), row),
        out_shape=jax.ShapeDtypeStruct((T_ALL, D_MODEL), BF16),
        compiler_params=_cparams(("arbitrary",)),
        name="ctx_mix",
    )(sink_l, q, kv, fm, ccsc, csl, conv_w_l)


def _lat_mix_kernel(sink_ref, q_ref, kv_ref, fm_ref, ck_ref, cv_ref, ccsc_ref, csl_ref, cw_ref,
                    mix_in_ref, o_ref):
    del mix_in_ref
    n = pl.program_id(1)
    nb = DEC_SEQ // BLOCK
    rows = Q_PER_KV * BLOCK
    band = 3 * BLOCK

    @pl.when(n == 0)
    def _():
        _fourier_conv(fm_ref, ccsc_ref, csl_ref, cw_ref, o_ref)

    q0 = pl.multiple_of(n * BLOCK, BLOCK)
    k0 = pl.multiple_of(jnp.clip(n - 1, 0, nb - 3) * BLOCK, BLOCK)
    qpos = q0 + lax.broadcasted_iota(I32, (rows, band), 0) % BLOCK
    kpos = k0 + lax.broadcasted_iota(I32, (rows, band), 1)
    valid = jnp.abs(qpos - kpos) <= WINDOW
    for g in range(N_KV_HEADS):
        kb = kv_ref[pl.ds(k0, band), g * HEAD_DIM:(g + 1) * HEAD_DIM].astype(BF16)
        vb = kv_ref[pl.ds(k0, band), KV_W + g * HEAD_DIM:KV_W + (g + 1) * HEAD_DIM].astype(BF16)
        k_ctx = ck_ref[0, 0, :, g * HEAD_DIM:(g + 1) * HEAD_DIM].astype(BF16)
        v_ctx = cv_ref[0, 0, :, g * HEAD_DIM:(g + 1) * HEAD_DIM].astype(BF16)
        sink = jnp.concatenate(
            [jnp.full((BLOCK, 1), sink_ref[g * Q_PER_KV + j], F32) for j in range(Q_PER_KV)], axis=0)
        q = jnp.concatenate(
            [q_ref[pl.ds(q0, BLOCK), (g * Q_PER_KV + j) * HEAD_DIM:(g * Q_PER_KV + j + 1) * HEAD_DIM]
             for j in range(Q_PER_KV)], axis=0)
        s_loc = lax.dot_general(q, kb, (((1,), (1,)), ((), ())), preferred_element_type=F32) * SCALE
        s_loc = jnp.where(valid, s_loc, NEG)
        s_ctx = lax.dot_general(q, k_ctx, (((1,), (1,)), ((), ())), preferred_element_type=F32) * SCALE
        m = jnp.maximum(jnp.maximum(jnp.max(s_loc, axis=-1, keepdims=True),
                                    jnp.max(s_ctx, axis=-1, keepdims=True)), sink)
        e_loc, e_ctx = jnp.exp(s_loc - m), jnp.exp(s_ctx - m)
        den = (jnp.sum(e_loc, axis=-1, keepdims=True) + jnp.sum(e_ctx, axis=-1, keepdims=True)
               + jnp.exp(sink - m))
        o = (jnp.dot(e_loc.astype(BF16), vb, preferred_element_type=F32)
             + jnp.dot(e_ctx.astype(BF16), v_ctx, preferred_element_type=F32)) / den
        for j in range(Q_PER_KV):
            h = g * Q_PER_KV + j
            o_ref[pl.ds(q0, BLOCK), h * HEAD_DIM:(h + 1) * HEAD_DIM] = o[j * BLOCK:(j + 1) * BLOCK].astype(BF16)


def _lat_mix(layer, sink_l, q, kv, fm, cache_k, cache_v, ccsc, csl, conv_w_l, mix):
    off = T_CTX // DEC_SEQ
    row = lambda b, n: (off + b, 0)
    const = lambda b, n: (0, 0)
    cache = lambda b, n: (b, layer, 0, 0)
    return pl.pallas_call(
        _lat_mix_kernel,
        grid=(DEC_BATCH, DEC_SEQ // BLOCK),
        in_specs=[pl.BlockSpec(memory_space=pltpu.SMEM),
                  pl.BlockSpec((DEC_SEQ, ATTN_W), row),
                  pl.BlockSpec((DEC_SEQ, 2 * KV_W), row),
                  pl.BlockSpec((DEC_SEQ, FOURIER_W + 3 * CONV_W), row),
                  pl.BlockSpec((1, 1, SEQ, KV_W), cache),
                  pl.BlockSpec((1, 1, SEQ, KV_W), cache),
                  pl.BlockSpec((FOURIER_GROUP_W, 2 * FOURIER_GROUP_W), const),
                  pl.BlockSpec((DEC_SEQ, 2 * DEC_SEQ), const, pipeline_mode=pl.Buffered(1)),
                  pl.BlockSpec((3, CONV_W), const),
                  pl.BlockSpec(memory_space=pl.ANY)],
        out_specs=pl.BlockSpec((DEC_SEQ, D_MODEL), row),
        out_shape=jax.ShapeDtypeStruct((T_ALL, D_MODEL), BF16),
        input_output_aliases={9: 0},
        compiler_params=_cparams(("arbitrary", "arbitrary")),
        name="lat_mix",
    )(sink_l, q, kv, fm, cache_k, cache_v, ccsc, csl, conv_w_l, mix)


def _first_index_of_max(vals, iota, sentinel):
    mx = jnp.max(vals, axis=0, keepdims=True)
    idx = jnp.min(jnp.where(vals == mx, iota, sentinel), axis=0, keepdims=True)
    return mx, idx


def _outproj_kernel(mix_ref, x_ref, mod_ref, g_ref, w_ref, wr_ref, rb_ref, tri_ref,
                    x1_ref, h2_ref, ids_ref, wts_ref, rank_ref, cnt_ref, run_ref):
    i = pl.program_id(0)

    @pl.when(i == 0)
    def _():
        run_ref[...] = jnp.zeros_like(run_ref)

    gate1, shift2, scale2 = mod_ref[0, 2:3, :], mod_ref[0, 3:4, :], mod_ref[0, 4:5, :]
    x1 = x_ref[...] + gate1 * jnp.dot(mix_ref[...], w_ref[...], preferred_element_type=F32)
    x1_ref[...] = x1
    h2 = _rmsnorm(x1, g_ref[...]) * (1.0 + scale2) + shift2
    h2_ref[...] = h2

    logits = lax.dot_general(wr_ref[...], h2, (((1,), (1,)), ((), ())),
                             precision=lax.Precision.HIGHEST, preferred_element_type=F32)
    scores = _sigmoid(logits)
    biased = scores + rb_ref[...]
    ninf = -jnp.inf
    mem = lax.broadcasted_iota(I32, (EXPERTS_PER_GROUP, TM), 0)
    grp_rows = []
    for g in range(N_GROUPS):
        bgp = biased[g * EXPERTS_PER_GROUP:(g + 1) * EXPERTS_PER_GROUP]
        m1, i1 = _first_index_of_max(bgp, mem, EXPERTS_PER_GROUP)
        m2 = jnp.max(jnp.where(mem == i1, ninf, bgp), axis=0, keepdims=True)
        grp_rows.append(m1 + m2)
    cur = jnp.concatenate(grp_rows, axis=0)
    gio = lax.broadcasted_iota(I32, (N_GROUPS, TM), 0)
    gsel = jnp.zeros((N_GROUPS, TM), F32)
    for _ in range(TOPK_GROUPS):
        _, gi = _first_index_of_max(cur, gio, N_GROUPS)
        hit = gio == gi
        gsel = jnp.where(hit, 1.0, gsel)
        cur = jnp.where(hit, ninf, cur)
    cur = jnp.concatenate(
        [jnp.where(gsel[g:g + 1] > 0.0, biased[g * EXPERTS_PER_GROUP:(g + 1) * EXPERTS_PER_GROUP], ninf)
         for g in range(N_GROUPS)], axis=0)
    eio = lax.broadcasted_iota(I32, (N_EXPERTS, TM), 0)
    chosen = jnp.zeros((N_EXPERTS, TM), F32)
    ids, sel = [], []
    for _ in range(TOP_K):
        _, ei = _first_index_of_max(cur, eio, N_EXPERTS)
        hit = eio == ei
        ids.append(ei)
        sel.append(jnp.sum(jnp.where(hit, scores, 0.0), axis=0, keepdims=True))
        chosen = jnp.where(hit, 1.0, chosen)
        cur = jnp.where(hit, ninf, cur)
    total = functools.reduce(lambda a, b: a + b, sel)
    ids_ref[...] = jnp.concatenate(ids, axis=0)
    wts_ref[...] = jnp.concatenate([s / total * ROUTE_SCALE for s in sel], axis=0)

    before = jnp.dot(chosen.astype(BF16), tri_ref[...], preferred_element_type=F32) + run_ref[:, 0:1]
    rank_ref[...] = jnp.concatenate(
        [jnp.sum(jnp.where(eio == ei, before, 0.0), axis=0, keepdims=True) for ei in ids], axis=0).astype(I32)
    run = run_ref[...] + jnp.sum(chosen, axis=1, keepdims=True)
    run_ref[...] = run
    cnt_ref[...] = run


def _outproj(mix, x, mods_l, g_ffn_l, w_out_bf, w_router_t, router_bias_l, tri):
    row = lambda i: (i, 0)
    col = lambda i: (0, i)
    const = lambda i: (0, 0)
    return pl.pallas_call(
        _outproj_kernel,
        grid=(N_TILES,),
        in_specs=[pl.BlockSpec((TM, D_MODEL), row),
                  pl.BlockSpec((TM, D_MODEL), row),
                  pl.BlockSpec((1, N_MOD, D_MODEL), lambda i: (_cond_row(i), 0, 0)),
                  pl.BlockSpec((1, D_MODEL), const),
                  pl.BlockSpec((D_MODEL, D_MODEL), const, pipeline_mode=pl.Buffered(1)),
                  pl.BlockSpec((N_EXPERTS, D_MODEL), const),
                  pl.BlockSpec((N_EXPERTS, 1), const),
                  pl.BlockSpec((TM, TM), const)],
        out_specs=[pl.BlockSpec((TM, D_MODEL), row),
                   pl.BlockSpec((TM, D_MODEL), row),
                   pl.BlockSpec((TOP_K, TM), col),
                   pl.BlockSpec((TOP_K, TM), col),
                   pl.BlockSpec((TOP_K, TM), col),
                   pl.BlockSpec((N_EXPERTS, META_LANES), const)],
        out_shape=[jax.ShapeDtypeStruct((T_ALL, D_MODEL), F32),
                   jax.ShapeDtypeStruct((T_ALL, D_MODEL), F32),
                   jax.ShapeDtypeStruct((TOP_K, T_ALL), I32),
                   jax.ShapeDtypeStruct((TOP_K, T_ALL), F32),
                   jax.ShapeDtypeStruct((TOP_K, T_ALL), I32),
                   jax.ShapeDtypeStruct((N_EXPERTS, META_LANES), F32)],
        scratch_shapes=[pltpu.VMEM((N_EXPERTS, META_LANES), F32)],
        compiler_params=_cparams(("arbitrary",)),
        name="outproj",
    )(mix, x, mods_l, g_ffn_l.reshape(1, D_MODEL), w_out_bf, w_router_t,
      router_bias_l.reshape(N_EXPERTS, 1), tri)


def _positions_kernel(ids_ref, rank_ref, cnt_ref, pos_ref, te_ref, meta_ref):
    ids = ids_ref[...]
    pos = rank_ref[...]
    tile_io = lax.broadcasted_iota(I32, (1, TE_LANES), 1)
    lane_io = lax.broadcasted_iota(I32, (1, META_LANES), 1)
    te = jnp.zeros((1, TE_LANES), I32)
    starts = jnp.zeros((1, META_LANES), I32)
    counts = jnp.zeros((1, META_LANES), I32)
    start = jnp.zeros((1, 1), I32)
    for e in range(N_EXPERTS):
        c = cnt_ref[e:e + 1, 0:1].astype(I32)
        pos = pos + jnp.where(ids == e, start, 0)
        starts = starts + jnp.where(lane_io == e, start, 0)
        counts = counts + jnp.where(lane_io == e, c, 0)
        start = start + ((c + (TM - 1)) // TM) * TM
        te = te + jnp.where(tile_io * TM >= start, 1, 0)
    pos_ref[...] = pos
    te_ref[...] = jnp.minimum(te, N_EXPERTS - 1)
    meta_ref[0:1, :] = starts
    meta_ref[1:2, :] = counts
    meta_ref[2:3, :] = jnp.broadcast_to(start // TM, (1, META_LANES))
    meta_ref[3:8, :] = jnp.zeros((5, META_LANES), I32)


def _positions(ids, rank, cnt):
    return pl.pallas_call(
        _positions_kernel,
        out_shape=[jax.ShapeDtypeStruct((TOP_K, T_ALL), I32),
                   jax.ShapeDtypeStruct((1, TE_LANES), I32),
                   jax.ShapeDtypeStruct((8, META_LANES), I32)],
        compiler_params=_cparams(),
        name="positions",
    )(ids, rank, cnt)


def _dispatch_kernel(pos_ref, meta_ref, h_ref, xs_ref, zero_ref, sem, pad_sem):
    i = pl.program_id(0)

    def row_copy(t, k):
        return pltpu.make_async_copy(h_ref.at[pl.ds(t, 1)], xs_ref.at[pl.ds(pos_ref[k, t], 1)], sem)

    def issue(t, c):
        for k in range(TOP_K):
            row_copy(t, k).start()
        return c

    def drain(t, c):
        for k in range(TOP_K):
            row_copy(t, k).wait()
        return c

    lax.fori_loop(0, TM, issue, 0)
    lax.fori_loop(0, TM, drain, 0)

    @pl.when(i == N_TILES - 1)
    def _():
        zero_ref[...] = jnp.zeros_like(zero_ref)

        def per_expert(e, c):
            count = meta_ref[1, e]
            first = meta_ref[0, e] + count
            npad = ((count + (TM - 1)) // TM) * TM - count

            def pad_copy(r):
                return pltpu.make_async_copy(zero_ref.at[pl.ds(0, 1)], xs_ref.at[pl.ds(first + r, 1)], pad_sem)

            def pad_issue(r, c2):
                pad_copy(r).start()
                return c2

            def pad_drain(r, c2):
                pad_copy(r).wait()
                return c2

            lax.fori_loop(0, npad, pad_issue, 0)
            lax.fori_loop(0, npad, pad_drain, 0)
            return c

        lax.fori_loop(0, N_EXPERTS, per_expert, 0)


def _dispatch(pos, meta, h2):
    return pl.pallas_call(
        _dispatch_kernel,
        grid=(N_TILES,),
        in_specs=[pl.BlockSpec((TOP_K, TM), lambda i: (0, i), memory_space=pltpu.SMEM),
                  pl.BlockSpec(memory_space=pltpu.SMEM),
                  pl.BlockSpec((TM, D_MODEL), lambda i: (i, 0))],
        out_specs=pl.BlockSpec(memory_space=pl.ANY),
        out_shape=jax.ShapeDtypeStruct((P_SORT, D_MODEL), F32),
        scratch_shapes=[pltpu.VMEM((8, D_MODEL), F32),
                        pltpu.SemaphoreType.DMA(()),
                        pltpu.SemaphoreType.DMA(())],
        compiler_params=_cparams(("arbitrary",)),
        name="dispatch",
    )(pos, meta, h2)


def _experts_kernel(te_ref, nt_ref, x_ref, wg_ref, wu_ref, wd_ref, y_ref, wg_bf, wu_bf, wd_bf):
    j = pl.program_id(0)
    used = j < nt_ref[0]
    prev = te_ref[jnp.maximum(j - 1, 0)]

    @pl.when(used & ((j == 0) | (te_ref[j] != prev)))
    def _():
        wg_bf[...] = wg_ref[0].astype(BF16)
        wu_bf[...] = wu_ref[0].astype(BF16)
        wd_bf[...] = wd_ref[0].astype(BF16)

    @pl.when(used)
    def _():
        x = x_ref[...].astype(BF16)
        g = jnp.dot(x, wg_bf[...], preferred_element_type=F32)
        u = jnp.dot(x, wu_bf[...], preferred_element_type=F32)
        a = (_silu(g) * u).astype(BF16)
        y_ref[...] = jnp.dot(a, wd_bf[...], preferred_element_type=F32)


def _experts(te, nt, xs, w_gate, w_up, w_down):
    n_tiles = xs.shape[0] // TM
    tile = lambda j, te, nt: (jnp.minimum(j, nt[0] - 1), 0)
    wsel = lambda j, te, nt: (te[jnp.minimum(j, nt[0] - 1)], 0, 0)
    return pl.pallas_call(
        _experts_kernel,
        grid_spec=pltpu.PrefetchScalarGridSpec(
            num_scalar_prefetch=2,
            grid=(n_tiles,),
            in_specs=[pl.BlockSpec((TM, D_MODEL), tile),
                      pl.BlockSpec((1, D_MODEL, EXPERT_FF), wsel),
                      pl.BlockSpec((1, D_MODEL, EXPERT_FF), wsel),
                      pl.BlockSpec((1, EXPERT_FF, D_MODEL), wsel)],
            out_specs=pl.BlockSpec((TM, D_MODEL), tile),
            scratch_shapes=[pltpu.VMEM((D_MODEL, EXPERT_FF), BF16),
                            pltpu.VMEM((D_MODEL, EXPERT_FF), BF16),
                            pltpu.VMEM((EXPERT_FF, D_MODEL), BF16)]),
        out_shape=jax.ShapeDtypeStruct(xs.shape, F32),
        compiler_params=_cparams(("arbitrary",)),
        name="experts",
    )(te, nt, xs, w_gate, w_up, w_down)


def _combine_kernel(final, pos_ref, wts_ref, x1_ref, ysh_ref, mod_ref, g_ref, y_ref, o_ref, buf_ref, sem):
    def row_copy(t, k):
        return pltpu.make_async_copy(y_ref.at[pl.ds(pos_ref[k, t], 1)], buf_ref.at[k, pl.ds(t, 1)], sem)

    def issue(t, c):
        for k in range(TOP_K):
            row_copy(t, k).start()
        return c

    def drain(t, c):
        for k in range(TOP_K):
            row_copy(t, k).wait()
        return c

    lax.fori_loop(0, TC_ROWS, issue, 0)
    lax.fori_loop(0, TC_ROWS, drain, 0)
    w = jnp.transpose(wts_ref[...])
    acc = ysh_ref[...]
    for k in range(TOP_K):
        acc = acc + w[:, k:k + 1] * buf_ref[k]
    x2 = x1_ref[...] + mod_ref[0, 5:6, :] * acc
    o_ref[...] = _rmsnorm(x2, g_ref[...]) if final else x2


def _combine(final, pos, wts, x1, ysh, mods_l, final_norm, y):
    row = lambda i: (i, 0)
    col = lambda i: (0, i)
    per_tm = TM // TC_ROWS
    return pl.pallas_call(
        functools.partial(_combine_kernel, final),
        grid=(T_ALL // TC_ROWS,),
        in_specs=[pl.BlockSpec((TOP_K, TC_ROWS), col, memory_space=pltpu.SMEM),
                  pl.BlockSpec((TOP_K, TC_ROWS), col),
                  pl.BlockSpec((TC_ROWS, D_MODEL), row),
                  pl.BlockSpec((TC_ROWS, D_MODEL), row),
                  pl.BlockSpec((1, N_MOD, D_MODEL), lambda i: (_cond_row(i // per_tm), 0, 0)),
                  pl.BlockSpec((1, D_MODEL), lambda i: (0, 0)),
                  pl.BlockSpec(memory_space=pl.ANY)],
        out_specs=pl.BlockSpec((TC_ROWS, D_MODEL), row),
        out_shape=jax.ShapeDtypeStruct((T_ALL, D_MODEL), F32),
        scratch_shapes=[pltpu.VMEM((TOP_K, TC_ROWS, D_MODEL), F32),
                        pltpu.SemaphoreType.DMA(())],
        compiler_params=_cparams(("arbitrary",)),
        name="combine",
    )(pos, wts, x1, ysh, mods_l, final_norm.reshape(1, D_MODEL), y)


def kernel(x_prompt, x_sample, cache_k, cache_v, c, c_ctx, w_mod, b_mod, g_mix, g_ffn, w_in, w_out, conv_w,
           attn_sink, w_router, router_bias, w_gate_e, w_up_e, w_down_e, w_gate_s, w_up_s, w_down_s, final_norm):
    ccsc = jnp.asarray(_channel_dft(), BF16)
    csl_ctx = jnp.asarray(_dft_tables(SEQ), BF16)
    csl_lat = jnp.asarray(_dft_tables(DEC_SEQ), BF16)
    rope = tuple(jnp.asarray(t, F32) for t in _rope_tables(DEC_SEQ))
    tri = jnp.asarray(np.triu(np.ones((TM, TM)), 1), BF16)
    shared_te = jnp.zeros((N_TILES,), I32)
    shared_nt = jnp.full((1,), N_TILES, I32)

    cond = jnp.concatenate([c_ctx[None], c, jnp.zeros((N_COND - 1 - DEC_BATCH, D_MODEL), F32)], axis=0)
    mods = _mods(cond, w_mod, b_mod).reshape(DEPTH, N_COND, N_MOD, D_MODEL)
    cache_k = cache_k.reshape(DEC_BATCH, DEPTH, SEQ, KV_W)
    cache_v = cache_v.reshape(DEC_BATCH, DEPTH, SEQ, KV_W)
    x = jnp.concatenate([x_prompt.reshape(T_CTX, D_MODEL), x_sample.reshape(T_LAT, D_MODEL)], axis=0)

    new_k, new_v = [], []
    for l in range(DEPTH):
        q, kv, fm = _inproj(x, mods[l], g_mix[l], w_in[l].astype(BF16), rope)
        new_k.append(kv[:T_CTX, :KV_W].reshape(BATCH, SEQ, N_KV_HEADS, HEAD_DIM))
        new_v.append(kv[:T_CTX, KV_W:].reshape(BATCH, SEQ, N_KV_HEADS, HEAD_DIM))
        mix = _ctx_mix(attn_sink[l], q, kv, fm, ccsc, csl_ctx, conv_w[l])
        mix = _lat_mix(l, attn_sink[l], q, kv, fm, cache_k, cache_v, ccsc, csl_lat, conv_w[l], mix)
        x1, h2, ids, wts, rank, cnt = _outproj(mix, x, mods[l], g_ffn[l], w_out[l].astype(BF16),
                                               w_router[l].T, router_bias[l], tri)
        pos, te, meta = _positions(ids, rank, cnt)
        xs = _dispatch(pos, meta, h2)
        y = _experts(te[0, :N_SORT_TILES], meta[2, :1], xs, w_gate_e[l], w_up_e[l], w_down_e[l])
        ysh = _experts(shared_te, shared_nt, h2, w_gate_s[l][None], w_up_s[l][None], w_down_s[l][None])
        x = _combine(l == DEPTH - 1, pos, wts, x1, ysh, mods[l], final_norm, y)

    y_prompt = x[:T_CTX].reshape(BATCH, SEQ, D_MODEL)
    y_sample = x[T_CTX:].reshape(DEC_BATCH, DEC_SEQ, D_MODEL)
    return y_prompt, y_sample, jnp.stack(new_k, axis=1), jnp.stack(new_v, axis=1)
```

```python
import functools

import numpy as np
import jax
import jax.numpy as jnp
from jax import lax
from jax.experimental import pallas as pl
from jax.experimental.pallas import tpu as pltpu

F32 = jnp.float32
BF16 = jnp.bfloat16
I32 = jnp.int32
U32 = jnp.uint32

D_MODEL = 2048
HALF = D_MODEL // 2
BATCH, SEQ = 32, 256
DEC_BATCH, DEC_SEQ = 4, 1024
DEPTH = 2
GRID_W = 64
HEAD_DIM = 128
N_HEADS, N_KV_HEADS, Q_PER_KV = 8, 2, 4
ATTN_W, KV_W = 1024, 256
WINDOW, BLOCK = 128, 128
SCALE = HEAD_DIM ** -0.5
ROPE_BASE = 10000.0
ROPE_PAIRS = 32
FOURIER_W, FOURIER_GROUPS, FOURIER_GROUP_W = 512, 4, 128
CONV_W = 512
IN_W = 3584
N_EXPERTS, TOP_K, N_GROUPS, EXPERTS_PER_GROUP, TOPK_GROUPS = 64, 8, 8, 8, 4
EXPERT_FF = 512
ROUTE_SCALE = 2.5
N_MOD = 6
EPS = 1e-6
NEG = -1e30

T_CTX = BATCH * SEQ
T_LAT = DEC_BATCH * DEC_SEQ
T_ALL = T_CTX + T_LAT
TR = 512
N_CTX_TILES = T_CTX // TR
LAT_TILES_PER_SEQ = DEC_SEQ // TR
N_TILES = T_ALL // TR
N_COND = 8
MOD_TN = 1024
TM = 256
TC_ROWS = 128
N_SORT_TILES = T_ALL * TOP_K // TM + N_EXPERTS
P_SORT = N_SORT_TILES * TM
META_LANES = 128
TE_LANES = 512
VMEM_LIMIT = 56 * 1024 * 1024


def _cparams(sem=None):
    return pltpu.CompilerParams(dimension_semantics=sem, vmem_limit_bytes=VMEM_LIMIT)


def _silu(x):
    return x / (1.0 + jnp.exp(-x))


def _sigmoid(x):
    return 1.0 / (1.0 + jnp.exp(-x))


def _rmsnorm(x, g):
    return x * lax.rsqrt(jnp.mean(x * x, axis=-1, keepdims=True) + EPS) * g


def _cond_row(i):
    return jnp.where(i < N_CTX_TILES, 0, 1 + (i - N_CTX_TILES) // LAT_TILES_PER_SEQ)


def _pack_halves(v):
    n = v.shape[1] // 2
    hi = pltpu.bitcast(v[:, :n].astype(BF16).astype(F32), U32)
    lo = pltpu.bitcast(v[:, n:].astype(BF16).astype(F32), U32)
    return hi | (lo >> 16)


def _unpack_halves(w):
    return pltpu.bitcast(w & jnp.uint32(0xFFFF0000), F32), pltpu.bitcast(w << 16, F32)


def _dft_tables(L):
    j = np.arange(L, dtype=np.int64)
    ang = 2.0 * np.pi * ((j[:, None] * j[None, :]) % L).astype(np.float64) / L
    c, s = np.cos(ang) / np.sqrt(L), np.sin(ang) / np.sqrt(L)
    return np.concatenate([c, -s], axis=1)


def _channel_dft():
    n = FOURIER_GROUP_W
    j = np.arange(n, dtype=np.int64)
    ang = 2.0 * np.pi * ((j[:, None] * j[None, :]) % n).astype(np.float64) / n
    return np.concatenate([np.cos(ang), np.sin(ang)], axis=1) / np.sqrt(n)


def _rope_tables(S):
    pos = np.arange(S)
    row, col = (pos // GRID_W).astype(np.float64), (pos % GRID_W).astype(np.float64)
    inv = ROPE_BASE ** (-np.arange(ROPE_PAIRS, dtype=np.float64) / ROPE_PAIRS)
    ar, ac = row[:, None] * inv, col[:, None] * inv
    z = np.zeros_like(ar)
    cos = np.concatenate([np.cos(ar), np.cos(ar), np.cos(ac), np.cos(ac)], axis=1)
    sin_hi = np.concatenate([-np.sin(ar), z, -np.sin(ac), z], axis=1)
    sin_lo = np.concatenate([z, np.sin(ar), z, np.sin(ac)], axis=1)
    return cos, sin_hi, sin_lo


def _mods_kernel(cond_ref, w_ref, b_ref, o_ref):
    s = _silu(cond_ref[...]).astype(BF16)
    o_ref[0] = jnp.dot(s, w_ref[0].astype(BF16), preferred_element_type=F32) + b_ref[0]


def _mods(cond, w_mod, b_mod):
    n = N_MOD * D_MODEL
    return pl.pallas_call(
        _mods_kernel,
        grid=(DEPTH, n // MOD_TN),
        in_specs=[pl.BlockSpec((N_COND, D_MODEL), lambda l, j: (0, 0)),
                  pl.BlockSpec((1, D_MODEL, MOD_TN), lambda l, j: (l, 0, j)),
                  pl.BlockSpec((1, 1, MOD_TN), lambda l, j: (l, 0, j))],
        out_specs=pl.BlockSpec((1, N_COND, MOD_TN), lambda l, j: (l, 0, j)),
        out_shape=jax.ShapeDtypeStruct((DEPTH, N_COND, n), F32),
        compiler_params=_cparams(("arbitrary", "arbitrary")),
        name="mods",
    )(cond, w_mod, b_mod.reshape(DEPTH, 1, n))


def _mod_spec(layer, tiles_per_tr=1):
    return pl.BlockSpec((1, 1, N_MOD, D_MODEL), lambda i: (layer, _cond_row(i // tiles_per_tr), 0, 0))


def _inproj_kernel(x_ref, mod_ref, g_ref, w_ref, cos_ref, shi_ref, slo_ref, q_ref, kv_ref, fm_ref):
    i = pl.program_id(0)
    shift, scale = mod_ref[0, 0, 0:1, :], mod_ref[0, 0, 1:2, :]
    h = (_rmsnorm(x_ref[...], g_ref[0]) * (1.0 + scale) + shift).astype(BF16)
    fm_ref[...] = jnp.dot(h, w_ref[0, :, ATTN_W + 2 * KV_W:], preferred_element_type=F32).astype(BF16)
    qkv = jnp.dot(h, w_ref[0, :, :ATTN_W + 2 * KV_W], preferred_element_type=F32)
    kv_ref[:, KV_W:] = qkv[:, ATTN_W + KV_W:]

    @pl.when(i < N_CTX_TILES)
    def _():
        q_ref[...] = qkv[:, :ATTN_W].astype(BF16)
        kv_ref[:, :KV_W] = qkv[:, ATTN_W:ATTN_W + KV_W]

    @pl.when(i >= N_CTX_TILES)
    def _():
        cos, shi, slo = cos_ref[...], shi_ref[...], slo_ref[...]
        for hd in range(N_HEADS + N_KV_HEADS):
            xh = qkv[:, hd * HEAD_DIM:(hd + 1) * HEAD_DIM]
            r = xh * cos + pltpu.roll(xh, HEAD_DIM - 32, 1) * shi + pltpu.roll(xh, 32, 1) * slo
            if hd < N_HEADS:
                q_ref[:, hd * HEAD_DIM:(hd + 1) * HEAD_DIM] = r.astype(BF16)
            else:
                kv_ref[:, (hd - N_HEADS) * HEAD_DIM:(hd - N_HEADS + 1) * HEAD_DIM] = r


def _inproj(layer, x, mods, g_mix, w_in_bf, rope):
    lat_blk = lambda i: (jnp.maximum(i - N_CTX_TILES, 0) % LAT_TILES_PER_SEQ, 0)
    row = lambda i: (i, 0)
    return pl.pallas_call(
        _inproj_kernel,
        grid=(N_TILES,),
        in_specs=[pl.BlockSpec((TR, D_MODEL), row),
                  _mod_spec(layer),
                  pl.BlockSpec((1, 1, D_MODEL), lambda i: (layer, 0, 0)),
                  pl.BlockSpec((1, D_MODEL, IN_W), lambda i: (layer, 0, 0), pipeline_mode=pl.Buffered(1)),
                  pl.BlockSpec((TR, HEAD_DIM), lat_blk),
                  pl.BlockSpec((TR, HEAD_DIM), lat_blk),
                  pl.BlockSpec((TR, HEAD_DIM), lat_blk)],
        out_specs=[pl.BlockSpec((TR, ATTN_W), row),
                   pl.BlockSpec((TR, 2 * KV_W), row),
                   pl.BlockSpec((TR, FOURIER_W + 3 * CONV_W), row)],
        out_shape=[jax.ShapeDtypeStruct((T_ALL, ATTN_W), BF16),
                   jax.ShapeDtypeStruct((T_ALL, 2 * KV_W), F32),
                   jax.ShapeDtypeStruct((T_ALL, FOURIER_W + 3 * CONV_W), BF16)],
        compiler_params=_cparams(("arbitrary",)),
        name="inproj",
    )(x, mods, g_mix.reshape(DEPTH, 1, D_MODEL), w_in_bf, *rope)


def _fourier_conv(fm_ref, ccsc_ref, csl_ref, cw_ref, o_ref):
    L = fm_ref.shape[0]
    for g in range(FOURIER_GROUPS):
        xg = fm_ref[:, g * FOURIER_GROUP_W:(g + 1) * FOURIER_GROUP_W]
        z = jnp.dot(xg, ccsc_ref[...], preferred_element_type=F32)
        zz = jnp.concatenate([z[:, :FOURIER_GROUP_W], z[:, FOURIER_GROUP_W:]], axis=0).astype(BF16)
        f = jnp.dot(csl_ref[...], zz, preferred_element_type=F32)
        o_ref[:, ATTN_W + g * FOURIER_GROUP_W:ATTN_W + (g + 1) * FOURIER_GROUP_W] = f.astype(BF16)
    xc = fm_ref[:, FOURIER_W:FOURIER_W + CONV_W].astype(F32)
    bg = fm_ref[:, FOURIER_W + CONV_W:FOURIER_W + 2 * CONV_W].astype(F32)
    cg = fm_ref[:, FOURIER_W + 2 * CONV_W:].astype(F32)
    u = cg * xc
    rows = lax.broadcasted_iota(I32, u.shape, 0)
    prev = jnp.where(rows == 0, 0.0, pltpu.roll(u, 1, 0))
    nxt = jnp.where(rows == L - 1, 0.0, pltpu.roll(u, L - 1, 0))
    y = prev * cw_ref[0, 0:1, :] + u * cw_ref[0, 1:2, :] + nxt * cw_ref[0, 2:3, :]
    o_ref[:, ATTN_W + FOURIER_W:] = (bg * y).astype(BF16)


def _ctx_mix_kernel(layer, sink_ref, q_ref, kv_ref, fm_ref, ccsc_ref, csl_ref, cw_ref, o_ref):
    for h in range(N_HEADS):
        g = h // Q_PER_KV
        q = q_ref[:, h * HEAD_DIM:(h + 1) * HEAD_DIM]
        k = kv_ref[:, g * HEAD_DIM:(g + 1) * HEAD_DIM].astype(BF16)
        v = kv_ref[:, KV_W + g * HEAD_DIM:KV_W + (g + 1) * HEAD_DIM].astype(BF16)
        s = lax.dot_general(q, k, (((1,), (1,)), ((), ())), preferred_element_type=F32) * SCALE
        sink = sink_ref[layer, h]
        m = jnp.maximum(jnp.max(s, axis=-1, keepdims=True), sink)
        e = jnp.exp(s - m)
        den = jnp.sum(e, axis=-1, keepdims=True) + jnp.exp(sink - m)
        o = jnp.dot(e.astype(BF16), v, preferred_element_type=F32) / den
        o_ref[:, h * HEAD_DIM:(h + 1) * HEAD_DIM] = o.astype(BF16)
    _fourier_conv(fm_ref, ccsc_ref, csl_ref, cw_ref, o_ref)


def _ctx_mix(layer, sink, q, kv, fm, ccsc, csl, conv_w):
    row = lambda b: (b, 0)
    const = lambda b: (0, 0)
    return pl.pallas_call(
        functools.partial(_ctx_mix_kernel, layer),
        grid=(BATCH,),
        in_specs=[pl.BlockSpec(memory_space=pltpu.SMEM),
                  pl.BlockSpec((SEQ, ATTN_W), row),
                  pl.BlockSpec((SEQ, 2 * KV_W), row),
                  pl.BlockSpec((SEQ, FOURIER_W + 3 * CONV_W), row),
                  pl.BlockSpec((FOURIER_GROUP_W, 2 * FOURIER_GROUP_W), const),
                  pl.BlockSpec((SEQ, 2 * SEQ), const),
                  pl.BlockSpec((1, 3, CONV_W), lambda b: (layer, 0, 0))],
        out_specs=pl.BlockSpec((SEQ, D_MODEL), row),
        out_shape=jax.ShapeDtypeStruct((T_CTX, D_MODEL), BF16),
        compiler_params=_cparams(("arbitrary",)),
        name="ctx_mix",
    )(sink, q, kv, fm, ccsc, csl, conv_w)


def _lat_mix_kernel(layer, sink_ref, q_ref, kv_ref, fm_ref, ck_ref, cv_ref, ccsc_ref, csl_ref, cw_ref, o_ref):
    n = pl.program_id(1)
    nb = DEC_SEQ // BLOCK
    rows = Q_PER_KV * BLOCK
    band = 3 * BLOCK

    @pl.when(n == 0)
    def _():
        _fourier_conv(fm_ref, ccsc_ref, csl_ref, cw_ref, o_ref)

    q0 = pl.multiple_of(n * BLOCK, BLOCK)
    k0 = pl.multiple_of(jnp.clip(n - 1, 0, nb - 3) * BLOCK, BLOCK)
    qpos = q0 + lax.broadcasted_iota(I32, (rows, band), 0) % BLOCK
    kpos = k0 + lax.broadcasted_iota(I32, (rows, band), 1)
    valid = jnp.abs(qpos - kpos) <= WINDOW
    for g in range(N_KV_HEADS):
        kb = kv_ref[pl.ds(k0, band), g * HEAD_DIM:(g + 1) * HEAD_DIM].astype(BF16)
        vb = kv_ref[pl.ds(k0, band), KV_W + g * HEAD_DIM:KV_W + (g + 1) * HEAD_DIM].astype(BF16)
        k_ctx = ck_ref[0, 0, :, g * HEAD_DIM:(g + 1) * HEAD_DIM].astype(BF16)
        v_ctx = cv_ref[0, 0, :, g * HEAD_DIM:(g + 1) * HEAD_DIM].astype(BF16)
        sink = jnp.concatenate(
            [jnp.full((BLOCK, 1), sink_ref[layer, g * Q_PER_KV + j], F32) for j in range(Q_PER_KV)], axis=0)
        q = jnp.concatenate(
            [q_ref[pl.ds(q0, BLOCK), (g * Q_PER_KV + j) * HEAD_DIM:(g * Q_PER_KV + j + 1) * HEAD_DIM]
             for j in range(Q_PER_KV)], axis=0)
        s_loc = lax.dot_general(q, kb, (((1,), (1,)), ((), ())), preferred_element_type=F32) * SCALE
        s_loc = jnp.where(valid, s_loc, NEG)
        s_ctx = lax.dot_general(q, k_ctx, (((1,), (1,)), ((), ())), preferred_element_type=F32) * SCALE
        m = jnp.maximum(jnp.maximum(jnp.max(s_loc, axis=-1, keepdims=True),
                                    jnp.max(s_ctx, axis=-1, keepdims=True)), sink)
        e_loc, e_ctx = jnp.exp(s_loc - m), jnp.exp(s_ctx - m)
        den = (jnp.sum(e_loc, axis=-1, keepdims=True) + jnp.sum(e_ctx, axis=-1, keepdims=True)
               + jnp.exp(sink - m))
        o = (jnp.dot(e_loc.astype(BF16), vb, preferred_element_type=F32)
             + jnp.dot(e_ctx.astype(BF16), v_ctx, preferred_element_type=F32)) / den
        for j in range(Q_PER_KV):
            h = g * Q_PER_KV + j
            o_ref[pl.ds(q0, BLOCK), h * HEAD_DIM:(h + 1) * HEAD_DIM] = o[j * BLOCK:(j + 1) * BLOCK].astype(BF16)


def _lat_mix(layer, sink, q, kv, fm, cache_k, cache_v, ccsc, csl, conv_w):
    off = T_CTX // DEC_SEQ
    row = lambda b, n: (off + b, 0)
    const = lambda b, n: (0, 0)
    cache = lambda b, n: (b, layer, 0, 0)
    return pl.pallas_call(
        functools.partial(_lat_mix_kernel, layer),
        grid=(DEC_BATCH, DEC_SEQ // BLOCK),
        in_specs=[pl.BlockSpec(memory_space=pltpu.SMEM),
                  pl.BlockSpec((DEC_SEQ, ATTN_W), row),
                  pl.BlockSpec((DEC_SEQ, 2 * KV_W), row),
                  pl.BlockSpec((DEC_SEQ, FOURIER_W + 3 * CONV_W), row),
                  pl.BlockSpec((1, 1, SEQ, KV_W), cache),
                  pl.BlockSpec((1, 1, SEQ, KV_W), cache),
                  pl.BlockSpec((FOURIER_GROUP_W, 2 * FOURIER_GROUP_W), const),
                  pl.BlockSpec((DEC_SEQ, 2 * DEC_SEQ), const, pipeline_mode=pl.Buffered(1)),
                  pl.BlockSpec((1, 3, CONV_W), lambda b, n: (layer, 0, 0))],
        out_specs=pl.BlockSpec((DEC_SEQ, D_MODEL), lambda b, n: (b, 0)),
        out_shape=jax.ShapeDtypeStruct((T_LAT, D_MODEL), BF16),
        compiler_params=_cparams(("arbitrary", "arbitrary")),
        name="lat_mix",
    )(sink, q, kv, fm, cache_k, cache_v, ccsc, csl, conv_w)


def _first_index_of_max(vals, iota, sentinel):
    mx = jnp.max(vals, axis=0, keepdims=True)
    idx = jnp.min(jnp.where(vals == mx, iota, sentinel), axis=0, keepdims=True)
    return mx, idx


def _outproj_kernel(mixc_ref, mixl_ref, x_ref, mod_ref, g_ref, w_ref, wr_ref, rb_ref, tri_ref,
                    x1_ref, h2_ref, ids_ref, wts_ref, rank_ref, cnt_ref, run_ref):
    i = pl.program_id(0)

    @pl.when(i == 0)
    def _():
        run_ref[...] = jnp.zeros_like(run_ref)

    gate1, shift2, scale2 = mod_ref[0, 0, 2:3, :], mod_ref[0, 0, 3:4, :], mod_ref[0, 0, 4:5, :]
    mix = jnp.where(i < N_CTX_TILES, mixc_ref[...], mixl_ref[...])
    x1 = x_ref[...] + gate1 * jnp.dot(mix, w_ref[0], preferred_element_type=F32)
    x1_ref[...] = x1
    h2 = _rmsnorm(x1, g_ref[0]) * (1.0 + scale2) + shift2
    h2_ref[...] = _pack_halves(h2)

    nt = (((1,), (1,)), ((), ()))
    h_hi = h2.astype(BF16)
    h_lo = (h2 - h_hi.astype(F32)).astype(BF16)
    part = lax.dot_general(wr_ref[0], h_hi, nt, preferred_element_type=F32)
    logits = (part[:N_EXPERTS] + part[N_EXPERTS:]
              + lax.dot_general(wr_ref[0, :N_EXPERTS, :], h_lo, nt, preferred_element_type=F32))
    scores = _sigmoid(logits)
    biased = scores + rb_ref[0]
    ninf = -jnp.inf
    mem = lax.broadcasted_iota(I32, (EXPERTS_PER_GROUP, TR), 0)
    grp_rows = []
    for g in range(N_GROUPS):
        bgp = biased[g * EXPERTS_PER_GROUP:(g + 1) * EXPERTS_PER_GROUP]
        m1, i1 = _first_index_of_max(bgp, mem, EXPERTS_PER_GROUP)
        m2 = jnp.max(jnp.where(mem == i1, ninf, bgp), axis=0, keepdims=True)
        grp_rows.append(m1 + m2)
    cur = jnp.concatenate(grp_rows, axis=0)
    gio = lax.broadcasted_iota(I32, (N_GROUPS, TR), 0)
    gsel = jnp.zeros((N_GROUPS, TR), F32)
    for _ in range(TOPK_GROUPS):
        _, gi = _first_index_of_max(cur, gio, N_GROUPS)
        hit = gio == gi
        gsel = jnp.where(hit, 1.0, gsel)
        cur = jnp.where(hit, ninf, cur)
    cur = jnp.concatenate(
        [jnp.where(gsel[g:g + 1] > 0.0, biased[g * EXPERTS_PER_GROUP:(g + 1) * EXPERTS_PER_GROUP], ninf)
         for g in range(N_GROUPS)], axis=0)
    eio = lax.broadcasted_iota(I32, (N_EXPERTS, TR), 0)
    chosen = jnp.zeros((N_EXPERTS, TR), F32)
    ids, sel = [], []
    for _ in range(TOP_K):
        _, ei = _first_index_of_max(cur, eio, N_EXPERTS)
        hit = eio == ei
        ids.append(ei)
        sel.append(jnp.sum(jnp.where(hit, scores, 0.0), axis=0, keepdims=True))
        chosen = jnp.where(hit, 1.0, chosen)
        cur = jnp.where(hit, ninf, cur)
    total = functools.reduce(lambda a, b: a + b, sel)
    ids_ref[...] = jnp.concatenate(ids, axis=0)
    wts_ref[...] = jnp.concatenate([s / total * ROUTE_SCALE for s in sel], axis=0)

    before = jnp.dot(chosen.astype(BF16), tri_ref[...], preferred_element_type=F32) + run_ref[:, 0:1]
    rank_ref[...] = jnp.concatenate(
        [jnp.sum(jnp.where(eio == ei, before, 0.0), axis=0, keepdims=True) for ei in ids], axis=0).astype(I32)
    run = run_ref[...] + jnp.sum(chosen, axis=1, keepdims=True)
    run_ref[...] = run
    cnt_ref[...] = run


def _outproj(layer, mix_ctx, mix_lat, x, mods, g_ffn, w_out_bf, wr_split, router_bias, tri):
    row = lambda i: (i, 0)
    col = lambda i: (0, i)
    const = lambda i: (0, 0)
    lay = lambda i: (layer, 0, 0)
    return pl.pallas_call(
        _outproj_kernel,
        grid=(N_TILES,),
        in_specs=[pl.BlockSpec((TR, D_MODEL), lambda i: (jnp.minimum(i, N_CTX_TILES - 1), 0)),
                  pl.BlockSpec((TR, D_MODEL), lambda i: (jnp.maximum(i - N_CTX_TILES, 0), 0)),
                  pl.BlockSpec((TR, D_MODEL), row),
                  _mod_spec(layer),
                  pl.BlockSpec((1, 1, D_MODEL), lay),
                  pl.BlockSpec((1, D_MODEL, D_MODEL), lay, pipeline_mode=pl.Buffered(1)),
                  pl.BlockSpec((1, 2 * N_EXPERTS, D_MODEL), lay),
                  pl.BlockSpec((1, N_EXPERTS, 1), lay),
                  pl.BlockSpec((TR, TR), const)],
        out_specs=[pl.BlockSpec((TR, D_MODEL), row),
                   pl.BlockSpec((TR, HALF), row),
                   pl.BlockSpec((TOP_K, TR), col),
                   pl.BlockSpec((TOP_K, TR), col),
                   pl.BlockSpec((TOP_K, TR), col),
                   pl.BlockSpec((N_EXPERTS, META_LANES), const)],
        out_shape=[jax.ShapeDtypeStruct((T_ALL, D_MODEL), F32),
                   jax.ShapeDtypeStruct((T_ALL, HALF), U32),
                   jax.ShapeDtypeStruct((TOP_K, T_ALL), I32),
                   jax.ShapeDtypeStruct((TOP_K, T_ALL), F32),
                   jax.ShapeDtypeStruct((TOP_K, T_ALL), I32),
                   jax.ShapeDtypeStruct((N_EXPERTS, META_LANES), F32)],
        scratch_shapes=[pltpu.VMEM((N_EXPERTS, META_LANES), F32)],
        compiler_params=_cparams(("arbitrary",)),
        name="outproj",
    )(mix_ctx, mix_lat, x, mods, g_ffn.reshape(DEPTH, 1, D_MODEL), w_out_bf, wr_split,
      router_bias.reshape(DEPTH, N_EXPERTS, 1), tri)


def _positions_kernel(ids_ref, rank_ref, cnt_ref, pos_ref, te_ref, meta_ref):
    ids = ids_ref[...]
    pos = rank_ref[...]
    tile_io = lax.broadcasted_iota(I32, (1, TE_LANES), 1)
    lane_io = lax.broadcasted_iota(I32, (1, META_LANES), 1)
    te = jnp.zeros((1, TE_LANES), I32)
    starts = jnp.zeros((1, META_LANES), I32)
    counts = jnp.zeros((1, META_LANES), I32)
    start = jnp.zeros((1, 1), I32)
    for e in range(N_EXPERTS):
        c = cnt_ref[e:e + 1, 0:1].astype(I32)
        pos = pos + jnp.where(ids == e, start, 0)
        starts = starts + jnp.where(lane_io == e, start, 0)
        counts = counts + jnp.where(lane_io == e, c, 0)
        start = start + ((c + (TM - 1)) // TM) * TM
        te = te + jnp.where(tile_io * TM >= start, 1, 0)
    pos_ref[...] = pos
    te_ref[...] = jnp.minimum(te, N_EXPERTS - 1)
    meta_ref[0:1, :] = starts
    meta_ref[1:2, :] = counts
    meta_ref[2:3, :] = jnp.broadcast_to(start // TM, (1, META_LANES))
    meta_ref[3:8, :] = jnp.zeros((5, META_LANES), I32)


def _positions(ids, rank, cnt):
    return pl.pallas_call(
        _positions_kernel,
        out_shape=[jax.ShapeDtypeStruct((TOP_K, T_ALL), I32),
                   jax.ShapeDtypeStruct((1, TE_LANES), I32),
                   jax.ShapeDtypeStruct((8, META_LANES), I32)],
        compiler_params=_cparams(),
        name="positions",
    )(ids, rank, cnt)


def _dispatch_kernel(pos_ref, meta_ref, h_ref, xs_ref, zero_ref, sem, pad_sem):
    i = pl.program_id(0)

    def row_copy(t, k):
        return pltpu.make_async_copy(h_ref.at[pl.ds(t, 1)], xs_ref.at[pl.ds(pos_ref[k, t], 1)], sem)

    def issue(t, c):
        for k in range(TOP_K):
            row_copy(t, k).start(priority=k % 2)
        return c

    def drain(t, c):
        for k in range(TOP_K):
            row_copy(t, k).wait()
        return c

    lax.fori_loop(0, TR, issue, 0)
    lax.fori_loop(0, TR, drain, 0)

    @pl.when(i == N_TILES - 1)
    def _():
        zero_ref[...] = jnp.zeros_like(zero_ref)

        def per_expert(e, c):
            count = meta_ref[1, e]
            first = meta_ref[0, e] + count
            npad = ((count + (TM - 1)) // TM) * TM - count

            def pad_copy(r):
                return pltpu.make_async_copy(zero_ref.at[pl.ds(0, 1)], xs_ref.at[pl.ds(first + r, 1)], pad_sem)

            def pad_issue(r, c2):
                pad_copy(r).start()
                return c2

            def pad_drain(r, c2):
                pad_copy(r).wait()
                return c2

            lax.fori_loop(0, npad, pad_issue, 0)
            lax.fori_loop(0, npad, pad_drain, 0)
            return c

        lax.fori_loop(0, N_EXPERTS, per_expert, 0)


def _dispatch(pos, meta, h2):
    return pl.pallas_call(
        _dispatch_kernel,
        grid=(N_TILES,),
        in_specs=[pl.BlockSpec((TOP_K, TR), lambda i: (0, i), memory_space=pltpu.SMEM),
                  pl.BlockSpec(memory_space=pltpu.SMEM),
                  pl.BlockSpec((TR, HALF), lambda i: (i, 0))],
        out_specs=pl.BlockSpec(memory_space=pl.ANY),
        out_shape=jax.ShapeDtypeStruct((P_SORT, HALF), U32),
        scratch_shapes=[pltpu.VMEM((8, HALF), U32),
                        pltpu.SemaphoreType.DMA(()),
                        pltpu.SemaphoreType.DMA(())],
        compiler_params=_cparams(("arbitrary",)),
        name="dispatch",
    )(pos, meta, h2)


def _experts_kernel(te_ref, nt_ref, x_ref, wg_ref, wu_ref, wd_ref, y_ref, wg_bf, wu_bf, wd_bf):
    j = pl.program_id(0)
    used = j < nt_ref[0]
    prev = te_ref[jnp.maximum(j - 1, 0)]

    @pl.when(used & ((j == 0) | (te_ref[j] != prev)))
    def _():
        wg_bf[...] = wg_ref[0, 0].astype(BF16)
        wu_bf[...] = wu_ref[0, 0].astype(BF16)
        wd_bf[...] = wd_ref[0, 0].astype(BF16)

    @pl.when(used)
    def _():
        xa, xb = _unpack_halves(x_ref[...])
        xa, xb = xa.astype(BF16), xb.astype(BF16)
        g = (jnp.dot(xa, wg_bf[:HALF], preferred_element_type=F32)
             + jnp.dot(xb, wg_bf[HALF:], preferred_element_type=F32))
        u = (jnp.dot(xa, wu_bf[:HALF], preferred_element_type=F32)
             + jnp.dot(xb, wu_bf[HALF:], preferred_element_type=F32))
        a = (_silu(g) * u).astype(BF16)
        y_ref[...] = _pack_halves(jnp.dot(a, wd_bf[...], preferred_element_type=F32))


def _experts(layer, te, nt, xs, w_gate, w_up, w_down):
    n_tiles = xs.shape[0] // TM
    tile = lambda j, te, nt: (jnp.minimum(j, nt[0] - 1), 0)
    wsel = lambda j, te, nt: (layer, te[jnp.minimum(j, nt[0] - 1)], 0, 0)
    return pl.pallas_call(
        _experts_kernel,
        grid_spec=pltpu.PrefetchScalarGridSpec(
            num_scalar_prefetch=2,
            grid=(n_tiles,),
            in_specs=[pl.BlockSpec((TM, HALF), tile),
                      pl.BlockSpec((1, 1, D_MODEL, EXPERT_FF), wsel),
                      pl.BlockSpec((1, 1, D_MODEL, EXPERT_FF), wsel),
                      pl.BlockSpec((1, 1, EXPERT_FF, D_MODEL), wsel)],
            out_specs=pl.BlockSpec((TM, HALF), tile),
            scratch_shapes=[pltpu.VMEM((D_MODEL, EXPERT_FF), BF16),
                            pltpu.VMEM((D_MODEL, EXPERT_FF), BF16),
                            pltpu.VMEM((EXPERT_FF, D_MODEL), BF16)]),
        out_shape=jax.ShapeDtypeStruct(xs.shape, U32),
        compiler_params=_cparams(("arbitrary",)),
        name="experts",
    )(te, nt, xs, w_gate, w_up, w_down)


def _combine_kernel(final, pos_ref, wts_ref, x1_ref, ysh_ref, mod_ref, g_ref, y_ref, *rest):
    *out_refs, buf_ref, sem = rest
    i = pl.program_id(0)

    def row_copy(t, k):
        return pltpu.make_async_copy(y_ref.at[pl.ds(pos_ref[k, t], 1)], buf_ref.at[k, pl.ds(t, 1)], sem)

    def issue(t, c):
        for k in range(TOP_K):
            row_copy(t, k).start(priority=k % 2)
        return c

    def drain(t, c):
        for k in range(TOP_K):
            row_copy(t, k).wait()
        return c

    lax.fori_loop(0, TC_ROWS, issue, 0)
    lax.fori_loop(0, TC_ROWS, drain, 0)
    w = jnp.transpose(wts_ref[...])
    acc_a, acc_b = _unpack_halves(ysh_ref[...])
    for k in range(TOP_K):
        ya, yb = _unpack_halves(buf_ref[k])
        acc_a = acc_a + w[:, k:k + 1] * ya
        acc_b = acc_b + w[:, k:k + 1] * yb
    x2 = x1_ref[...] + mod_ref[0, 0, 5:6, :] * jnp.concatenate([acc_a, acc_b], axis=1)
    if not final:
        out_refs[0][...] = x2
        return
    out = _rmsnorm(x2, g_ref[...])
    oc_ref, ol_ref = out_refs

    @pl.when(i < T_CTX // TC_ROWS)
    def _():
        oc_ref[...] = out

    @pl.when(i >= T_CTX // TC_ROWS)
    def _():
        ol_ref[...] = out


def _combine(layer, pos, wts, x1, ysh, mods, final_norm, y):
    final = layer == DEPTH - 1
    row = lambda i: (i, 0)
    col = lambda i: (0, i)
    n_ctx = T_CTX // TC_ROWS
    if final:
        out_specs = [pl.BlockSpec((TC_ROWS, D_MODEL), lambda i: (jnp.minimum(i, n_ctx - 1), 0)),
                     pl.BlockSpec((TC_ROWS, D_MODEL), lambda i: (jnp.maximum(i - n_ctx, 0), 0))]
        out_shape = [jax.ShapeDtypeStruct((T_CTX, D_MODEL), F32), jax.ShapeDtypeStruct((T_LAT, D_MODEL), F32)]
    else:
        out_specs = [pl.BlockSpec((TC_ROWS, D_MODEL), row)]
        out_shape = [jax.ShapeDtypeStruct((T_ALL, D_MODEL), F32)]
    return pl.pallas_call(
        functools.partial(_combine_kernel, final),
        grid=(T_ALL // TC_ROWS,),
        in_specs=[pl.BlockSpec((TOP_K, TC_ROWS), col, memory_space=pltpu.SMEM),
                  pl.BlockSpec((TOP_K, TC_ROWS), col),
                  pl.BlockSpec((TC_ROWS, D_MODEL), row),
                  pl.BlockSpec((TC_ROWS, HALF), row),
                  _mod_spec(layer, TR // TC_ROWS),
                  pl.BlockSpec((1, D_MODEL), lambda i: (0, 0)),
                  pl.BlockSpec(memory_space=pl.ANY)],
        out_specs=out_specs,
        out_shape=out_shape,
        scratch_shapes=[pltpu.VMEM((TOP_K, TC_ROWS, HALF), U32),
                        pltpu.SemaphoreType.DMA(())],
        compiler_params=_cparams(("arbitrary",)),
        name="combine",
    )(pos, wts, x1, ysh, mods, final_norm.reshape(1, D_MODEL), y)


def kernel(x_prompt, x_sample, cache_k, cache_v, c, c_ctx, w_mod, b_mod, g_mix, g_ffn, w_in, w_out, conv_w,
           attn_sink, w_router, router_bias, w_gate_e, w_up_e, w_down_e, w_gate_s, w_up_s, w_down_s, final_norm):
    ccsc = jnp.asarray(_channel_dft(), BF16)
    csl_ctx = jnp.asarray(_dft_tables(SEQ), BF16)
    csl_lat = jnp.asarray(_dft_tables(DEC_SEQ), BF16)
    rope = tuple(jnp.asarray(t, F32) for t in _rope_tables(DEC_SEQ))
    tri = jnp.asarray(np.triu(np.ones((TR, TR)), 1), BF16)
    shared_te = jnp.zeros((T_ALL // TM,), I32)
    shared_nt = jnp.full((1,), T_ALL // TM, I32)

    cond = jnp.concatenate([c_ctx[None], c, jnp.zeros((N_COND - 1 - DEC_BATCH, D_MODEL), F32)], axis=0)
    mods = _mods(cond, w_mod, b_mod).reshape(DEPTH, N_COND, N_MOD, D_MODEL)
    cache_k = cache_k.reshape(DEC_BATCH, DEPTH, SEQ, KV_W)
    cache_v = cache_v.reshape(DEC_BATCH, DEPTH, SEQ, KV_W)
    w_in_bf, w_out_bf = w_in.astype(BF16), w_out.astype(BF16)
    wr_t = jnp.swapaxes(w_router, 1, 2)
    wr_hi = wr_t.astype(BF16)
    wr_split = jnp.concatenate([wr_hi, (wr_t - wr_hi.astype(F32)).astype(BF16)], axis=1)
    ws_gate, ws_up, ws_down = w_gate_s[:, None], w_up_s[:, None], w_down_s[:, None]
    x = jnp.concatenate([x_prompt.reshape(T_CTX, D_MODEL), x_sample.reshape(T_LAT, D_MODEL)], axis=0)

    new_k, new_v = [], []
    for l in range(DEPTH):
        q, kv, fm = _inproj(l, x, mods, g_mix, w_in_bf, rope)
        new_k.append(kv[:T_CTX, :KV_W].reshape(BATCH, SEQ, N_KV_HEADS, HEAD_DIM))
        new_v.append(kv[:T_CTX, KV_W:].reshape(BATCH, SEQ, N_KV_HEADS, HEAD_DIM))
        mix_ctx = _ctx_mix(l, attn_sink, q, kv, fm, ccsc, csl_ctx, conv_w)
        mix_lat = _lat_mix(l, attn_sink, q, kv, fm, cache_k, cache_v, ccsc, csl_lat, conv_w)
        x1, h2, ids, wts, rank, cnt = _outproj(l, mix_ctx, mix_lat, x, mods, g_ffn, w_out_bf, wr_split,
                                               router_bias, tri)
        pos, te, meta = _positions(ids, rank, cnt)
        xs = _dispatch(pos, meta, h2)
        y = _experts(l, te[0, :N_SORT_TILES], meta[2, :1], xs, w_gate_e, w_up_e, w_down_e)
        ysh = _experts(l, shared_te, shared_nt, h2, ws_gate, ws_up, ws_down)
        outs = _combine(l, pos, wts, x1, ysh, mods, final_norm, y)
        x = outs[0]

    y_prompt = outs[0].reshape(BATCH, SEQ, D_MODEL)
    y_sample = outs[1].reshape(DEC_BATCH, DEC_SEQ, D_MODEL)
    return y_prompt, y_sample, jnp.stack(new_k, axis=1), jnp.stack(new_v, axis=1)
```

```python
import functools

import numpy as np
import jax
import jax.numpy as jnp
from jax import lax
from jax.experimental import pallas as pl
from jax.experimental.pallas import tpu as pltpu

F32 = jnp.float32
BF16 = jnp.bfloat16
I32 = jnp.int32
U32 = jnp.uint32

D_MODEL = 2048
HALF = D_MODEL // 2
BATCH, SEQ = 32, 256
DEC_BATCH, DEC_SEQ = 4, 1024
DEPTH = 2
GRID_W = 64
HEAD_DIM = 128
N_HEADS, N_KV_HEADS, Q_PER_KV = 8, 2, 4
ATTN_W, KV_W = 1024, 256
WINDOW, BLOCK = 128, 128
SCALE = HEAD_DIM ** -0.5
ROPE_BASE = 10000.0
ROPE_PAIRS = 32
FOURIER_W, FOURIER_GROUPS, FOURIER_GROUP_W = 512, 4, 128
CONV_W = 512
IN_W = 3584
N_EXPERTS, TOP_K, N_GROUPS, EXPERTS_PER_GROUP, TOPK_GROUPS = 64, 8, 8, 8, 4
EXPERT_FF = 512
ROUTE_SCALE = 2.5
N_MOD = 6
EPS = 1e-6
NEG = -1e30

T_CTX = BATCH * SEQ
T_LAT = DEC_BATCH * DEC_SEQ
T_ALL = T_CTX + T_LAT
TR = 512
N_CTX_TILES = T_CTX // TR
LAT_TILES_PER_SEQ = DEC_SEQ // TR
N_TILES = T_ALL // TR
N_COND = 8
MOD_TN = 1024
TM = 256
TC_ROWS = 128
N_SORT_TILES = T_ALL * TOP_K // TM + N_EXPERTS
P_SORT = N_SORT_TILES * TM
META_LANES = 128
TE_LANES = 512
VMEM_LIMIT = 56 * 1024 * 1024


def _cparams(sem=None):
    return pltpu.CompilerParams(dimension_semantics=sem, vmem_limit_bytes=VMEM_LIMIT)


def _silu(x):
    return x / (1.0 + jnp.exp(-x))


def _sigmoid(x):
    return 1.0 / (1.0 + jnp.exp(-x))


def _rmsnorm(x, g):
    return x * lax.rsqrt(jnp.mean(x * x, axis=-1, keepdims=True) + EPS) * g


def _cond_row(i):
    return jnp.where(i < N_CTX_TILES, 0, 1 + (i - N_CTX_TILES) // LAT_TILES_PER_SEQ)


def _pack_halves(v):
    n = v.shape[1] // 2
    hi = pltpu.bitcast(v[:, :n].astype(BF16).astype(F32), U32)
    lo = pltpu.bitcast(v[:, n:].astype(BF16).astype(F32), U32)
    return hi | (lo >> 16)


def _unpack_halves(w):
    return pltpu.bitcast(w & jnp.uint32(0xFFFF0000), F32), pltpu.bitcast(w << 16, F32)


SLAB, LANES = 8, 128


def _load_slabs(ref, n, lead=()):
    return jnp.concatenate([ref[lead + (pl.ds(s, n, stride=SLAB), slice(None))] for s in range(SLAB)], axis=1)


def _store_slabs(ref, words):
    n = words.shape[0]
    for s in range(SLAB):
        ref[pl.ds(s, n, stride=SLAB), :] = words[:, s * LANES:(s + 1) * LANES]


def _dft_tables(L):
    j = np.arange(L, dtype=np.int64)
    ang = 2.0 * np.pi * ((j[:, None] * j[None, :]) % L).astype(np.float64) / L
    c, s = np.cos(ang) / np.sqrt(L), np.sin(ang) / np.sqrt(L)
    return np.concatenate([c, -s], axis=1)


def _channel_dft():
    n = FOURIER_GROUP_W
    j = np.arange(n, dtype=np.int64)
    ang = 2.0 * np.pi * ((j[:, None] * j[None, :]) % n).astype(np.float64) / n
    return np.concatenate([np.cos(ang), np.sin(ang)], axis=1) / np.sqrt(n)


def _rope_tables(S):
    pos = np.arange(S)
    row, col = (pos // GRID_W).astype(np.float64), (pos % GRID_W).astype(np.float64)
    inv = ROPE_BASE ** (-np.arange(ROPE_PAIRS, dtype=np.float64) / ROPE_PAIRS)
    ar, ac = row[:, None] * inv, col[:, None] * inv
    z = np.zeros_like(ar)
    cos = np.concatenate([np.cos(ar), np.cos(ar), np.cos(ac), np.cos(ac)], axis=1)
    sin_hi = np.concatenate([-np.sin(ar), z, -np.sin(ac), z], axis=1)
    sin_lo = np.concatenate([z, np.sin(ar), z, np.sin(ac)], axis=1)
    return cos, sin_hi, sin_lo


def _mods_kernel(cond_ref, w_ref, b_ref, o_ref):
    s = _silu(cond_ref[...]).astype(BF16)
    o_ref[0] = jnp.dot(s, w_ref[0].astype(BF16), preferred_element_type=F32) + b_ref[0]


def _mods(cond, w_mod, b_mod):
    n = N_MOD * D_MODEL
    return pl.pallas_call(
        _mods_kernel,
        grid=(DEPTH, n // MOD_TN),
        in_specs=[pl.BlockSpec((N_COND, D_MODEL), lambda l, j: (0, 0)),
                  pl.BlockSpec((1, D_MODEL, MOD_TN), lambda l, j: (l, 0, j)),
                  pl.BlockSpec((1, 1, MOD_TN), lambda l, j: (l, 0, j))],
        out_specs=pl.BlockSpec((1, N_COND, MOD_TN), lambda l, j: (l, 0, j)),
        out_shape=jax.ShapeDtypeStruct((DEPTH, N_COND, n), F32),
        compiler_params=_cparams(("arbitrary", "arbitrary")),
        name="mods",
    )(cond, w_mod, b_mod.reshape(DEPTH, 1, n))


def _mod_spec(layer, tiles_per_tr=1):
    return pl.BlockSpec((1, 1, N_MOD, D_MODEL), lambda i: (layer, _cond_row(i // tiles_per_tr), 0, 0))


def _inproj_kernel(x_ref, mod_ref, g_ref, w_ref, cos_ref, shi_ref, slo_ref, q_ref, kv_ref, fm_ref):
    i = pl.program_id(0)
    shift, scale = mod_ref[0, 0, 0:1, :], mod_ref[0, 0, 1:2, :]
    h = (_rmsnorm(x_ref[...], g_ref[0]) * (1.0 + scale) + shift).astype(BF16)
    fm_ref[...] = jnp.dot(h, w_ref[0, :, ATTN_W + 2 * KV_W:], preferred_element_type=F32).astype(BF16)
    qkv = jnp.dot(h, w_ref[0, :, :ATTN_W + 2 * KV_W], preferred_element_type=F32)
    kv_ref[:, KV_W:] = qkv[:, ATTN_W + KV_W:]

    @pl.when(i < N_CTX_TILES)
    def _():
        q_ref[...] = qkv[:, :ATTN_W].astype(BF16)
        kv_ref[:, :KV_W] = qkv[:, ATTN_W:ATTN_W + KV_W]

    @pl.when(i >= N_CTX_TILES)
    def _():
        cos, shi, slo = cos_ref[...], shi_ref[...], slo_ref[...]
        for hd in range(N_HEADS + N_KV_HEADS):
            xh = qkv[:, hd * HEAD_DIM:(hd + 1) * HEAD_DIM]
            r = xh * cos + pltpu.roll(xh, HEAD_DIM - 32, 1) * shi + pltpu.roll(xh, 32, 1) * slo
            if hd < N_HEADS:
                q_ref[:, hd * HEAD_DIM:(hd + 1) * HEAD_DIM] = r.astype(BF16)
            else:
                kv_ref[:, (hd - N_HEADS) * HEAD_DIM:(hd - N_HEADS + 1) * HEAD_DIM] = r


def _inproj(layer, x, mods, g_mix, w_in_bf, rope):
    lat_blk = lambda i: (jnp.maximum(i - N_CTX_TILES, 0) % LAT_TILES_PER_SEQ, 0)
    row = lambda i: (i, 0)
    return pl.pallas_call(
        _inproj_kernel,
        grid=(N_TILES,),
        in_specs=[pl.BlockSpec((TR, D_MODEL), row),
                  _mod_spec(layer),
                  pl.BlockSpec((1, 1, D_MODEL), lambda i: (layer, 0, 0)),
                  pl.BlockSpec((1, D_MODEL, IN_W), lambda i: (layer, 0, 0), pipeline_mode=pl.Buffered(1)),
                  pl.BlockSpec((TR, HEAD_DIM), lat_blk),
                  pl.BlockSpec((TR, HEAD_DIM), lat_blk),
                  pl.BlockSpec((TR, HEAD_DIM), lat_blk)],
        out_specs=[pl.BlockSpec((TR, ATTN_W), row),
                   pl.BlockSpec((TR, 2 * KV_W), row),
                   pl.BlockSpec((TR, FOURIER_W + 3 * CONV_W), row)],
        out_shape=[jax.ShapeDtypeStruct((T_ALL, ATTN_W), BF16),
                   jax.ShapeDtypeStruct((T_ALL, 2 * KV_W), F32),
                   jax.ShapeDtypeStruct((T_ALL, FOURIER_W + 3 * CONV_W), BF16)],
        compiler_params=_cparams(("arbitrary",)),
        name="inproj",
    )(x, mods, g_mix.reshape(DEPTH, 1, D_MODEL), w_in_bf, *rope)


def _fourier_conv(fm_ref, ccsc_ref, csl_ref, cw_ref, o_ref):
    L = fm_ref.shape[0]
    for g in range(FOURIER_GROUPS):
        xg = fm_ref[:, g * FOURIER_GROUP_W:(g + 1) * FOURIER_GROUP_W]
        z = jnp.dot(xg, ccsc_ref[...], preferred_element_type=F32)
        zz = jnp.concatenate([z[:, :FOURIER_GROUP_W], z[:, FOURIER_GROUP_W:]], axis=0).astype(BF16)
        f = jnp.dot(csl_ref[...], zz, preferred_element_type=F32)
        o_ref[:, ATTN_W + g * FOURIER_GROUP_W:ATTN_W + (g + 1) * FOURIER_GROUP_W] = f.astype(BF16)
    xc = fm_ref[:, FOURIER_W:FOURIER_W + CONV_W].astype(F32)
    bg = fm_ref[:, FOURIER_W + CONV_W:FOURIER_W + 2 * CONV_W].astype(F32)
    cg = fm_ref[:, FOURIER_W + 2 * CONV_W:].astype(F32)
    u = cg * xc
    rows = lax.broadcasted_iota(I32, u.shape, 0)
    prev = jnp.where(rows == 0, 0.0, pltpu.roll(u, 1, 0))
    nxt = jnp.where(rows == L - 1, 0.0, pltpu.roll(u, L - 1, 0))
    y = prev * cw_ref[0, 0:1, :] + u * cw_ref[0, 1:2, :] + nxt * cw_ref[0, 2:3, :]
    o_ref[:, ATTN_W + FOURIER_W:] = (bg * y).astype(BF16)


def _ctx_mix_kernel(layer, sink_ref, q_ref, kv_ref, fm_ref, ccsc_ref, csl_ref, cw_ref, o_ref):
    for h in range(N_HEADS):
        g = h // Q_PER_KV
        q = q_ref[:, h * HEAD_DIM:(h + 1) * HEAD_DIM]
        k = kv_ref[:, g * HEAD_DIM:(g + 1) * HEAD_DIM].astype(BF16)
        v = kv_ref[:, KV_W + g * HEAD_DIM:KV_W + (g + 1) * HEAD_DIM].astype(BF16)
        s = lax.dot_general(q, k, (((1,), (1,)), ((), ())), preferred_element_type=F32) * SCALE
        sink = sink_ref[layer, h]
        m = jnp.maximum(jnp.max(s, axis=-1, keepdims=True), sink)
        e = jnp.exp(s - m)
        den = jnp.sum(e, axis=-1, keepdims=True) + jnp.exp(sink - m)
        o = jnp.dot(e.astype(BF16), v, preferred_element_type=F32) / den
        o_ref[:, h * HEAD_DIM:(h + 1) * HEAD_DIM] = o.astype(BF16)
    _fourier_conv(fm_ref, ccsc_ref, csl_ref, cw_ref, o_ref)


def _ctx_mix(layer, sink, q, kv, fm, ccsc, csl, conv_w):
    row = lambda b: (b, 0)
    const = lambda b: (0, 0)
    return pl.pallas_call(
        functools.partial(_ctx_mix_kernel, layer),
        grid=(BATCH,),
        in_specs=[pl.BlockSpec(memory_space=pltpu.SMEM),
                  pl.BlockSpec((SEQ, ATTN_W), row),
                  pl.BlockSpec((SEQ, 2 * KV_W), row),
                  pl.BlockSpec((SEQ, FOURIER_W + 3 * CONV_W), row),
                  pl.BlockSpec((FOURIER_GROUP_W, 2 * FOURIER_GROUP_W), const),
                  pl.BlockSpec((SEQ, 2 * SEQ), const),
                  pl.BlockSpec((1, 3, CONV_W), lambda b: (layer, 0, 0))],
        out_specs=pl.BlockSpec((SEQ, D_MODEL), row),
        out_shape=jax.ShapeDtypeStruct((T_CTX, D_MODEL), BF16),
        compiler_params=_cparams(("arbitrary",)),
        name="ctx_mix",
    )(sink, q, kv, fm, ccsc, csl, conv_w)


def _lat_mix_kernel(layer, sink_ref, q_ref, kv_ref, fm_ref, ck_ref, cv_ref, ccsc_ref, csl_ref, cw_ref, o_ref):
    n = pl.program_id(1)
    nb = DEC_SEQ // BLOCK
    rows = Q_PER_KV * BLOCK
    band = 3 * BLOCK

    @pl.when(n == 0)
    def _():
        _fourier_conv(fm_ref, ccsc_ref, csl_ref, cw_ref, o_ref)

    q0 = pl.multiple_of(n * BLOCK, BLOCK)
    k0 = pl.multiple_of(jnp.clip(n - 1, 0, nb - 3) * BLOCK, BLOCK)
    qpos = q0 + lax.broadcasted_iota(I32, (rows, band), 0) % BLOCK
    kpos = k0 + lax.broadcasted_iota(I32, (rows, band), 1)
    valid = jnp.abs(qpos - kpos) <= WINDOW
    for g in range(N_KV_HEADS):
        kb = kv_ref[pl.ds(k0, band), g * HEAD_DIM:(g + 1) * HEAD_DIM].astype(BF16)
        vb = kv_ref[pl.ds(k0, band), KV_W + g * HEAD_DIM:KV_W + (g + 1) * HEAD_DIM].astype(BF16)
        k_ctx = ck_ref[0, 0, :, g * HEAD_DIM:(g + 1) * HEAD_DIM].astype(BF16)
        v_ctx = cv_ref[0, 0, :, g * HEAD_DIM:(g + 1) * HEAD_DIM].astype(BF16)
        sink = jnp.concatenate(
            [jnp.full((BLOCK, 1), sink_ref[layer, g * Q_PER_KV + j], F32) for j in range(Q_PER_KV)], axis=0)
        q = jnp.concatenate(
            [q_ref[pl.ds(q0, BLOCK), (g * Q_PER_KV + j) * HEAD_DIM:(g * Q_PER_KV + j + 1) * HEAD_DIM]
             for j in range(Q_PER_KV)], axis=0)
        s_loc = lax.dot_general(q, kb, (((1,), (1,)), ((), ())), preferred_element_type=F32) * SCALE
        s_loc = jnp.where(valid, s_loc, NEG)
        s_ctx = lax.dot_general(q, k_ctx, (((1,), (1,)), ((), ())), preferred_element_type=F32) * SCALE
        m = jnp.maximum(jnp.maximum(jnp.max(s_loc, axis=-1, keepdims=True),
                                    jnp.max(s_ctx, axis=-1, keepdims=True)), sink)
        e_loc, e_ctx = jnp.exp(s_loc - m), jnp.exp(s_ctx - m)
        den = (jnp.sum(e_loc, axis=-1, keepdims=True) + jnp.sum(e_ctx, axis=-1, keepdims=True)
               + jnp.exp(sink - m))
        o = (jnp.dot(e_loc.astype(BF16), vb, preferred_element_type=F32)
             + jnp.dot(e_ctx.astype(BF16), v_ctx, preferred_element_type=F32)) / den
        for j in range(Q_PER_KV):
            h = g * Q_PER_KV + j
            o_ref[pl.ds(q0, BLOCK), h * HEAD_DIM:(h + 1) * HEAD_DIM] = o[j * BLOCK:(j + 1) * BLOCK].astype(BF16)


def _lat_mix(layer, sink, q, kv, fm, cache_k, cache_v, ccsc, csl, conv_w):
    off = T_CTX // DEC_SEQ
    row = lambda b, n: (off + b, 0)
    const = lambda b, n: (0, 0)
    cache = lambda b, n: (b, layer, 0, 0)
    return pl.pallas_call(
        functools.partial(_lat_mix_kernel, layer),
        grid=(DEC_BATCH, DEC_SEQ // BLOCK),
        in_specs=[pl.BlockSpec(memory_space=pltpu.SMEM),
                  pl.BlockSpec((DEC_SEQ, ATTN_W), row),
                  pl.BlockSpec((DEC_SEQ, 2 * KV_W), row),
                  pl.BlockSpec((DEC_SEQ, FOURIER_W + 3 * CONV_W), row),
                  pl.BlockSpec((1, 1, SEQ, KV_W), cache),
                  pl.BlockSpec((1, 1, SEQ, KV_W), cache),
                  pl.BlockSpec((FOURIER_GROUP_W, 2 * FOURIER_GROUP_W), const),
                  pl.BlockSpec((DEC_SEQ, 2 * DEC_SEQ), const, pipeline_mode=pl.Buffered(1)),
                  pl.BlockSpec((1, 3, CONV_W), lambda b, n: (layer, 0, 0))],
        out_specs=pl.BlockSpec((DEC_SEQ, D_MODEL), lambda b, n: (b, 0)),
        out_shape=jax.ShapeDtypeStruct((T_LAT, D_MODEL), BF16),
        compiler_params=_cparams(("arbitrary", "arbitrary")),
        name="lat_mix",
    )(sink, q, kv, fm, cache_k, cache_v, ccsc, csl, conv_w)


def _first_index_of_max(vals, iota, sentinel):
    mx = jnp.max(vals, axis=0, keepdims=True)
    idx = jnp.min(jnp.where(vals == mx, iota, sentinel), axis=0, keepdims=True)
    return mx, idx


def _outproj_kernel(mixc_ref, mixl_ref, x_ref, mod_ref, g_ref, w_ref, wr_ref, rb_ref, tri_ref,
                    x1_ref, h2_ref, ids_ref, wts_ref, rank_ref, cnt_ref, run_ref):
    i = pl.program_id(0)

    @pl.when(i == 0)
    def _():
        run_ref[...] = jnp.zeros_like(run_ref)

    gate1, shift2, scale2 = mod_ref[0, 0, 2:3, :], mod_ref[0, 0, 3:4, :], mod_ref[0, 0, 4:5, :]
    mix = jnp.where(i < N_CTX_TILES, mixc_ref[...], mixl_ref[...])
    x1 = x_ref[...] + gate1 * jnp.dot(mix, w_ref[0], preferred_element_type=F32)
    x1_ref[...] = x1
    h2 = _rmsnorm(x1, g_ref[0]) * (1.0 + scale2) + shift2
    _store_slabs(h2_ref, _pack_halves(h2))

    nt = (((1,), (1,)), ((), ()))
    h_hi = h2.astype(BF16)
    h_lo = (h2 - h_hi.astype(F32)).astype(BF16)
    part = lax.dot_general(wr_ref[0], h_hi, nt, preferred_element_type=F32)
    logits = (part[:N_EXPERTS] + part[N_EXPERTS:]
              + lax.dot_general(wr_ref[0, :N_EXPERTS, :], h_lo, nt, preferred_element_type=F32))
    scores = _sigmoid(logits)
    biased = scores + rb_ref[0]
    ninf = -jnp.inf
    mem = lax.broadcasted_iota(I32, (EXPERTS_PER_GROUP, TR), 0)
    grp_rows = []
    for g in range(N_GROUPS):
        bgp = biased[g * EXPERTS_PER_GROUP:(g + 1) * EXPERTS_PER_GROUP]
        m1, i1 = _first_index_of_max(bgp, mem, EXPERTS_PER_GROUP)
        m2 = jnp.max(jnp.where(mem == i1, ninf, bgp), axis=0, keepdims=True)
        grp_rows.append(m1 + m2)
    cur = jnp.concatenate(grp_rows, axis=0)
    gio = lax.broadcasted_iota(I32, (N_GROUPS, TR), 0)
    gsel = jnp.zeros((N_GROUPS, TR), F32)
    for _ in range(TOPK_GROUPS):
        _, gi = _first_index_of_max(cur, gio, N_GROUPS)
        hit = gio == gi
        gsel = jnp.where(hit, 1.0, gsel)
        cur = jnp.where(hit, ninf, cur)
    cur = jnp.concatenate(
        [jnp.where(gsel[g:g + 1] > 0.0, biased[g * EXPERTS_PER_GROUP:(g + 1) * EXPERTS_PER_GROUP], ninf)
         for g in range(N_GROUPS)], axis=0)
    eio = lax.broadcasted_iota(I32, (N_EXPERTS, TR), 0)
    chosen = jnp.zeros((N_EXPERTS, TR), F32)
    ids, sel = [], []
    for _ in range(TOP_K):
        _, ei = _first_index_of_max(cur, eio, N_EXPERTS)
        hit = eio == ei
        ids.append(ei)
        sel.append(jnp.sum(jnp.where(hit, scores, 0.0), axis=0, keepdims=True))
        chosen = jnp.where(hit, 1.0, chosen)
        cur = jnp.where(hit, ninf, cur)
    total = functools.reduce(lambda a, b: a + b, sel)
    ids_ref[...] = jnp.concatenate(ids, axis=0)
    wts_ref[...] = jnp.concatenate([s / total * ROUTE_SCALE for s in sel], axis=0)

    before = jnp.dot(chosen.astype(BF16), tri_ref[...], preferred_element_type=F32) + run_ref[:, 0:1]
    rank_ref[...] = jnp.concatenate(
        [jnp.sum(jnp.where(eio == ei, before, 0.0), axis=0, keepdims=True) for ei in ids], axis=0).astype(I32)
    run = run_ref[...] + jnp.sum(chosen, axis=1, keepdims=True)
    run_ref[...] = run
    cnt_ref[...] = run


def _outproj(layer, mix_ctx, mix_lat, x, mods, g_ffn, w_out_bf, wr_split, router_bias, tri):
    row = lambda i: (i, 0)
    col = lambda i: (0, i)
    const = lambda i: (0, 0)
    lay = lambda i: (layer, 0, 0)
    return pl.pallas_call(
        _outproj_kernel,
        grid=(N_TILES,),
        in_specs=[pl.BlockSpec((TR, D_MODEL), lambda i: (jnp.minimum(i, N_CTX_TILES - 1), 0)),
                  pl.BlockSpec((TR, D_MODEL), lambda i: (jnp.maximum(i - N_CTX_TILES, 0), 0)),
                  pl.BlockSpec((TR, D_MODEL), row),
                  _mod_spec(layer),
                  pl.BlockSpec((1, 1, D_MODEL), lay),
                  pl.BlockSpec((1, D_MODEL, D_MODEL), lay, pipeline_mode=pl.Buffered(1)),
                  pl.BlockSpec((1, 2 * N_EXPERTS, D_MODEL), lay),
                  pl.BlockSpec((1, N_EXPERTS, 1), lay),
                  pl.BlockSpec((TR, TR), const)],
        out_specs=[pl.BlockSpec((TR, D_MODEL), row),
                   pl.BlockSpec((TR * SLAB, LANES), row),
                   pl.BlockSpec((TOP_K, TR), col),
                   pl.BlockSpec((TOP_K, TR), col),
                   pl.BlockSpec((TOP_K, TR), col),
                   pl.BlockSpec((N_EXPERTS, META_LANES), const)],
        out_shape=[jax.ShapeDtypeStruct((T_ALL, D_MODEL), F32),
                   jax.ShapeDtypeStruct((T_ALL * SLAB, LANES), U32),
                   jax.ShapeDtypeStruct((TOP_K, T_ALL), I32),
                   jax.ShapeDtypeStruct((TOP_K, T_ALL), F32),
                   jax.ShapeDtypeStruct((TOP_K, T_ALL), I32),
                   jax.ShapeDtypeStruct((N_EXPERTS, META_LANES), F32)],
        scratch_shapes=[pltpu.VMEM((N_EXPERTS, META_LANES), F32)],
        compiler_params=_cparams(("arbitrary",)),
        name="outproj",
    )(mix_ctx, mix_lat, x, mods, g_ffn.reshape(DEPTH, 1, D_MODEL), w_out_bf, wr_split,
      router_bias.reshape(DEPTH, N_EXPERTS, 1), tri)


def _positions_kernel(ids_ref, rank_ref, cnt_ref, pos_ref, te_ref, meta_ref):
    ids = ids_ref[...]
    pos = rank_ref[...]
    tile_io = lax.broadcasted_iota(I32, (1, TE_LANES), 1)
    lane_io = lax.broadcasted_iota(I32, (1, META_LANES), 1)
    te = jnp.zeros((1, TE_LANES), I32)
    starts = jnp.zeros((1, META_LANES), I32)
    counts = jnp.zeros((1, META_LANES), I32)
    start = jnp.zeros((1, 1), I32)
    for e in range(N_EXPERTS):
        c = cnt_ref[e:e + 1, 0:1].astype(I32)
        pos = pos + jnp.where(ids == e, start, 0)
        starts = starts + jnp.where(lane_io == e, start, 0)
        counts = counts + jnp.where(lane_io == e, c, 0)
        start = start + ((c + (TM - 1)) // TM) * TM
        te = te + jnp.where(tile_io * TM >= start, 1, 0)
    pos_ref[...] = pos
    te_ref[...] = jnp.minimum(te, N_EXPERTS - 1)
    meta_ref[0:1, :] = starts
    meta_ref[1:2, :] = counts
    meta_ref[2:3, :] = jnp.broadcast_to(start // TM, (1, META_LANES))
    meta_ref[3:8, :] = jnp.zeros((5, META_LANES), I32)


def _positions(ids, rank, cnt):
    return pl.pallas_call(
        _positions_kernel,
        out_shape=[jax.ShapeDtypeStruct((TOP_K, T_ALL), I32),
                   jax.ShapeDtypeStruct((1, TE_LANES), I32),
                   jax.ShapeDtypeStruct((8, META_LANES), I32)],
        compiler_params=_cparams(),
        name="positions",
    )(ids, rank, cnt)


def _dispatch_kernel(pos_ref, meta_ref, h_ref, xs_ref, zero_ref, sem, pad_sem):
    i = pl.program_id(0)

    def row_copy(t, k):
        src = h_ref.at[pl.ds(pl.multiple_of(t * SLAB, SLAB), SLAB)]
        dst = xs_ref.at[pl.ds(pl.multiple_of(pos_ref[k, t] * SLAB, SLAB), SLAB)]
        return pltpu.make_async_copy(src, dst, sem)

    def issue(t, c):
        for k in range(TOP_K):
            row_copy(t, k).start(priority=k % 2)
        return c

    def drain(t, c):
        for k in range(TOP_K):
            row_copy(t, k).wait()
        return c

    lax.fori_loop(0, TR, issue, 0)
    lax.fori_loop(0, TR, drain, 0)

    @pl.when(i == N_TILES - 1)
    def _():
        zero_ref[...] = jnp.zeros_like(zero_ref)

        def per_expert(e, c):
            count = meta_ref[1, e]
            first = meta_ref[0, e] + count
            npad = ((count + (TM - 1)) // TM) * TM - count

            def pad_copy(r):
                dst = xs_ref.at[pl.ds(pl.multiple_of((first + r) * SLAB, SLAB), SLAB)]
                return pltpu.make_async_copy(zero_ref, dst, pad_sem)

            def pad_issue(r, c2):
                pad_copy(r).start()
                return c2

            def pad_drain(r, c2):
                pad_copy(r).wait()
                return c2

            lax.fori_loop(0, npad, pad_issue, 0)
            lax.fori_loop(0, npad, pad_drain, 0)
            return c

        lax.fori_loop(0, N_EXPERTS, per_expert, 0)


def _dispatch(pos, meta, h2):
    return pl.pallas_call(
        _dispatch_kernel,
        grid=(N_TILES,),
        in_specs=[pl.BlockSpec((TOP_K, TR), lambda i: (0, i), memory_space=pltpu.SMEM),
                  pl.BlockSpec(memory_space=pltpu.SMEM),
                  pl.BlockSpec((TR * SLAB, LANES), lambda i: (i, 0))],
        out_specs=pl.BlockSpec(memory_space=pl.ANY),
        out_shape=jax.ShapeDtypeStruct((P_SORT * SLAB, LANES), U32),
        scratch_shapes=[pltpu.VMEM((SLAB, LANES), U32),
                        pltpu.SemaphoreType.DMA(()),
                        pltpu.SemaphoreType.DMA(())],
        compiler_params=_cparams(("arbitrary",)),
        name="dispatch",
    )(pos, meta, h2)


def _experts_kernel(te_ref, nt_ref, x_ref, wg_ref, wu_ref, wd_ref, y_ref, wg_bf, wu_bf, wd_bf):
    j = pl.program_id(0)
    used = j < nt_ref[0]
    prev = te_ref[jnp.maximum(j - 1, 0)]

    @pl.when(used & ((j == 0) | (te_ref[j] != prev)))
    def _():
        wg_bf[...] = wg_ref[0, 0].astype(BF16)
        wu_bf[...] = wu_ref[0, 0].astype(BF16)
        wd_bf[...] = wd_ref[0, 0].astype(BF16)

    @pl.when(used)
    def _():
        xa, xb = _unpack_halves(_load_slabs(x_ref, TM))
        xa, xb = xa.astype(BF16), xb.astype(BF16)
        g = (jnp.dot(xa, wg_bf[:HALF], preferred_element_type=F32)
             + jnp.dot(xb, wg_bf[HALF:], preferred_element_type=F32))
        u = (jnp.dot(xa, wu_bf[:HALF], preferred_element_type=F32)
             + jnp.dot(xb, wu_bf[HALF:], preferred_element_type=F32))
        a = (_silu(g) * u).astype(BF16)
        _store_slabs(y_ref, _pack_halves(jnp.dot(a, wd_bf[...], preferred_element_type=F32)))


def _experts(layer, te, nt, xs, w_gate, w_up, w_down):
    n_tiles = xs.shape[0] // (TM * SLAB)
    tile = lambda j, te, nt: (jnp.minimum(j, nt[0] - 1), 0)
    wsel = lambda j, te, nt: (layer, te[jnp.minimum(j, nt[0] - 1)], 0, 0)
    return pl.pallas_call(
        _experts_kernel,
        grid_spec=pltpu.PrefetchScalarGridSpec(
            num_scalar_prefetch=2,
            grid=(n_tiles,),
            in_specs=[pl.BlockSpec((TM * SLAB, LANES), tile),
                      pl.BlockSpec((1, 1, D_MODEL, EXPERT_FF), wsel),
                      pl.BlockSpec((1, 1, D_MODEL, EXPERT_FF), wsel),
                      pl.BlockSpec((1, 1, EXPERT_FF, D_MODEL), wsel)],
            out_specs=pl.BlockSpec((TM * SLAB, LANES), tile),
            scratch_shapes=[pltpu.VMEM((D_MODEL, EXPERT_FF), BF16),
                            pltpu.VMEM((D_MODEL, EXPERT_FF), BF16),
                            pltpu.VMEM((EXPERT_FF, D_MODEL), BF16)]),
        out_shape=jax.ShapeDtypeStruct(xs.shape, U32),
        compiler_params=_cparams(("arbitrary",)),
        name="experts",
    )(te, nt, xs, w_gate, w_up, w_down)


def _combine_kernel(final, pos_ref, wts_ref, x1_ref, ysh_ref, mod_ref, g_ref, y_ref, *rest):
    *out_refs, buf_ref, x2_ref, wb_ref, sem = rest
    i = pl.program_id(0)

    def row_copy(t, k):
        src = y_ref.at[pl.ds(pl.multiple_of(pos_ref[k, t] * SLAB, SLAB), SLAB)]
        dst = buf_ref.at[k, pl.ds(pl.multiple_of(t * SLAB, SLAB), SLAB)]
        return pltpu.make_async_copy(src, dst, sem)

    def issue(t, c):
        for k in range(TOP_K):
            row_copy(t, k).start(priority=k % 2)
        return c

    def drain(t, c):
        for k in range(TOP_K):
            row_copy(t, k).wait()
        return c

    lax.fori_loop(0, TC_ROWS, issue, 0)
    lax.fori_loop(0, TC_ROWS, drain, 0)
    w = jnp.transpose(wts_ref[...])
    for k in range(TOP_K):
        wb_ref[k] = jnp.broadcast_to(w[:, k:k + 1], (TC_ROWS, LANES))
    x2_dst = x2_ref if final else out_refs[0]
    sq = jnp.zeros((TC_ROWS, LANES), F32)
    for s in range(SLAB):
        acc_a, acc_b = _unpack_halves(ysh_ref[pl.ds(s, TC_ROWS, stride=SLAB), :])
        for k in range(TOP_K):
            ya, yb = _unpack_halves(buf_ref[k, pl.ds(s, TC_ROWS, stride=SLAB), :])
            acc_a = acc_a + wb_ref[k] * ya
            acc_b = acc_b + wb_ref[k] * yb
        for half, acc in ((0, acc_a), (1, acc_b)):
            cols = slice(half * HALF + s * LANES, half * HALF + (s + 1) * LANES)
            x2 = x1_ref[:, cols] + mod_ref[0, 0, 5:6, cols] * acc
            x2_dst[:, cols] = x2
            if final:
                sq = sq + x2 * x2
    if not final:
        return
    ms = jnp.sum(sq, axis=-1, keepdims=True) * (1.0 / D_MODEL)
    out = x2_ref[...] * lax.rsqrt(ms + EPS) * g_ref[...]
    oc_ref, ol_ref = out_refs

    @pl.when(i < T_CTX // TC_ROWS)
    def _():
        oc_ref[...] = out

    @pl.when(i >= T_CTX // TC_ROWS)
    def _():
        ol_ref[...] = out


def _combine(layer, pos, wts, x1, ysh, mods, final_norm, y):
    final = layer == DEPTH - 1
    row = lambda i: (i, 0)
    col = lambda i: (0, i)
    n_ctx = T_CTX // TC_ROWS
    if final:
        out_specs = [pl.BlockSpec((TC_ROWS, D_MODEL), lambda i: (jnp.minimum(i, n_ctx - 1), 0)),
                     pl.BlockSpec((TC_ROWS, D_MODEL), lambda i: (jnp.maximum(i - n_ctx, 0), 0))]
        out_shape = [jax.ShapeDtypeStruct((T_CTX, D_MODEL), F32), jax.ShapeDtypeStruct((T_LAT, D_MODEL), F32)]
    else:
        out_specs = [pl.BlockSpec((TC_ROWS, D_MODEL), row)]
        out_shape = [jax.ShapeDtypeStruct((T_ALL, D_MODEL), F32)]
    return pl.pallas_call(
        functools.partial(_combine_kernel, final),
        grid=(T_ALL // TC_ROWS,),
        in_specs=[pl.BlockSpec((TOP_K, TC_ROWS), col, memory_space=pltpu.SMEM),
                  pl.BlockSpec((TOP_K, TC_ROWS), col),
                  pl.BlockSpec((TC_ROWS, D_MODEL), row),
                  pl.BlockSpec((TC_ROWS * SLAB, LANES), row),
                  _mod_spec(layer, TR // TC_ROWS),
                  pl.BlockSpec((1, D_MODEL), lambda i: (0, 0)),
                  pl.BlockSpec(memory_space=pl.ANY)],
        out_specs=out_specs,
        out_shape=out_shape,
        scratch_shapes=[pltpu.VMEM((TOP_K, TC_ROWS * SLAB, LANES), U32),
                        pltpu.VMEM((TC_ROWS, D_MODEL), F32),
                        pltpu.VMEM((TOP_K, TC_ROWS, LANES), F32),
                        pltpu.SemaphoreType.DMA(())],
        compiler_params=_cparams(("arbitrary",)),
        name="combine",
    )(pos, wts, x1, ysh, mods, final_norm.reshape(1, D_MODEL), y)


def kernel(x_prompt, x_sample, cache_k, cache_v, c, c_ctx, w_mod, b_mod, g_mix, g_ffn, w_in, w_out, conv_w,
           attn_sink, w_router, router_bias, w_gate_e, w_up_e, w_down_e, w_gate_s, w_up_s, w_down_s, final_norm):
    ccsc = jnp.asarray(_channel_dft(), BF16)
    csl_ctx = jnp.asarray(_dft_tables(SEQ), BF16)
    csl_lat = jnp.asarray(_dft_tables(DEC_SEQ), BF16)
    rope = tuple(jnp.asarray(t, F32) for t in _rope_tables(DEC_SEQ))
    tri = jnp.asarray(np.triu(np.ones((TR, TR)), 1), BF16)
    shared_te = jnp.zeros((T_ALL // TM,), I32)
    shared_nt = jnp.full((1,), T_ALL // TM, I32)

    cond = jnp.concatenate([c_ctx[None], c, jnp.zeros((N_COND - 1 - DEC_BATCH, D_MODEL), F32)], axis=0)
    mods = _mods(cond, w_mod, b_mod).reshape(DEPTH, N_COND, N_MOD, D_MODEL)
    cache_k = cache_k.reshape(DEC_BATCH, DEPTH, SEQ, KV_W)
    cache_v = cache_v.reshape(DEC_BATCH, DEPTH, SEQ, KV_W)
    w_in_bf, w_out_bf = w_in.astype(BF16), w_out.astype(BF16)
    wr_t = jnp.swapaxes(w_router, 1, 2)
    wr_hi = wr_t.astype(BF16)
    wr_split = jnp.concatenate([wr_hi, (wr_t - wr_hi.astype(F32)).astype(BF16)], axis=1)
    ws_gate, ws_up, ws_down = w_gate_s[:, None], w_up_s[:, None], w_down_s[:, None]
    x = jnp.concatenate([x_prompt.reshape(T_CTX, D_MODEL), x_sample.reshape(T_LAT, D_MODEL)], axis=0)

    new_k, new_v = [], []
    for l in range(DEPTH):
        q, kv, fm = _inproj(l, x, mods, g_mix, w_in_bf, rope)
        new_k.append(kv[:T_CTX, :KV_W].reshape(BATCH, SEQ, N_KV_HEADS, HEAD_DIM))
        new_v.append(kv[:T_CTX, KV_W:].reshape(BATCH, SEQ, N_KV_HEADS, HEAD_DIM))
        mix_ctx = _ctx_mix(l, attn_sink, q, kv, fm, ccsc, csl_ctx, conv_w)
        mix_lat = _lat_mix(l, attn_sink, q, kv, fm, cache_k, cache_v, ccsc, csl_lat, conv_w)
        x1, h2, ids, wts, rank, cnt = _outproj(l, mix_ctx, mix_lat, x, mods, g_ffn, w_out_bf, wr_split,
                                               router_bias, tri)
        pos, te, meta = _positions(ids, rank, cnt)
        xs = _dispatch(pos, meta, h2)
        y = _experts(l, te[0, :N_SORT_TILES], meta[2, :1], xs, w_gate_e, w_up_e, w_down_e)
        ysh = _experts(l, shared_te, shared_nt, h2, ws_gate, ws_up, ws_down)
        outs = _combine(l, pos, wts, x1, ysh, mods, final_norm, y)
        x = outs[0]

    y_prompt = outs[0].reshape(BATCH, SEQ, D_MODEL)
    y_sample = outs[1].reshape(DEC_BATCH, DEC_SEQ, D_MODEL)
    return y_prompt, y_sample, jnp.stack(new_k, axis=1), jnp.stack(new_v, axis=1)
```

```python
import functools

import numpy as np
import jax
import jax.numpy as jnp
from jax import lax
from jax.experimental import pallas as pl
from jax.experimental.pallas import tpu as pltpu
from jax.experimental.pallas import tpu_sc as plsc

F32 = jnp.float32
BF16 = jnp.bfloat16
I32 = jnp.int32
U32 = jnp.uint32

D_MODEL = 2048
HALF = D_MODEL // 2
BATCH, SEQ = 32, 256
DEC_BATCH, DEC_SEQ = 4, 1024
DEPTH = 2
GRID_W = 64
HEAD_DIM = 128
N_HEADS, N_KV_HEADS, Q_PER_KV = 8, 2, 4
ATTN_W, KV_W = 1024, 256
WINDOW, BLOCK = 128, 128
SCALE = HEAD_DIM ** -0.5
ROPE_BASE = 10000.0
ROPE_PAIRS = 32
FOURIER_W, FOURIER_GROUPS, FOURIER_GROUP_W = 512, 4, 128
CONV_W = 512
IN_W = 3584
N_EXPERTS, TOP_K, N_GROUPS, EXPERTS_PER_GROUP, TOPK_GROUPS = 64, 8, 8, 8, 4
EXPERT_FF = 512
ROUTE_SCALE = 2.5
N_MOD = 6
EPS = 1e-6
NEG = -1e30

T_CTX = BATCH * SEQ
T_LAT = DEC_BATCH * DEC_SEQ
T_ALL = T_CTX + T_LAT
TR = 512
N_CTX_TILES = T_CTX // TR
LAT_TILES_PER_SEQ = DEC_SEQ // TR
N_TILES = T_ALL // TR
N_COND = 8
MOD_TN = 1024
TM = 256
TC_ROWS = 128
N_SORT_TILES = T_ALL * TOP_K // TM + N_EXPERTS
P_SORT = N_SORT_TILES * TM
META_LANES = 128
TE_LANES = 512
VMEM_LIMIT = 56 * 1024 * 1024


def _cparams(sem=None):
    return pltpu.CompilerParams(dimension_semantics=sem, vmem_limit_bytes=VMEM_LIMIT)


def _silu(x):
    return x / (1.0 + jnp.exp(-x))


def _sigmoid(x):
    return 1.0 / (1.0 + jnp.exp(-x))


def _rmsnorm(x, g):
    return x * lax.rsqrt(jnp.mean(x * x, axis=-1, keepdims=True) + EPS) * g


def _cond_row(i):
    return jnp.where(i < N_CTX_TILES, 0, 1 + (i - N_CTX_TILES) // LAT_TILES_PER_SEQ)


def _pack_halves(v):
    n = v.shape[1] // 2
    hi = pltpu.bitcast(v[:, :n].astype(BF16).astype(F32), U32)
    lo = pltpu.bitcast(v[:, n:].astype(BF16).astype(F32), U32)
    return hi | (lo >> 16)


def _unpack_halves(w):
    return pltpu.bitcast(w & jnp.uint32(0xFFFF0000), F32), pltpu.bitcast(w << 16, F32)


SLAB, LANES = 8, 128


def _load_slabs(ref, n, lead=()):
    return jnp.concatenate([ref[lead + (pl.ds(s, n, stride=SLAB), slice(None))] for s in range(SLAB)], axis=1)


def _store_slabs(ref, words):
    n = words.shape[0]
    for s in range(SLAB):
        ref[pl.ds(s, n, stride=SLAB), :] = words[:, s * LANES:(s + 1) * LANES]


def _dft_tables(L):
    j = np.arange(L, dtype=np.int64)
    ang = 2.0 * np.pi * ((j[:, None] * j[None, :]) % L).astype(np.float64) / L
    c, s = np.cos(ang) / np.sqrt(L), np.sin(ang) / np.sqrt(L)
    return np.concatenate([c, -s], axis=1)


def _channel_dft():
    n = FOURIER_GROUP_W
    j = np.arange(n, dtype=np.int64)
    ang = 2.0 * np.pi * ((j[:, None] * j[None, :]) % n).astype(np.float64) / n
    return np.concatenate([np.cos(ang), np.sin(ang)], axis=1) / np.sqrt(n)


def _rope_tables(S):
    pos = np.arange(S)
    row, col = (pos // GRID_W).astype(np.float64), (pos % GRID_W).astype(np.float64)
    inv = ROPE_BASE ** (-np.arange(ROPE_PAIRS, dtype=np.float64) / ROPE_PAIRS)
    ar, ac = row[:, None] * inv, col[:, None] * inv
    z = np.zeros_like(ar)
    cos = np.concatenate([np.cos(ar), np.cos(ar), np.cos(ac), np.cos(ac)], axis=1)
    sin_hi = np.concatenate([-np.sin(ar), z, -np.sin(ac), z], axis=1)
    sin_lo = np.concatenate([z, np.sin(ar), z, np.sin(ac)], axis=1)
    return cos, sin_hi, sin_lo


def _mods_kernel(cond_ref, w_ref, b_ref, o_ref):
    s = _silu(cond_ref[...]).astype(BF16)
    o_ref[0] = jnp.dot(s, w_ref[0].astype(BF16), preferred_element_type=F32) + b_ref[0]


def _mods(cond, w_mod, b_mod):
    n = N_MOD * D_MODEL
    return pl.pallas_call(
        _mods_kernel,
        grid=(DEPTH, n // MOD_TN),
        in_specs=[pl.BlockSpec((N_COND, D_MODEL), lambda l, j: (0, 0)),
                  pl.BlockSpec((1, D_MODEL, MOD_TN), lambda l, j: (l, 0, j)),
                  pl.BlockSpec((1, 1, MOD_TN), lambda l, j: (l, 0, j))],
        out_specs=pl.BlockSpec((1, N_COND, MOD_TN), lambda l, j: (l, 0, j)),
        out_shape=jax.ShapeDtypeStruct((DEPTH, N_COND, n), F32),
        compiler_params=_cparams(("arbitrary", "arbitrary")),
        name="mods",
    )(cond, w_mod, b_mod.reshape(DEPTH, 1, n))


def _mod_spec(layer, tiles_per_tr=1):
    return pl.BlockSpec((1, 1, N_MOD, D_MODEL), lambda i: (layer, _cond_row(i // tiles_per_tr), 0, 0))


def _inproj_kernel(x_ref, mod_ref, g_ref, w_ref, cos_ref, shi_ref, slo_ref, q_ref, kv_ref, fm_ref):
    i = pl.program_id(0)
    shift, scale = mod_ref[0, 0, 0:1, :], mod_ref[0, 0, 1:2, :]
    h = (_rmsnorm(x_ref[...], g_ref[0]) * (1.0 + scale) + shift).astype(BF16)
    fm_ref[...] = jnp.dot(h, w_ref[0, :, ATTN_W + 2 * KV_W:], preferred_element_type=F32).astype(BF16)
    qkv = jnp.dot(h, w_ref[0, :, :ATTN_W + 2 * KV_W], preferred_element_type=F32)
    kv_ref[:, KV_W:] = qkv[:, ATTN_W + KV_W:]

    @pl.when(i < N_CTX_TILES)
    def _():
        q_ref[...] = qkv[:, :ATTN_W].astype(BF16)
        kv_ref[:, :KV_W] = qkv[:, ATTN_W:ATTN_W + KV_W]

    @pl.when(i >= N_CTX_TILES)
    def _():
        cos, shi, slo = cos_ref[...], shi_ref[...], slo_ref[...]
        for hd in range(N_HEADS + N_KV_HEADS):
            xh = qkv[:, hd * HEAD_DIM:(hd + 1) * HEAD_DIM]
            r = xh * cos + pltpu.roll(xh, HEAD_DIM - 32, 1) * shi + pltpu.roll(xh, 32, 1) * slo
            if hd < N_HEADS:
                q_ref[:, hd * HEAD_DIM:(hd + 1) * HEAD_DIM] = r.astype(BF16)
            else:
                kv_ref[:, (hd - N_HEADS) * HEAD_DIM:(hd - N_HEADS + 1) * HEAD_DIM] = r


def _inproj(layer, x, mods, g_mix, w_in_bf, rope):
    lat_blk = lambda i: (jnp.maximum(i - N_CTX_TILES, 0) % LAT_TILES_PER_SEQ, 0)
    row = lambda i: (i, 0)
    return pl.pallas_call(
        _inproj_kernel,
        grid=(N_TILES,),
        in_specs=[pl.BlockSpec((TR, D_MODEL), row),
                  _mod_spec(layer),
                  pl.BlockSpec((1, 1, D_MODEL), lambda i: (layer, 0, 0)),
                  pl.BlockSpec((1, D_MODEL, IN_W), lambda i: (layer, 0, 0), pipeline_mode=pl.Buffered(1)),
                  pl.BlockSpec((TR, HEAD_DIM), lat_blk),
                  pl.BlockSpec((TR, HEAD_DIM), lat_blk),
                  pl.BlockSpec((TR, HEAD_DIM), lat_blk)],
        out_specs=[pl.BlockSpec((TR, ATTN_W), row),
                   pl.BlockSpec((TR, 2 * KV_W), row),
                   pl.BlockSpec((TR, FOURIER_W + 3 * CONV_W), row)],
        out_shape=[jax.ShapeDtypeStruct((T_ALL, ATTN_W), BF16),
                   jax.ShapeDtypeStruct((T_ALL, 2 * KV_W), F32),
                   jax.ShapeDtypeStruct((T_ALL, FOURIER_W + 3 * CONV_W), BF16)],
        compiler_params=_cparams(("arbitrary",)),
        name="inproj",
    )(x, mods, g_mix.reshape(DEPTH, 1, D_MODEL), w_in_bf, *rope)


def _fourier_conv(fm_ref, ccsc_ref, csl_ref, cw_ref, o_ref):
    L = fm_ref.shape[0]
    for g in range(FOURIER_GROUPS):
        xg = fm_ref[:, g * FOURIER_GROUP_W:(g + 1) * FOURIER_GROUP_W]
        z = jnp.dot(xg, ccsc_ref[...], preferred_element_type=F32)
        zz = jnp.concatenate([z[:, :FOURIER_GROUP_W], z[:, FOURIER_GROUP_W:]], axis=0).astype(BF16)
        f = jnp.dot(csl_ref[...], zz, preferred_element_type=F32)
        o_ref[:, ATTN_W + g * FOURIER_GROUP_W:ATTN_W + (g + 1) * FOURIER_GROUP_W] = f.astype(BF16)
    xc = fm_ref[:, FOURIER_W:FOURIER_W + CONV_W].astype(F32)
    bg = fm_ref[:, FOURIER_W + CONV_W:FOURIER_W + 2 * CONV_W].astype(F32)
    cg = fm_ref[:, FOURIER_W + 2 * CONV_W:].astype(F32)
    u = cg * xc
    rows = lax.broadcasted_iota(I32, u.shape, 0)
    prev = jnp.where(rows == 0, 0.0, pltpu.roll(u, 1, 0))
    nxt = jnp.where(rows == L - 1, 0.0, pltpu.roll(u, L - 1, 0))
    y = prev * cw_ref[0, 0:1, :] + u * cw_ref[0, 1:2, :] + nxt * cw_ref[0, 2:3, :]
    o_ref[:, ATTN_W + FOURIER_W:] = (bg * y).astype(BF16)


def _ctx_mix_kernel(layer, sink_ref, q_ref, kv_ref, fm_ref, ccsc_ref, csl_ref, cw_ref, o_ref):
    for h in range(N_HEADS):
        g = h // Q_PER_KV
        q = q_ref[:, h * HEAD_DIM:(h + 1) * HEAD_DIM]
        k = kv_ref[:, g * HEAD_DIM:(g + 1) * HEAD_DIM].astype(BF16)
        v = kv_ref[:, KV_W + g * HEAD_DIM:KV_W + (g + 1) * HEAD_DIM].astype(BF16)
        s = lax.dot_general(q, k, (((1,), (1,)), ((), ())), preferred_element_type=F32) * SCALE
        sink = sink_ref[layer, h]
        m = jnp.maximum(jnp.max(s, axis=-1, keepdims=True), sink)
        e = jnp.exp(s - m)
        den = jnp.sum(e, axis=-1, keepdims=True) + jnp.exp(sink - m)
        o = jnp.dot(e.astype(BF16), v, preferred_element_type=F32) / den
        o_ref[:, h * HEAD_DIM:(h + 1) * HEAD_DIM] = o.astype(BF16)
    _fourier_conv(fm_ref, ccsc_ref, csl_ref, cw_ref, o_ref)


def _ctx_mix(layer, sink, q, kv, fm, ccsc, csl, conv_w):
    row = lambda b: (b, 0)
    const = lambda b: (0, 0)
    return pl.pallas_call(
        functools.partial(_ctx_mix_kernel, layer),
        grid=(BATCH,),
        in_specs=[pl.BlockSpec(memory_space=pltpu.SMEM),
                  pl.BlockSpec((SEQ, ATTN_W), row),
                  pl.BlockSpec((SEQ, 2 * KV_W), row),
                  pl.BlockSpec((SEQ, FOURIER_W + 3 * CONV_W), row),
                  pl.BlockSpec((FOURIER_GROUP_W, 2 * FOURIER_GROUP_W), const),
                  pl.BlockSpec((SEQ, 2 * SEQ), const),
                  pl.BlockSpec((1, 3, CONV_W), lambda b: (layer, 0, 0))],
        out_specs=pl.BlockSpec((SEQ, D_MODEL), row),
        out_shape=jax.ShapeDtypeStruct((T_CTX, D_MODEL), BF16),
        compiler_params=_cparams(("arbitrary",)),
        name="ctx_mix",
    )(sink, q, kv, fm, ccsc, csl, conv_w)


def _lat_mix_kernel(layer, sink_ref, q_ref, kv_ref, fm_ref, ck_ref, cv_ref, ccsc_ref, csl_ref, cw_ref, o_ref):
    n = pl.program_id(1)
    nb = DEC_SEQ // BLOCK
    rows = Q_PER_KV * BLOCK
    band = 3 * BLOCK

    @pl.when(n == 0)
    def _():
        _fourier_conv(fm_ref, ccsc_ref, csl_ref, cw_ref, o_ref)

    q0 = pl.multiple_of(n * BLOCK, BLOCK)
    k0 = pl.multiple_of(jnp.clip(n - 1, 0, nb - 3) * BLOCK, BLOCK)
    qpos = q0 + lax.broadcasted_iota(I32, (rows, band), 0) % BLOCK
    kpos = k0 + lax.broadcasted_iota(I32, (rows, band), 1)
    valid = jnp.abs(qpos - kpos) <= WINDOW
    for g in range(N_KV_HEADS):
        kb = kv_ref[pl.ds(k0, band), g * HEAD_DIM:(g + 1) * HEAD_DIM].astype(BF16)
        vb = kv_ref[pl.ds(k0, band), KV_W + g * HEAD_DIM:KV_W + (g + 1) * HEAD_DIM].astype(BF16)
        k_ctx = ck_ref[0, 0, :, g * HEAD_DIM:(g + 1) * HEAD_DIM].astype(BF16)
        v_ctx = cv_ref[0, 0, :, g * HEAD_DIM:(g + 1) * HEAD_DIM].astype(BF16)
        sink = jnp.concatenate(
            [jnp.full((BLOCK, 1), sink_ref[layer, g * Q_PER_KV + j], F32) for j in range(Q_PER_KV)], axis=0)
        q = jnp.concatenate(
            [q_ref[pl.ds(q0, BLOCK), (g * Q_PER_KV + j) * HEAD_DIM:(g * Q_PER_KV + j + 1) * HEAD_DIM]
             for j in range(Q_PER_KV)], axis=0)
        s_loc = lax.dot_general(q, kb, (((1,), (1,)), ((), ())), preferred_element_type=F32) * SCALE
        s_loc = jnp.where(valid, s_loc, NEG)
        s_ctx = lax.dot_general(q, k_ctx, (((1,), (1,)), ((), ())), preferred_element_type=F32) * SCALE
        m = jnp.maximum(jnp.maximum(jnp.max(s_loc, axis=-1, keepdims=True),
                                    jnp.max(s_ctx, axis=-1, keepdims=True)), sink)
        e_loc, e_ctx = jnp.exp(s_loc - m), jnp.exp(s_ctx - m)
        den = (jnp.sum(e_loc, axis=-1, keepdims=True) + jnp.sum(e_ctx, axis=-1, keepdims=True)
               + jnp.exp(sink - m))
        o = (jnp.dot(e_loc.astype(BF16), vb, preferred_element_type=F32)
             + jnp.dot(e_ctx.astype(BF16), v_ctx, preferred_element_type=F32)) / den
        for j in range(Q_PER_KV):
            h = g * Q_PER_KV + j
            o_ref[pl.ds(q0, BLOCK), h * HEAD_DIM:(h + 1) * HEAD_DIM] = o[j * BLOCK:(j + 1) * BLOCK].astype(BF16)


def _lat_mix(layer, sink, q, kv, fm, cache_k, cache_v, ccsc, csl, conv_w):
    off = T_CTX // DEC_SEQ
    row = lambda b, n: (off + b, 0)
    const = lambda b, n: (0, 0)
    cache = lambda b, n: (b, layer, 0, 0)
    return pl.pallas_call(
        functools.partial(_lat_mix_kernel, layer),
        grid=(DEC_BATCH, DEC_SEQ // BLOCK),
        in_specs=[pl.BlockSpec(memory_space=pltpu.SMEM),
                  pl.BlockSpec((DEC_SEQ, ATTN_W), row),
                  pl.BlockSpec((DEC_SEQ, 2 * KV_W), row),
                  pl.BlockSpec((DEC_SEQ, FOURIER_W + 3 * CONV_W), row),
                  pl.BlockSpec((1, 1, SEQ, KV_W), cache),
                  pl.BlockSpec((1, 1, SEQ, KV_W), cache),
                  pl.BlockSpec((FOURIER_GROUP_W, 2 * FOURIER_GROUP_W), const),
                  pl.BlockSpec((DEC_SEQ, 2 * DEC_SEQ), const, pipeline_mode=pl.Buffered(1)),
                  pl.BlockSpec((1, 3, CONV_W), lambda b, n: (layer, 0, 0))],
        out_specs=pl.BlockSpec((DEC_SEQ, D_MODEL), lambda b, n: (b, 0)),
        out_shape=jax.ShapeDtypeStruct((T_LAT, D_MODEL), BF16),
        compiler_params=_cparams(("arbitrary", "arbitrary")),
        name="lat_mix",
    )(sink, q, kv, fm, cache_k, cache_v, ccsc, csl, conv_w)


def _first_index_of_max(vals, iota, sentinel):
    mx = jnp.max(vals, axis=0, keepdims=True)
    idx = jnp.min(jnp.where(vals == mx, iota, sentinel), axis=0, keepdims=True)
    return mx, idx


def _outproj_kernel(mixc_ref, mixl_ref, x_ref, mod_ref, g_ref, w_ref, wr_ref, rb_ref, tri_ref,
                    x1_ref, h2_ref, ids_ref, wts_ref, rank_ref, cnt_ref, run_ref):
    i = pl.program_id(0)

    @pl.when(i == 0)
    def _():
        run_ref[...] = jnp.zeros_like(run_ref)

    gate1, shift2, scale2 = mod_ref[0, 0, 2:3, :], mod_ref[0, 0, 3:4, :], mod_ref[0, 0, 4:5, :]
    mix = jnp.where(i < N_CTX_TILES, mixc_ref[...], mixl_ref[...])
    x1 = x_ref[...] + gate1 * jnp.dot(mix, w_ref[0], preferred_element_type=F32)
    x1_ref[...] = x1
    h2 = _rmsnorm(x1, g_ref[0]) * (1.0 + scale2) + shift2
    _store_slabs(h2_ref, _pack_halves(h2))

    nt = (((1,), (1,)), ((), ()))
    h_hi = h2.astype(BF16)
    h_lo = (h2 - h_hi.astype(F32)).astype(BF16)
    part = lax.dot_general(wr_ref[0], h_hi, nt, preferred_element_type=F32)
    logits = (part[:N_EXPERTS] + part[N_EXPERTS:]
              + lax.dot_general(wr_ref[0, :N_EXPERTS, :], h_lo, nt, preferred_element_type=F32))
    scores = _sigmoid(logits)
    biased = scores + rb_ref[0]
    ninf = -jnp.inf
    mem = lax.broadcasted_iota(I32, (EXPERTS_PER_GROUP, TR), 0)
    grp_rows = []
    for g in range(N_GROUPS):
        bgp = biased[g * EXPERTS_PER_GROUP:(g + 1) * EXPERTS_PER_GROUP]
        m1, i1 = _first_index_of_max(bgp, mem, EXPERTS_PER_GROUP)
        m2 = jnp.max(jnp.where(mem == i1, ninf, bgp), axis=0, keepdims=True)
        grp_rows.append(m1 + m2)
    cur = jnp.concatenate(grp_rows, axis=0)
    gio = lax.broadcasted_iota(I32, (N_GROUPS, TR), 0)
    gsel = jnp.zeros((N_GROUPS, TR), F32)
    for _ in range(TOPK_GROUPS):
        _, gi = _first_index_of_max(cur, gio, N_GROUPS)
        hit = gio == gi
        gsel = jnp.where(hit, 1.0, gsel)
        cur = jnp.where(hit, ninf, cur)
    cur = jnp.concatenate(
        [jnp.where(gsel[g:g + 1] > 0.0, biased[g * EXPERTS_PER_GROUP:(g + 1) * EXPERTS_PER_GROUP], ninf)
         for g in range(N_GROUPS)], axis=0)
    eio = lax.broadcasted_iota(I32, (N_EXPERTS, TR), 0)
    chosen = jnp.zeros((N_EXPERTS, TR), F32)
    ids, sel = [], []
    for _ in range(TOP_K):
        _, ei = _first_index_of_max(cur, eio, N_EXPERTS)
        hit = eio == ei
        ids.append(ei)
        sel.append(jnp.sum(jnp.where(hit, scores, 0.0), axis=0, keepdims=True))
        chosen = jnp.where(hit, 1.0, chosen)
        cur = jnp.where(hit, ninf, cur)
    total = functools.reduce(lambda a, b: a + b, sel)
    ids_ref[...] = jnp.concatenate(ids, axis=0)
    wts_ref[...] = jnp.concatenate([s / total * ROUTE_SCALE for s in sel], axis=0)

    before = jnp.dot(chosen.astype(BF16), tri_ref[...], preferred_element_type=F32) + run_ref[:, 0:1]
    rank_ref[...] = jnp.concatenate(
        [jnp.sum(jnp.where(eio == ei, before, 0.0), axis=0, keepdims=True) for ei in ids], axis=0).astype(I32)
    run = run_ref[...] + jnp.sum(chosen, axis=1, keepdims=True)
    run_ref[...] = run
    cnt_ref[...] = run


def _outproj(layer, mix_ctx, mix_lat, x, mods, g_ffn, w_out_bf, wr_split, router_bias, tri):
    row = lambda i: (i, 0)
    col = lambda i: (0, i)
    const = lambda i: (0, 0)
    lay = lambda i: (layer, 0, 0)
    return pl.pallas_call(
        _outproj_kernel,
        grid=(N_TILES,),
        in_specs=[pl.BlockSpec((TR, D_MODEL), lambda i: (jnp.minimum(i, N_CTX_TILES - 1), 0)),
                  pl.BlockSpec((TR, D_MODEL), lambda i: (jnp.maximum(i - N_CTX_TILES, 0), 0)),
                  pl.BlockSpec((TR, D_MODEL), row),
                  _mod_spec(layer),
                  pl.BlockSpec((1, 1, D_MODEL), lay),
                  pl.BlockSpec((1, D_MODEL, D_MODEL), lay, pipeline_mode=pl.Buffered(1)),
                  pl.BlockSpec((1, 2 * N_EXPERTS, D_MODEL), lay),
                  pl.BlockSpec((1, N_EXPERTS, 1), lay),
                  pl.BlockSpec((TR, TR), const)],
        out_specs=[pl.BlockSpec((TR, D_MODEL), row),
                   pl.BlockSpec((TR * SLAB, LANES), row),
                   pl.BlockSpec((TOP_K, TR), col),
                   pl.BlockSpec((TOP_K, TR), col),
                   pl.BlockSpec((TOP_K, TR), col),
                   pl.BlockSpec((N_EXPERTS, META_LANES), const)],
        out_shape=[jax.ShapeDtypeStruct((T_ALL, D_MODEL), F32),
                   jax.ShapeDtypeStruct((T_ALL * SLAB, LANES), U32),
                   jax.ShapeDtypeStruct((TOP_K, T_ALL), I32),
                   jax.ShapeDtypeStruct((TOP_K, T_ALL), F32),
                   jax.ShapeDtypeStruct((TOP_K, T_ALL), I32),
                   jax.ShapeDtypeStruct((N_EXPERTS, META_LANES), F32)],
        scratch_shapes=[pltpu.VMEM((N_EXPERTS, META_LANES), F32)],
        compiler_params=_cparams(("arbitrary",)),
        name="outproj",
    )(mix_ctx, mix_lat, x, mods, g_ffn.reshape(DEPTH, 1, D_MODEL), w_out_bf, wr_split,
      router_bias.reshape(DEPTH, N_EXPERTS, 1), tri)


def _positions_kernel(ids_ref, rank_ref, cnt_ref, pos_ref, te_ref, meta_ref):
    ids = ids_ref[...]
    pos = rank_ref[...]
    tile_io = lax.broadcasted_iota(I32, (1, TE_LANES), 1)
    lane_io = lax.broadcasted_iota(I32, (1, META_LANES), 1)
    te = jnp.zeros((1, TE_LANES), I32)
    starts = jnp.zeros((1, META_LANES), I32)
    counts = jnp.zeros((1, META_LANES), I32)
    start = jnp.zeros((1, 1), I32)
    for e in range(N_EXPERTS):
        c = cnt_ref[e:e + 1, 0:1].astype(I32)
        pos = pos + jnp.where(ids == e, start, 0)
        starts = starts + jnp.where(lane_io == e, start, 0)
        counts = counts + jnp.where(lane_io == e, c, 0)
        start = start + ((c + (TM - 1)) // TM) * TM
        te = te + jnp.where(tile_io * TM >= start, 1, 0)
    pos_ref[...] = pos
    te_ref[...] = jnp.minimum(te, N_EXPERTS - 1)
    meta_ref[0:1, :] = starts
    meta_ref[1:2, :] = counts
    meta_ref[2:3, :] = jnp.broadcast_to(start // TM, (1, META_LANES))
    meta_ref[3:8, :] = jnp.zeros((5, META_LANES), I32)


def _positions(ids, rank, cnt):
    return pl.pallas_call(
        _positions_kernel,
        out_shape=[jax.ShapeDtypeStruct((TOP_K, T_ALL), I32),
                   jax.ShapeDtypeStruct((1, TE_LANES), I32),
                   jax.ShapeDtypeStruct((8, META_LANES), I32)],
        compiler_params=_cparams(),
        name="positions",
    )(ids, rank, cnt)


SC_CORES, SC_SUBCORES = 2, 16
SC_WORKERS = SC_CORES * SC_SUBCORES
SC_CHUNK = 32


def _sc_kernel(body, out_type, name):
    return pl.kernel(
        body,
        out_type=out_type,
        mesh=plsc.VectorSubcoreMesh(core_axis_name="c", subcore_axis_name="s",
                                    num_cores=SC_CORES, num_subcores=SC_SUBCORES),
        scratch_types=[pltpu.VMEM((SC_CHUNK,), I32),
                       pltpu.VMEM((SC_CHUNK, SLAB, LANES), U32),
                       pltpu.SemaphoreType.DMA],
        compiler_params=pltpu.CompilerParams(use_tc_tiling_on_sc=True),
        name=name,
    )


def _sc_worker():
    return lax.axis_index("s") * SC_CORES + lax.axis_index("c")


def _dispatch(pos_flat, h3):
    per_worker = T_ALL // SC_WORKERS

    def body(pos_hbm, h_hbm, xs_hbm, idx_v, rows_v, sem):
        base = _sc_worker() * per_worker

        @pl.loop(0, per_worker // SC_CHUNK)
        def _(c):
            t0 = pl.multiple_of(base + c * SC_CHUNK, SC_CHUNK)
            pltpu.sync_copy(h_hbm.at[pl.ds(t0, SC_CHUNK)], rows_v)
            for k in range(TOP_K):
                pltpu.sync_copy(pos_hbm.at[pl.ds(k * T_ALL + t0, SC_CHUNK)], idx_v)
                pltpu.async_copy(rows_v, xs_hbm.at[idx_v], sem).wait()

    return _sc_kernel(body, jax.ShapeDtypeStruct((P_SORT, SLAB, LANES), U32), "sc_dispatch")(pos_flat, h3)


def _gather(pos_flat, y3):
    n_items = TOP_K * T_ALL
    per_worker = n_items // SC_WORKERS

    def body(pos_hbm, y_hbm, out_hbm, idx_v, rows_v, sem):
        base = _sc_worker() * per_worker

        @pl.loop(0, per_worker // SC_CHUNK)
        def _(c):
            i0 = pl.multiple_of(base + c * SC_CHUNK, SC_CHUNK)
            pltpu.sync_copy(pos_hbm.at[pl.ds(i0, SC_CHUNK)], idx_v)
            pltpu.async_copy(y_hbm.at[idx_v], rows_v, sem).wait()
            pltpu.sync_copy(rows_v, out_hbm.at[pl.ds(i0, SC_CHUNK)])

    return _sc_kernel(body, jax.ShapeDtypeStruct((n_items, SLAB, LANES), U32), "sc_gather")(pos_flat, y3)


def _experts_kernel(te_ref, nt_ref, end_ref, x_ref, wg_ref, wu_ref, wd_ref, y_ref, wg_bf, wu_bf, wd_bf):
    j = pl.program_id(0)
    used = j < nt_ref[0]
    prev = te_ref[jnp.maximum(j - 1, 0)]

    @pl.when(used & ((j == 0) | (te_ref[j] != prev)))
    def _():
        wg_bf[...] = wg_ref[0, 0].astype(BF16)
        wu_bf[...] = wu_ref[0, 0].astype(BF16)
        wd_bf[...] = wd_ref[0, 0].astype(BF16)

    @pl.when(used)
    def _():
        valid = end_ref[te_ref[j]] - j * TM
        rows = lax.broadcasted_iota(I32, (TM, HALF), 0)
        words = jnp.where(rows < valid, _load_slabs(x_ref, TM), jnp.uint32(0))
        xa, xb = _unpack_halves(words)
        xa, xb = xa.astype(BF16), xb.astype(BF16)
        g = (jnp.dot(xa, wg_bf[:HALF], preferred_element_type=F32)
             + jnp.dot(xb, wg_bf[HALF:], preferred_element_type=F32))
        u = (jnp.dot(xa, wu_bf[:HALF], preferred_element_type=F32)
             + jnp.dot(xb, wu_bf[HALF:], preferred_element_type=F32))
        a = (_silu(g) * u).astype(BF16)
        _store_slabs(y_ref, _pack_halves(jnp.dot(a, wd_bf[...], preferred_element_type=F32)))


def _experts(layer, te, nt, end, xs, w_gate, w_up, w_down):
    n_tiles = xs.shape[0] // (TM * SLAB)
    tile = lambda j, te, nt, end: (jnp.minimum(j, nt[0] - 1), 0)
    wsel = lambda j, te, nt, end: (layer, te[jnp.minimum(j, nt[0] - 1)], 0, 0)
    return pl.pallas_call(
        _experts_kernel,
        grid_spec=pltpu.PrefetchScalarGridSpec(
            num_scalar_prefetch=3,
            grid=(n_tiles,),
            in_specs=[pl.BlockSpec((TM * SLAB, LANES), tile),
                      pl.BlockSpec((1, 1, D_MODEL, EXPERT_FF), wsel),
                      pl.BlockSpec((1, 1, D_MODEL, EXPERT_FF), wsel),
                      pl.BlockSpec((1, 1, EXPERT_FF, D_MODEL), wsel)],
            out_specs=pl.BlockSpec((TM * SLAB, LANES), tile),
            scratch_shapes=[pltpu.VMEM((D_MODEL, EXPERT_FF), BF16),
                            pltpu.VMEM((D_MODEL, EXPERT_FF), BF16),
                            pltpu.VMEM((EXPERT_FF, D_MODEL), BF16)]),
        out_shape=jax.ShapeDtypeStruct(xs.shape, U32),
        compiler_params=_cparams(("arbitrary",)),
        name="experts",
    )(te, nt, end, xs, w_gate, w_up, w_down)


def _combine_kernel(final, wts_ref, x1_ref, ysh_ref, mod_ref, g_ref, buf_ref, *rest):
    *out_refs, x2_ref, wb_ref = rest
    i = pl.program_id(0)
    w = jnp.transpose(wts_ref[...])
    for k in range(TOP_K):
        wb_ref[k] = jnp.broadcast_to(w[:, k:k + 1], (TC_ROWS, LANES))
    x2_dst = x2_ref if final else out_refs[0]
    sq = jnp.zeros((TC_ROWS, LANES), F32)
    for s in range(SLAB):
        acc_a, acc_b = _unpack_halves(ysh_ref[pl.ds(s, TC_ROWS, stride=SLAB), :])
        for k in range(TOP_K):
            ya, yb = _unpack_halves(buf_ref[k, pl.ds(s, TC_ROWS, stride=SLAB), :])
            acc_a = acc_a + wb_ref[k] * ya
            acc_b = acc_b + wb_ref[k] * yb
        for half, acc in ((0, acc_a), (1, acc_b)):
            cols = slice(half * HALF + s * LANES, half * HALF + (s + 1) * LANES)
            x2 = x1_ref[:, cols] + mod_ref[0, 0, 5:6, cols] * acc
            x2_dst[:, cols] = x2
            if final:
                sq = sq + x2 * x2
    if not final:
        return
    ms = jnp.sum(sq, axis=-1, keepdims=True) * (1.0 / D_MODEL)
    out = x2_ref[...] * lax.rsqrt(ms + EPS) * g_ref[...]
    oc_ref, ol_ref = out_refs

    @pl.when(i < T_CTX // TC_ROWS)
    def _():
        oc_ref[...] = out

    @pl.when(i >= T_CTX // TC_ROWS)
    def _():
        ol_ref[...] = out


def _combine(layer, wts, x1, ysh, mods, final_norm, yg):
    final = layer == DEPTH - 1
    row = lambda i: (i, 0)
    col = lambda i: (0, i)
    n_ctx = T_CTX // TC_ROWS
    if final:
        out_specs = [pl.BlockSpec((TC_ROWS, D_MODEL), lambda i: (jnp.minimum(i, n_ctx - 1), 0)),
                     pl.BlockSpec((TC_ROWS, D_MODEL), lambda i: (jnp.maximum(i - n_ctx, 0), 0))]
        out_shape = [jax.ShapeDtypeStruct((T_CTX, D_MODEL), F32), jax.ShapeDtypeStruct((T_LAT, D_MODEL), F32)]
    else:
        out_specs = [pl.BlockSpec((TC_ROWS, D_MODEL), row)]
        out_shape = [jax.ShapeDtypeStruct((T_ALL, D_MODEL), F32)]
    return pl.pallas_call(
        functools.partial(_combine_kernel, final),
        grid=(T_ALL // TC_ROWS,),
        in_specs=[pl.BlockSpec((TOP_K, TC_ROWS), col),
                  pl.BlockSpec((TC_ROWS, D_MODEL), row),
                  pl.BlockSpec((TC_ROWS * SLAB, LANES), row),
                  _mod_spec(layer, TR // TC_ROWS),
                  pl.BlockSpec((1, D_MODEL), lambda i: (0, 0)),
                  pl.BlockSpec((TOP_K, TC_ROWS * SLAB, LANES), lambda i: (0, i, 0))],
        out_specs=out_specs,
        out_shape=out_shape,
        scratch_shapes=[pltpu.VMEM((TC_ROWS, D_MODEL), F32),
                        pltpu.VMEM((TOP_K, TC_ROWS, LANES), F32)],
        compiler_params=_cparams(("arbitrary",)),
        name="combine",
    )(wts, x1, ysh, mods, final_norm.reshape(1, D_MODEL), yg)


def kernel(x_prompt, x_sample, cache_k, cache_v, c, c_ctx, w_mod, b_mod, g_mix, g_ffn, w_in, w_out, conv_w,
           attn_sink, w_router, router_bias, w_gate_e, w_up_e, w_down_e, w_gate_s, w_up_s, w_down_s, final_norm):
    ccsc = jnp.asarray(_channel_dft(), BF16)
    csl_ctx = jnp.asarray(_dft_tables(SEQ), BF16)
    csl_lat = jnp.asarray(_dft_tables(DEC_SEQ), BF16)
    rope = tuple(jnp.asarray(t, F32) for t in _rope_tables(DEC_SEQ))
    tri = jnp.asarray(np.triu(np.ones((TR, TR)), 1), BF16)
    shared_te = jnp.zeros((T_ALL // TM,), I32)
    shared_nt = jnp.full((1,), T_ALL // TM, I32)
    shared_end = jnp.full((1,), T_ALL, I32)

    cond = jnp.concatenate([c_ctx[None], c, jnp.zeros((N_COND - 1 - DEC_BATCH, D_MODEL), F32)], axis=0)
    mods = _mods(cond, w_mod, b_mod).reshape(DEPTH, N_COND, N_MOD, D_MODEL)
    cache_k = cache_k.reshape(DEC_BATCH, DEPTH, SEQ, KV_W)
    cache_v = cache_v.reshape(DEC_BATCH, DEPTH, SEQ, KV_W)
    w_in_bf, w_out_bf = w_in.astype(BF16), w_out.astype(BF16)
    wr_t = jnp.swapaxes(w_router, 1, 2)
    wr_hi = wr_t.astype(BF16)
    wr_split = jnp.concatenate([wr_hi, (wr_t - wr_hi.astype(F32)).astype(BF16)], axis=1)
    ws_gate, ws_up, ws_down = w_gate_s[:, None], w_up_s[:, None], w_down_s[:, None]
    x = jnp.concatenate([x_prompt.reshape(T_CTX, D_MODEL), x_sample.reshape(T_LAT, D_MODEL)], axis=0)

    new_k, new_v = [], []
    for l in range(DEPTH):
        q, kv, fm = _inproj(l, x, mods, g_mix, w_in_bf, rope)
        new_k.append(kv[:T_CTX, :KV_W].reshape(BATCH, SEQ, N_KV_HEADS, HEAD_DIM))
        new_v.append(kv[:T_CTX, KV_W:].reshape(BATCH, SEQ, N_KV_HEADS, HEAD_DIM))
        mix_ctx = _ctx_mix(l, attn_sink, q, kv, fm, ccsc, csl_ctx, conv_w)
        mix_lat = _lat_mix(l, attn_sink, q, kv, fm, cache_k, cache_v, ccsc, csl_lat, conv_w)
        x1, h2, ids, wts, rank, cnt = _outproj(l, mix_ctx, mix_lat, x, mods, g_ffn, w_out_bf, wr_split,
                                               router_bias, tri)
        pos, te, meta = _positions(ids, rank, cnt)
        pos_flat = pos.reshape(TOP_K * T_ALL)
        xs = _dispatch(pos_flat, h2.reshape(T_ALL, SLAB, LANES)).reshape(P_SORT * SLAB, LANES)
        y = _experts(l, te[0, :N_SORT_TILES], meta[2, :1], meta[0, :N_EXPERTS] + meta[1, :N_EXPERTS], xs,
                     w_gate_e, w_up_e, w_down_e)
        ysh = _experts(l, shared_te, shared_nt, shared_end, h2, ws_gate, ws_up, ws_down)
        yg = _gather(pos_flat, y.reshape(P_SORT, SLAB, LANES)).reshape(TOP_K, T_ALL * SLAB, LANES)
        outs = _combine(l, wts, x1, ysh, mods, final_norm, yg)
        x = outs[0]

    y_prompt = outs[0].reshape(BATCH, SEQ, D_MODEL)
    y_sample = outs[1].reshape(DEC_BATCH, DEC_SEQ, D_MODEL)
    return y_prompt, y_sample, jnp.stack(new_k, axis=1), jnp.stack(new_v, axis=1)
```

```python
import functools

import numpy as np
import jax
import jax.numpy as jnp
from jax import lax
from jax.experimental import pallas as pl
from jax.experimental.pallas import tpu as pltpu
from jax.experimental.pallas import tpu_sc as plsc

F32 = jnp.float32
BF16 = jnp.bfloat16
I32 = jnp.int32
U32 = jnp.uint32

D_MODEL = 2048
HALF = D_MODEL // 2
BATCH, SEQ = 32, 256
DEC_BATCH, DEC_SEQ = 4, 1024
DEPTH = 2
GRID_W = 64
HEAD_DIM = 128
N_HEADS, N_KV_HEADS, Q_PER_KV = 8, 2, 4
ATTN_W, KV_W = 1024, 256
WINDOW, BLOCK = 128, 128
SCALE = HEAD_DIM ** -0.5
ROPE_BASE = 10000.0
ROPE_PAIRS = 32
FOURIER_W, FOURIER_GROUPS, FOURIER_GROUP_W = 512, 4, 128
CONV_W = 512
IN_W = 3584
N_EXPERTS, TOP_K, N_GROUPS, EXPERTS_PER_GROUP, TOPK_GROUPS = 64, 8, 8, 8, 4
EXPERT_FF = 512
ROUTE_SCALE = 2.5
N_MOD = 6
EPS = 1e-6
NEG = -1e30

T_CTX = BATCH * SEQ
T_LAT = DEC_BATCH * DEC_SEQ
T_ALL = T_CTX + T_LAT
TR = 512
N_CTX_TILES = T_CTX // TR
LAT_TILES_PER_SEQ = DEC_SEQ // TR
N_TILES = T_ALL // TR
N_COND = 8
MOD_TN = 1024
TM = 512
TC_ROWS = 128
N_SORT_TILES = T_ALL * TOP_K // TM + N_EXPERTS
P_SORT = N_SORT_TILES * TM
META_LANES = 128
TE_LANES = 512
VMEM_LIMIT = 56 * 1024 * 1024


def _cparams(sem=None):
    return pltpu.CompilerParams(dimension_semantics=sem, vmem_limit_bytes=VMEM_LIMIT)


def _silu(x):
    return x / (1.0 + jnp.exp(-x))


def _sigmoid(x):
    return 1.0 / (1.0 + jnp.exp(-x))


def _rmsnorm(x, g):
    return x * lax.rsqrt(jnp.mean(x * x, axis=-1, keepdims=True) + EPS) * g


def _cond_row(i):
    return jnp.where(i < N_CTX_TILES, 0, 1 + (i - N_CTX_TILES) // LAT_TILES_PER_SEQ)


def _pack_halves(v):
    n = v.shape[1] // 2
    hi = pltpu.bitcast(v[:, :n].astype(BF16).astype(F32), U32)
    lo = pltpu.bitcast(v[:, n:].astype(BF16).astype(F32), U32)
    return hi | (lo >> 16)


def _unpack_halves(w):
    return pltpu.bitcast(w & jnp.uint32(0xFFFF0000), F32), pltpu.bitcast(w << 16, F32)


SLAB, LANES = 8, 128


def _load_slabs(ref, n, lead=()):
    return jnp.concatenate([ref[lead + (pl.ds(s, n, stride=SLAB), slice(None))] for s in range(SLAB)], axis=1)


def _store_slabs(ref, words):
    n = words.shape[0]
    for s in range(SLAB):
        ref[pl.ds(s, n, stride=SLAB), :] = words[:, s * LANES:(s + 1) * LANES]


def _dft_tables(L):
    j = np.arange(L, dtype=np.int64)
    ang = 2.0 * np.pi * ((j[:, None] * j[None, :]) % L).astype(np.float64) / L
    c, s = np.cos(ang) / np.sqrt(L), np.sin(ang) / np.sqrt(L)
    return np.concatenate([c, -s], axis=1)


def _channel_dft():
    n = FOURIER_GROUP_W
    j = np.arange(n, dtype=np.int64)
    ang = 2.0 * np.pi * ((j[:, None] * j[None, :]) % n).astype(np.float64) / n
    return np.concatenate([np.cos(ang), np.sin(ang)], axis=1) / np.sqrt(n)


def _rope_tables(S):
    pos = np.arange(S)
    row, col = (pos // GRID_W).astype(np.float64), (pos % GRID_W).astype(np.float64)
    inv = ROPE_BASE ** (-np.arange(ROPE_PAIRS, dtype=np.float64) / ROPE_PAIRS)
    ar, ac = row[:, None] * inv, col[:, None] * inv
    z = np.zeros_like(ar)
    cos = np.concatenate([np.cos(ar), np.cos(ar), np.cos(ac), np.cos(ac)], axis=1)
    sin_hi = np.concatenate([-np.sin(ar), z, -np.sin(ac), z], axis=1)
    sin_lo = np.concatenate([z, np.sin(ar), z, np.sin(ac)], axis=1)
    return cos, sin_hi, sin_lo


def _mods_kernel(cond_ref, w_ref, b_ref, o_ref):
    s = _silu(cond_ref[...]).astype(BF16)
    o_ref[0] = jnp.dot(s, w_ref[0].astype(BF16), preferred_element_type=F32) + b_ref[0]


def _mods(cond, w_mod, b_mod):
    n = N_MOD * D_MODEL
    return pl.pallas_call(
        _mods_kernel,
        grid=(DEPTH, n // MOD_TN),
        in_specs=[pl.BlockSpec((N_COND, D_MODEL), lambda l, j: (0, 0)),
                  pl.BlockSpec((1, D_MODEL, MOD_TN), lambda l, j: (l, 0, j)),
                  pl.BlockSpec((1, 1, MOD_TN), lambda l, j: (l, 0, j))],
        out_specs=pl.BlockSpec((1, N_COND, MOD_TN), lambda l, j: (l, 0, j)),
        out_shape=jax.ShapeDtypeStruct((DEPTH, N_COND, n), F32),
        compiler_params=_cparams(("arbitrary", "arbitrary")),
        name="mods",
    )(cond, w_mod, b_mod.reshape(DEPTH, 1, n))


def _mod_spec(layer, tiles_per_tr=1):
    return pl.BlockSpec((1, 1, N_MOD, D_MODEL), lambda i: (layer, _cond_row(i // tiles_per_tr), 0, 0))


def _inproj_kernel(x_ref, mod_ref, g_ref, w_ref, cos_ref, shi_ref, slo_ref, q_ref, kv_ref, fm_ref):
    i = pl.program_id(0)
    shift, scale = mod_ref[0, 0, 0:1, :], mod_ref[0, 0, 1:2, :]
    h = (_rmsnorm(x_ref[...], g_ref[0]) * (1.0 + scale) + shift).astype(BF16)
    fm_ref[...] = jnp.dot(h, w_ref[0, :, ATTN_W + 2 * KV_W:], preferred_element_type=F32).astype(BF16)
    qkv = jnp.dot(h, w_ref[0, :, :ATTN_W + 2 * KV_W], preferred_element_type=F32)
    kv_ref[:, KV_W:] = qkv[:, ATTN_W + KV_W:]

    @pl.when(i < N_CTX_TILES)
    def _():
        q_ref[...] = qkv[:, :ATTN_W].astype(BF16)
        kv_ref[:, :KV_W] = qkv[:, ATTN_W:ATTN_W + KV_W]

    @pl.when(i >= N_CTX_TILES)
    def _():
        cos, shi, slo = cos_ref[...], shi_ref[...], slo_ref[...]
        for hd in range(N_HEADS + N_KV_HEADS):
            xh = qkv[:, hd * HEAD_DIM:(hd + 1) * HEAD_DIM]
            r = xh * cos + pltpu.roll(xh, HEAD_DIM - 32, 1) * shi + pltpu.roll(xh, 32, 1) * slo
            if hd < N_HEADS:
                q_ref[:, hd * HEAD_DIM:(hd + 1) * HEAD_DIM] = r.astype(BF16)
            else:
                kv_ref[:, (hd - N_HEADS) * HEAD_DIM:(hd - N_HEADS + 1) * HEAD_DIM] = r


def _inproj(layer, x, mods, g_mix, w_in_bf, rope):
    lat_blk = lambda i: (jnp.maximum(i - N_CTX_TILES, 0) % LAT_TILES_PER_SEQ, 0)
    row = lambda i: (i, 0)
    return pl.pallas_call(
        _inproj_kernel,
        grid=(N_TILES,),
        in_specs=[pl.BlockSpec((TR, D_MODEL), row),
                  _mod_spec(layer),
                  pl.BlockSpec((1, 1, D_MODEL), lambda i: (layer, 0, 0)),
                  pl.BlockSpec((1, D_MODEL, IN_W), lambda i: (layer, 0, 0), pipeline_mode=pl.Buffered(1)),
                  pl.BlockSpec((TR, HEAD_DIM), lat_blk),
                  pl.BlockSpec((TR, HEAD_DIM), lat_blk),
                  pl.BlockSpec((TR, HEAD_DIM), lat_blk)],
        out_specs=[pl.BlockSpec((TR, ATTN_W), row),
                   pl.BlockSpec((TR, 2 * KV_W), row),
                   pl.BlockSpec((TR, FOURIER_W + 3 * CONV_W), row)],
        out_shape=[jax.ShapeDtypeStruct((T_ALL, ATTN_W), BF16),
                   jax.ShapeDtypeStruct((T_ALL, 2 * KV_W), F32),
                   jax.ShapeDtypeStruct((T_ALL, FOURIER_W + 3 * CONV_W), BF16)],
        compiler_params=_cparams(("arbitrary",)),
        name="inproj",
    )(x, mods, g_mix.reshape(DEPTH, 1, D_MODEL), w_in_bf, *rope)


def _fourier_conv(fm_ref, ccsc_ref, csl_ref, cw_ref, o_ref):
    L = fm_ref.shape[0]
    for g in range(FOURIER_GROUPS):
        xg = fm_ref[:, g * FOURIER_GROUP_W:(g + 1) * FOURIER_GROUP_W]
        z = jnp.dot(xg, ccsc_ref[...], preferred_element_type=F32)
        zz = jnp.concatenate([z[:, :FOURIER_GROUP_W], z[:, FOURIER_GROUP_W:]], axis=0).astype(BF16)
        f = jnp.dot(csl_ref[...], zz, preferred_element_type=F32)
        o_ref[:, ATTN_W + g * FOURIER_GROUP_W:ATTN_W + (g + 1) * FOURIER_GROUP_W] = f.astype(BF16)
    xc = fm_ref[:, FOURIER_W:FOURIER_W + CONV_W].astype(F32)
    bg = fm_ref[:, FOURIER_W + CONV_W:FOURIER_W + 2 * CONV_W].astype(F32)
    cg = fm_ref[:, FOURIER_W + 2 * CONV_W:].astype(F32)
    u = cg * xc
    rows = lax.broadcasted_iota(I32, u.shape, 0)
    prev = jnp.where(rows == 0, 0.0, pltpu.roll(u, 1, 0))
    nxt = jnp.where(rows == L - 1, 0.0, pltpu.roll(u, L - 1, 0))
    y = prev * cw_ref[0, 0:1, :] + u * cw_ref[0, 1:2, :] + nxt * cw_ref[0, 2:3, :]
    o_ref[:, ATTN_W + FOURIER_W:] = (bg * y).astype(BF16)


def _ctx_mix_kernel(layer, sink_ref, q_ref, kv_ref, fm_ref, ccsc_ref, csl_ref, cw_ref, o_ref):
    for h in range(N_HEADS):
        g = h // Q_PER_KV
        q = q_ref[:, h * HEAD_DIM:(h + 1) * HEAD_DIM]
        k = kv_ref[:, g * HEAD_DIM:(g + 1) * HEAD_DIM].astype(BF16)
        v = kv_ref[:, KV_W + g * HEAD_DIM:KV_W + (g + 1) * HEAD_DIM].astype(BF16)
        s = lax.dot_general(q, k, (((1,), (1,)), ((), ())), preferred_element_type=F32) * SCALE
        sink = sink_ref[layer, h]
        m = jnp.maximum(jnp.max(s, axis=-1, keepdims=True), sink)
        e = jnp.exp(s - m)
        den = jnp.sum(e, axis=-1, keepdims=True) + jnp.exp(sink - m)
        o = jnp.dot(e.astype(BF16), v, preferred_element_type=F32) / den
        o_ref[:, h * HEAD_DIM:(h + 1) * HEAD_DIM] = o.astype(BF16)
    _fourier_conv(fm_ref, ccsc_ref, csl_ref, cw_ref, o_ref)


def _ctx_mix(layer, sink, q, kv, fm, ccsc, csl, conv_w):
    row = lambda b: (b, 0)
    const = lambda b: (0, 0)
    return pl.pallas_call(
        functools.partial(_ctx_mix_kernel, layer),
        grid=(BATCH,),
        in_specs=[pl.BlockSpec(memory_space=pltpu.SMEM),
                  pl.BlockSpec((SEQ, ATTN_W), row),
                  pl.BlockSpec((SEQ, 2 * KV_W), row),
                  pl.BlockSpec((SEQ, FOURIER_W + 3 * CONV_W), row),
                  pl.BlockSpec((FOURIER_GROUP_W, 2 * FOURIER_GROUP_W), const),
                  pl.BlockSpec((SEQ, 2 * SEQ), const),
                  pl.BlockSpec((1, 3, CONV_W), lambda b: (layer, 0, 0))],
        out_specs=pl.BlockSpec((SEQ, D_MODEL), row),
        out_shape=jax.ShapeDtypeStruct((T_CTX, D_MODEL), BF16),
        compiler_params=_cparams(("arbitrary",)),
        name="ctx_mix",
    )(sink, q, kv, fm, ccsc, csl, conv_w)


def _lat_mix_kernel(layer, sink_ref, q_ref, kv_ref, fm_ref, ck_ref, cv_ref, ccsc_ref, csl_ref, cw_ref, o_ref):
    n = pl.program_id(1)
    nb = DEC_SEQ // BLOCK
    rows = Q_PER_KV * BLOCK
    band = 3 * BLOCK

    @pl.when(n == 0)
    def _():
        _fourier_conv(fm_ref, ccsc_ref, csl_ref, cw_ref, o_ref)

    q0 = pl.multiple_of(n * BLOCK, BLOCK)
    k0 = pl.multiple_of(jnp.clip(n - 1, 0, nb - 3) * BLOCK, BLOCK)
    qpos = q0 + lax.broadcasted_iota(I32, (rows, band), 0) % BLOCK
    kpos = k0 + lax.broadcasted_iota(I32, (rows, band), 1)
    valid = jnp.abs(qpos - kpos) <= WINDOW
    for g in range(N_KV_HEADS):
        kb = kv_ref[pl.ds(k0, band), g * HEAD_DIM:(g + 1) * HEAD_DIM].astype(BF16)
        vb = kv_ref[pl.ds(k0, band), KV_W + g * HEAD_DIM:KV_W + (g + 1) * HEAD_DIM].astype(BF16)
        k_ctx = ck_ref[0, 0, :, g * HEAD_DIM:(g + 1) * HEAD_DIM].astype(BF16)
        v_ctx = cv_ref[0, 0, :, g * HEAD_DIM:(g + 1) * HEAD_DIM].astype(BF16)
        sink = jnp.concatenate(
            [jnp.full((BLOCK, 1), sink_ref[layer, g * Q_PER_KV + j], F32) for j in range(Q_PER_KV)], axis=0)
        q = jnp.concatenate(
            [q_ref[pl.ds(q0, BLOCK), (g * Q_PER_KV + j) * HEAD_DIM:(g * Q_PER_KV + j + 1) * HEAD_DIM]
             for j in range(Q_PER_KV)], axis=0)
        s_loc = lax.dot_general(q, kb, (((1,), (1,)), ((), ())), preferred_element_type=F32) * SCALE
        s_loc = jnp.where(valid, s_loc, NEG)
        s_ctx = lax.dot_general(q, k_ctx, (((1,), (1,)), ((), ())), preferred_element_type=F32) * SCALE
        m = jnp.maximum(jnp.maximum(jnp.max(s_loc, axis=-1, keepdims=True),
                                    jnp.max(s_ctx, axis=-1, keepdims=True)), sink)
        e_loc, e_ctx = jnp.exp(s_loc - m), jnp.exp(s_ctx - m)
        den = (jnp.sum(e_loc, axis=-1, keepdims=True) + jnp.sum(e_ctx, axis=-1, keepdims=True)
               + jnp.exp(sink - m))
        o = (jnp.dot(e_loc.astype(BF16), vb, preferred_element_type=F32)
             + jnp.dot(e_ctx.astype(BF16), v_ctx, preferred_element_type=F32)) / den
        for j in range(Q_PER_KV):
            h = g * Q_PER_KV + j
            o_ref[pl.ds(q0, BLOCK), h * HEAD_DIM:(h + 1) * HEAD_DIM] = o[j * BLOCK:(j + 1) * BLOCK].astype(BF16)


def _lat_mix(layer, sink, q, kv, fm, cache_k, cache_v, ccsc, csl, conv_w):
    off = T_CTX // DEC_SEQ
    row = lambda b, n: (off + b, 0)
    const = lambda b, n: (0, 0)
    cache = lambda b, n: (b, layer, 0, 0)
    return pl.pallas_call(
        functools.partial(_lat_mix_kernel, layer),
        grid=(DEC_BATCH, DEC_SEQ // BLOCK),
        in_specs=[pl.BlockSpec(memory_space=pltpu.SMEM),
                  pl.BlockSpec((DEC_SEQ, ATTN_W), row),
                  pl.BlockSpec((DEC_SEQ, 2 * KV_W), row),
                  pl.BlockSpec((DEC_SEQ, FOURIER_W + 3 * CONV_W), row),
                  pl.BlockSpec((1, 1, SEQ, KV_W), cache),
                  pl.BlockSpec((1, 1, SEQ, KV_W), cache),
                  pl.BlockSpec((FOURIER_GROUP_W, 2 * FOURIER_GROUP_W), const),
                  pl.BlockSpec((DEC_SEQ, 2 * DEC_SEQ), const, pipeline_mode=pl.Buffered(1)),
                  pl.BlockSpec((1, 3, CONV_W), lambda b, n: (layer, 0, 0))],
        out_specs=pl.BlockSpec((DEC_SEQ, D_MODEL), lambda b, n: (b, 0)),
        out_shape=jax.ShapeDtypeStruct((T_LAT, D_MODEL), BF16),
        compiler_params=_cparams(("arbitrary", "arbitrary")),
        name="lat_mix",
    )(sink, q, kv, fm, cache_k, cache_v, ccsc, csl, conv_w)


def _first_index_of_max(vals, iota, sentinel):
    mx = jnp.max(vals, axis=0, keepdims=True)
    idx = jnp.min(jnp.where(vals == mx, iota, sentinel), axis=0, keepdims=True)
    return mx, idx


def _outproj_kernel(mixc_ref, mixl_ref, x_ref, mod_ref, g_ref, w_ref, wr_ref, rb_ref, tri_ref,
                    x1_ref, h2_ref, ids_ref, wts_ref, rank_ref, cnt_ref, run_ref):
    i = pl.program_id(0)

    @pl.when(i == 0)
    def _():
        run_ref[...] = jnp.zeros_like(run_ref)

    gate1, shift2, scale2 = mod_ref[0, 0, 2:3, :], mod_ref[0, 0, 3:4, :], mod_ref[0, 0, 4:5, :]
    mix = jnp.where(i < N_CTX_TILES, mixc_ref[...], mixl_ref[...])
    x1 = x_ref[...] + gate1 * jnp.dot(mix, w_ref[0], preferred_element_type=F32)
    x1_ref[...] = x1
    h2 = _rmsnorm(x1, g_ref[0]) * (1.0 + scale2) + shift2
    _store_slabs(h2_ref, _pack_halves(h2))

    nt = (((1,), (1,)), ((), ()))
    h_hi = h2.astype(BF16)
    h_lo = (h2 - h_hi.astype(F32)).astype(BF16)
    part = lax.dot_general(wr_ref[0], h_hi, nt, preferred_element_type=F32)
    logits = (part[:N_EXPERTS] + part[N_EXPERTS:]
              + lax.dot_general(wr_ref[0, :N_EXPERTS, :], h_lo, nt, preferred_element_type=F32))
    scores = _sigmoid(logits)
    biased = scores + rb_ref[0]
    ninf = -jnp.inf
    mem = lax.broadcasted_iota(I32, (EXPERTS_PER_GROUP, TR), 0)
    grp_rows = []
    for g in range(N_GROUPS):
        bgp = biased[g * EXPERTS_PER_GROUP:(g + 1) * EXPERTS_PER_GROUP]
        m1, i1 = _first_index_of_max(bgp, mem, EXPERTS_PER_GROUP)
        m2 = jnp.max(jnp.where(mem == i1, ninf, bgp), axis=0, keepdims=True)
        grp_rows.append(m1 + m2)
    cur = jnp.concatenate(grp_rows, axis=0)
    gio = lax.broadcasted_iota(I32, (N_GROUPS, TR), 0)
    gsel = jnp.zeros((N_GROUPS, TR), F32)
    for _ in range(TOPK_GROUPS):
        _, gi = _first_index_of_max(cur, gio, N_GROUPS)
        hit = gio == gi
        gsel = jnp.where(hit, 1.0, gsel)
        cur = jnp.where(hit, ninf, cur)
    cur = jnp.concatenate(
        [jnp.where(gsel[g:g + 1] > 0.0, biased[g * EXPERTS_PER_GROUP:(g + 1) * EXPERTS_PER_GROUP], ninf)
         for g in range(N_GROUPS)], axis=0)
    eio = lax.broadcasted_iota(I32, (N_EXPERTS, TR), 0)
    chosen = jnp.zeros((N_EXPERTS, TR), F32)
    ids, sel = [], []
    for _ in range(TOP_K):
        _, ei = _first_index_of_max(cur, eio, N_EXPERTS)
        hit = eio == ei
        ids.append(ei)
        sel.append(jnp.sum(jnp.where(hit, scores, 0.0), axis=0, keepdims=True))
        chosen = jnp.where(hit, 1.0, chosen)
        cur = jnp.where(hit, ninf, cur)
    total = functools.reduce(lambda a, b: a + b, sel)
    ids_ref[...] = jnp.concatenate(ids, axis=0)
    wts_ref[...] = jnp.concatenate([s / total * ROUTE_SCALE for s in sel], axis=0)

    before = jnp.dot(chosen.astype(BF16), tri_ref[...], preferred_element_type=F32) + run_ref[:, 0:1]
    rank_ref[...] = jnp.concatenate(
        [jnp.sum(jnp.where(eio == ei, before, 0.0), axis=0, keepdims=True) for ei in ids], axis=0).astype(I32)
    run = run_ref[...] + jnp.sum(chosen, axis=1, keepdims=True)
    run_ref[...] = run
    cnt_ref[...] = run


def _outproj(layer, mix_ctx, mix_lat, x, mods, g_ffn, w_out_bf, wr_split, router_bias, tri):
    row = lambda i: (i, 0)
    col = lambda i: (0, i)
    const = lambda i: (0, 0)
    lay = lambda i: (layer, 0, 0)
    return pl.pallas_call(
        _outproj_kernel,
        grid=(N_TILES,),
        in_specs=[pl.BlockSpec((TR, D_MODEL), lambda i: (jnp.minimum(i, N_CTX_TILES - 1), 0)),
                  pl.BlockSpec((TR, D_MODEL), lambda i: (jnp.maximum(i - N_CTX_TILES, 0), 0)),
                  pl.BlockSpec((TR, D_MODEL), row),
                  _mod_spec(layer),
                  pl.BlockSpec((1, 1, D_MODEL), lay),
                  pl.BlockSpec((1, D_MODEL, D_MODEL), lay, pipeline_mode=pl.Buffered(1)),
                  pl.BlockSpec((1, 2 * N_EXPERTS, D_MODEL), lay),
                  pl.BlockSpec((1, N_EXPERTS, 1), lay),
                  pl.BlockSpec((TR, TR), const)],
        out_specs=[pl.BlockSpec((TR, D_MODEL), row),
                   pl.BlockSpec((TR * SLAB, LANES), row),
                   pl.BlockSpec((TOP_K, TR), col),
                   pl.BlockSpec((TOP_K, TR), col),
                   pl.BlockSpec((TOP_K, TR), col),
                   pl.BlockSpec((N_EXPERTS, META_LANES), const)],
        out_shape=[jax.ShapeDtypeStruct((T_ALL, D_MODEL), F32),
                   jax.ShapeDtypeStruct((T_ALL * SLAB, LANES), U32),
                   jax.ShapeDtypeStruct((TOP_K, T_ALL), I32),
                   jax.ShapeDtypeStruct((TOP_K, T_ALL), F32),
                   jax.ShapeDtypeStruct((TOP_K, T_ALL), I32),
                   jax.ShapeDtypeStruct((N_EXPERTS, META_LANES), F32)],
        scratch_shapes=[pltpu.VMEM((N_EXPERTS, META_LANES), F32)],
        compiler_params=_cparams(("arbitrary",)),
        name="outproj",
    )(mix_ctx, mix_lat, x, mods, g_ffn.reshape(DEPTH, 1, D_MODEL), w_out_bf, wr_split,
      router_bias.reshape(DEPTH, N_EXPERTS, 1), tri)


def _positions_kernel(ids_ref, rank_ref, cnt_ref, pos_ref, te_ref, meta_ref):
    ids = ids_ref[...]
    pos = rank_ref[...]
    tile_io = lax.broadcasted_iota(I32, (1, TE_LANES), 1)
    lane_io = lax.broadcasted_iota(I32, (1, META_LANES), 1)
    te = jnp.zeros((1, TE_LANES), I32)
    starts = jnp.zeros((1, META_LANES), I32)
    counts = jnp.zeros((1, META_LANES), I32)
    start = jnp.zeros((1, 1), I32)
    for e in range(N_EXPERTS):
        c = cnt_ref[e:e + 1, 0:1].astype(I32)
        pos = pos + jnp.where(ids == e, start, 0)
        starts = starts + jnp.where(lane_io == e, start, 0)
        counts = counts + jnp.where(lane_io == e, c, 0)
        start = start + ((c + (TM - 1)) // TM) * TM
        te = te + jnp.where(tile_io * TM >= start, 1, 0)
    pos_ref[...] = pos
    te_ref[...] = jnp.minimum(te, N_EXPERTS - 1)
    meta_ref[0:1, :] = starts
    meta_ref[1:2, :] = counts
    meta_ref[2:3, :] = jnp.broadcast_to(start // TM, (1, META_LANES))
    meta_ref[3:8, :] = jnp.zeros((5, META_LANES), I32)


def _positions(ids, rank, cnt):
    return pl.pallas_call(
        _positions_kernel,
        out_shape=[jax.ShapeDtypeStruct((TOP_K, T_ALL), I32),
                   jax.ShapeDtypeStruct((1, TE_LANES), I32),
                   jax.ShapeDtypeStruct((8, META_LANES), I32)],
        compiler_params=_cparams(),
        name="positions",
    )(ids, rank, cnt)


SC_CORES, SC_SUBCORES = 2, 16
SC_WORKERS = SC_CORES * SC_SUBCORES
SC_CHUNK = 32


def _sc_kernel(body, out_type, n_idx_rows, name):
    return pl.kernel(
        body,
        out_type=out_type,
        mesh=plsc.VectorSubcoreMesh(core_axis_name="c", subcore_axis_name="s",
                                    num_cores=SC_CORES, num_subcores=SC_SUBCORES),
        scratch_types=[pltpu.VMEM((n_idx_rows, SC_CHUNK), I32),
                       pltpu.VMEM((2, SC_CHUNK, SLAB, LANES), U32),
                       pltpu.SemaphoreType.DMA,
                       pltpu.SemaphoreType.DMA((2,)),
                       pltpu.SemaphoreType.DMA((2,))],
        compiler_params=pltpu.CompilerParams(use_tc_tiling_on_sc=True),
        name=name,
    )


def _sc_worker():
    return lax.axis_index("s") * SC_CORES + lax.axis_index("c")


def _sc_load_indices(n_rows, src_of_row, idx_v, sem):
    @pl.loop(0, n_rows)
    def _(r):
        src, dst = src_of_row(r), idx_v.at[r]
        pltpu.make_async_copy(src, dst, sem).start()

    @pl.loop(0, n_rows)
    def _(r):
        src, dst = src_of_row(r), idx_v.at[r]
        pltpu.make_async_copy(src, dst, sem).wait()


def _sc_two_slot_pipeline(n_chunks, fill, fill_wait, drain, drain_wait):
    assert n_chunks % 2 == 0 and n_chunks >= 4
    fill(0, 0)

    @pl.loop(0, n_chunks // 2)
    def _(pair):
        c = 2 * pair

        @pl.when(pair >= 1)
        def _():
            drain_wait(c - 1, 1)

        fill(c + 1, 1)
        fill_wait(c, 0)
        drain(c, 0)

        @pl.when(pair < n_chunks // 2 - 1)
        def _():
            drain_wait(c, 0)
            fill(c + 2, 0)

        fill_wait(c + 1, 1)
        drain(c + 1, 1)

    drain_wait(n_chunks - 2, 0)
    drain_wait(n_chunks - 1, 1)


def _dispatch(pos_flat, h3):
    n_chunks = T_ALL // SC_WORKERS // SC_CHUNK

    def body(pos_hbm, h_hbm, xs_hbm, idx_v, rows_v, idx_sem, fill_sem, drain_sem):
        t_base = _sc_worker() * (n_chunks * SC_CHUNK)

        def index_src(r):
            t0 = t_base + (r // TOP_K) * SC_CHUNK
            return pos_hbm.at[pl.ds(pl.multiple_of((r % TOP_K) * T_ALL + t0, SC_CHUNK), SC_CHUNK)]

        _sc_load_indices(n_chunks * TOP_K, index_src, idx_v, idx_sem)

        def load(c, slot):
            t0 = pl.multiple_of(t_base + c * SC_CHUNK, SC_CHUNK)
            return pltpu.make_async_copy(h_hbm.at[pl.ds(t0, SC_CHUNK)], rows_v.at[slot], fill_sem.at[slot])

        def scatter(c, slot, k):
            return pltpu.make_async_copy(rows_v.at[slot], xs_hbm.at[idx_v.at[c * TOP_K + k]], drain_sem.at[slot])

        def drain(c, slot):
            for k in range(TOP_K):
                scatter(c, slot, k).start()

        def drain_wait(c, slot):
            for k in range(TOP_K):
                scatter(c, slot, k).wait()

        _sc_two_slot_pipeline(n_chunks, lambda c, s: load(c, s).start(), lambda c, s: load(c, s).wait(),
                              drain, drain_wait)

    return _sc_kernel(body, jax.ShapeDtypeStruct((P_SORT, SLAB, LANES), U32), n_chunks * TOP_K,
                      "sc_dispatch")(pos_flat, h3)


def _gather(pos_flat, y3):
    n_items = TOP_K * T_ALL
    n_chunks = n_items // SC_WORKERS // SC_CHUNK

    def body(pos_hbm, y_hbm, out_hbm, idx_v, rows_v, idx_sem, fill_sem, drain_sem):
        base = _sc_worker() * (n_chunks * SC_CHUNK)

        def index_src(r):
            return pos_hbm.at[pl.ds(pl.multiple_of(base + r * SC_CHUNK, SC_CHUNK), SC_CHUNK)]

        _sc_load_indices(n_chunks, index_src, idx_v, idx_sem)

        def gather(c, slot):
            return pltpu.make_async_copy(y_hbm.at[idx_v.at[c]], rows_v.at[slot], fill_sem.at[slot])

        def store(c, slot):
            i0 = pl.multiple_of(base + c * SC_CHUNK, SC_CHUNK)
            return pltpu.make_async_copy(rows_v.at[slot], out_hbm.at[pl.ds(i0, SC_CHUNK)], drain_sem.at[slot])

        _sc_two_slot_pipeline(n_chunks, lambda c, s: gather(c, s).start(), lambda c, s: gather(c, s).wait(),
                              lambda c, s: store(c, s).start(), lambda c, s: store(c, s).wait())

    return _sc_kernel(body, jax.ShapeDtypeStruct((n_items, SLAB, LANES), U32), n_chunks,
                      "sc_gather")(pos_flat, y3)


def _experts_kernel(te_ref, nt_ref, end_ref, x_ref, wg_ref, wu_ref, wd_ref, y_ref, wg_bf, wu_bf, wd_bf):
    j = pl.program_id(0)
    used = j < nt_ref[0]
    prev = te_ref[jnp.maximum(j - 1, 0)]

    @pl.when(used & ((j == 0) | (te_ref[j] != prev)))
    def _():
        wg_bf[...] = wg_ref[0, 0].astype(BF16)
        wu_bf[...] = wu_ref[0, 0].astype(BF16)
        wd_bf[...] = wd_ref[0, 0].astype(BF16)

    @pl.when(used)
    def _():
        valid = end_ref[te_ref[j]] - j * TM
        rows = lax.broadcasted_iota(I32, (TM, HALF), 0)
        words = jnp.where(rows < valid, _load_slabs(x_ref, TM), jnp.uint32(0))
        xa, xb = _unpack_halves(words)
        xa, xb = xa.astype(BF16), xb.astype(BF16)
        g = (jnp.dot(xa, wg_bf[:HALF], preferred_element_type=F32)
             + jnp.dot(xb, wg_bf[HALF:], preferred_element_type=F32))
        u = (jnp.dot(xa, wu_bf[:HALF], preferred_element_type=F32)
             + jnp.dot(xb, wu_bf[HALF:], preferred_element_type=F32))
        a = (_silu(g) * u).astype(BF16)
        _store_slabs(y_ref, _pack_halves(jnp.dot(a, wd_bf[...], preferred_element_type=F32)))


def _experts(layer, te, nt, end, xs, w_gate, w_up, w_down):
    n_tiles = xs.shape[0] // (TM * SLAB)
    tile = lambda j, te, nt, end: (jnp.minimum(j, nt[0] - 1), 0)
    wsel = lambda j, te, nt, end: (layer, te[jnp.minimum(j, nt[0] - 1)], 0, 0)
    return pl.pallas_call(
        _experts_kernel,
        grid_spec=pltpu.PrefetchScalarGridSpec(
            num_scalar_prefetch=3,
            grid=(n_tiles,),
            in_specs=[pl.BlockSpec((TM * SLAB, LANES), tile),
                      pl.BlockSpec((1, 1, D_MODEL, EXPERT_FF), wsel),
                      pl.BlockSpec((1, 1, D_MODEL, EXPERT_FF), wsel),
                      pl.BlockSpec((1, 1, EXPERT_FF, D_MODEL), wsel)],
            out_specs=pl.BlockSpec((TM * SLAB, LANES), tile),
            scratch_shapes=[pltpu.VMEM((D_MODEL, EXPERT_FF), BF16),
                            pltpu.VMEM((D_MODEL, EXPERT_FF), BF16),
                            pltpu.VMEM((EXPERT_FF, D_MODEL), BF16)]),
        out_shape=jax.ShapeDtypeStruct(xs.shape, U32),
        compiler_params=_cparams(("arbitrary",)),
        name="experts",
    )(te, nt, end, xs, w_gate, w_up, w_down)


def _combine_kernel(final, wts_ref, x1_ref, ysh_ref, mod_ref, g_ref, buf_ref, *rest):
    *out_refs, x2_ref, wb_ref = rest
    i = pl.program_id(0)
    w = jnp.transpose(wts_ref[...])
    for k in range(TOP_K):
        wb_ref[k] = jnp.broadcast_to(w[:, k:k + 1], (TC_ROWS, LANES))
    x2_dst = x2_ref if final else out_refs[0]
    sq = jnp.zeros((TC_ROWS, LANES), F32)
    for s in range(SLAB):
        acc_a, acc_b = _unpack_halves(ysh_ref[pl.ds(s, TC_ROWS, stride=SLAB), :])
        for k in range(TOP_K):
            ya, yb = _unpack_halves(buf_ref[k, pl.ds(s, TC_ROWS, stride=SLAB), :])
            acc_a = acc_a + wb_ref[k] * ya
            acc_b = acc_b + wb_ref[k] * yb
        for half, acc in ((0, acc_a), (1, acc_b)):
            cols = slice(half * HALF + s * LANES, half * HALF + (s + 1) * LANES)
            x2 = x1_ref[:, cols] + mod_ref[0, 0, 5:6, cols] * acc
            x2_dst[:, cols] = x2
            if final:
                sq = sq + x2 * x2
    if not final:
        return
    ms = jnp.sum(sq, axis=-1, keepdims=True) * (1.0 / D_MODEL)
    out = x2_ref[...] * lax.rsqrt(ms + EPS) * g_ref[...]
    oc_ref, ol_ref = out_refs

    @pl.when(i < T_CTX // TC_ROWS)
    def _():
        oc_ref[...] = out

    @pl.when(i >= T_CTX // TC_ROWS)
    def _():
        ol_ref[...] = out


def _combine(layer, wts, x1, ysh, mods, final_norm, yg):
    final = layer == DEPTH - 1
    row = lambda i: (i, 0)
    col = lambda i: (0, i)
    n_ctx = T_CTX // TC_ROWS
    if final:
        out_specs = [pl.BlockSpec((TC_ROWS, D_MODEL), lambda i: (jnp.minimum(i, n_ctx - 1), 0)),
                     pl.BlockSpec((TC_ROWS, D_MODEL), lambda i: (jnp.maximum(i - n_ctx, 0), 0))]
        out_shape = [jax.ShapeDtypeStruct((T_CTX, D_MODEL), F32), jax.ShapeDtypeStruct((T_LAT, D_MODEL), F32)]
    else:
        out_specs = [pl.BlockSpec((TC_ROWS, D_MODEL), row)]
        out_shape = [jax.ShapeDtypeStruct((T_ALL, D_MODEL), F32)]
    return pl.pallas_call(
        functools.partial(_combine_kernel, final),
        grid=(T_ALL // TC_ROWS,),
        in_specs=[pl.BlockSpec((TOP_K, TC_ROWS), col),
                  pl.BlockSpec((TC_ROWS, D_MODEL), row),
                  pl.BlockSpec((TC_ROWS * SLAB, LANES), row),
                  _mod_spec(layer, TR // TC_ROWS),
                  pl.BlockSpec((1, D_MODEL), lambda i: (0, 0)),
                  pl.BlockSpec((TOP_K, TC_ROWS * SLAB, LANES), lambda i: (0, i, 0))],
        out_specs=out_specs,
        out_shape=out_shape,
        scratch_shapes=[pltpu.VMEM((TC_ROWS, D_MODEL), F32),
                        pltpu.VMEM((TOP_K, TC_ROWS, LANES), F32)],
        compiler_params=_cparams(("arbitrary",)),
        name="combine",
    )(wts, x1, ysh, mods, final_norm.reshape(1, D_MODEL), yg)


def kernel(x_prompt, x_sample, cache_k, cache_v, c, c_ctx, w_mod, b_mod, g_mix, g_ffn, w_in, w_out, conv_w,
           attn_sink, w_router, router_bias, w_gate_e, w_up_e, w_down_e, w_gate_s, w_up_s, w_down_s, final_norm):
    ccsc = jnp.asarray(_channel_dft(), BF16)
    csl_ctx = jnp.asarray(_dft_tables(SEQ), BF16)
    csl_lat = jnp.asarray(_dft_tables(DEC_SEQ), BF16)
    rope = tuple(jnp.asarray(t, F32) for t in _rope_tables(DEC_SEQ))
    tri = jnp.asarray(np.triu(np.ones((TR, TR)), 1), BF16)
    shared_te = jnp.zeros((T_ALL // TM,), I32)
    shared_nt = jnp.full((1,), T_ALL // TM, I32)
    shared_end = jnp.full((1,), T_ALL, I32)

    cond = jnp.concatenate([c_ctx[None], c, jnp.zeros((N_COND - 1 - DEC_BATCH, D_MODEL), F32)], axis=0)
    mods = _mods(cond, w_mod, b_mod).reshape(DEPTH, N_COND, N_MOD, D_MODEL)
    cache_k = cache_k.reshape(DEC_BATCH, DEPTH, SEQ, KV_W)
    cache_v = cache_v.reshape(DEC_BATCH, DEPTH, SEQ, KV_W)
    w_in_bf, w_out_bf = w_in.astype(BF16), w_out.astype(BF16)
    wr_t = jnp.swapaxes(w_router, 1, 2)
    wr_hi = wr_t.astype(BF16)
    wr_split = jnp.concatenate([wr_hi, (wr_t - wr_hi.astype(F32)).astype(BF16)], axis=1)
    ws_gate, ws_up, ws_down = w_gate_s[:, None], w_up_s[:, None], w_down_s[:, None]
    x = jnp.concatenate([x_prompt.reshape(T_CTX, D_MODEL), x_sample.reshape(T_LAT, D_MODEL)], axis=0)

    new_k, new_v = [], []
    for l in range(DEPTH):
        q, kv, fm = _inproj(l, x, mods, g_mix, w_in_bf, rope)
        new_k.append(kv[:T_CTX, :KV_W].reshape(BATCH, SEQ, N_KV_HEADS, HEAD_DIM))
        new_v.append(kv[:T_CTX, KV_W:].reshape(BATCH, SEQ, N_KV_HEADS, HEAD_DIM))
        mix_ctx = _ctx_mix(l, attn_sink, q, kv, fm, ccsc, csl_ctx, conv_w)
        mix_lat = _lat_mix(l, attn_sink, q, kv, fm, cache_k, cache_v, ccsc, csl_lat, conv_w)
        x1, h2, ids, wts, rank, cnt = _outproj(l, mix_ctx, mix_lat, x, mods, g_ffn, w_out_bf, wr_split,
                                               router_bias, tri)
        pos, te, meta = _positions(ids, rank, cnt)
        pos_flat = pos.reshape(TOP_K * T_ALL)
        xs = _dispatch(pos_flat, h2.reshape(T_ALL, SLAB, LANES)).reshape(P_SORT * SLAB, LANES)
        y = _experts(l, te[0, :N_SORT_TILES], meta[2, :1], meta[0, :N_EXPERTS] + meta[1, :N_EXPERTS], xs,
                     w_gate_e, w_up_e, w_down_e)
        ysh = _experts(l, shared_te, shared_nt, shared_end, h2, ws_gate, ws_up, ws_down)
        yg = _gather(pos_flat, y.reshape(P_SORT, SLAB, LANES)).reshape(TOP_K, T_ALL * SLAB, LANES)
        outs = _combine(l, wts, x1, ysh, mods, final_norm, yg)
        x = outs[0]

    y_prompt = outs[0].reshape(BATCH, SEQ, D_MODEL)
    y_sample = outs[1].reshape(DEC_BATCH, DEC_SEQ, D_MODEL)
    return y_prompt, y_sample, jnp.stack(new_k, axis=1), jnp.stack(new_v, axis=1)
```

```python
import functools

import numpy as np
import jax
import jax.numpy as jnp
from jax import lax
from jax.experimental import pallas as pl
from jax.experimental.pallas import tpu as pltpu
from jax.experimental.pallas import tpu_sc as plsc

F32 = jnp.float32
BF16 = jnp.bfloat16
I32 = jnp.int32
U32 = jnp.uint32

D_MODEL = 2048
HALF = D_MODEL // 2
BATCH, SEQ = 32, 256
DEC_BATCH, DEC_SEQ = 4, 1024
DEPTH = 2
GRID_W = 64
HEAD_DIM = 128
N_HEADS, N_KV_HEADS, Q_PER_KV = 8, 2, 4
ATTN_W, KV_W = 1024, 256
WINDOW, BLOCK = 128, 128
SCALE = HEAD_DIM ** -0.5
ROPE_BASE = 10000.0
ROPE_PAIRS = 32
FOURIER_W, FOURIER_GROUPS, FOURIER_GROUP_W = 512, 4, 128
CONV_W = 512
IN_W = 3584
N_EXPERTS, TOP_K, N_GROUPS, EXPERTS_PER_GROUP, TOPK_GROUPS = 64, 8, 8, 8, 4
EXPERT_FF = 512
ROUTE_SCALE = 2.5
N_MOD = 6
EPS = 1e-6
NEG = -1e30

T_CTX = BATCH * SEQ
T_LAT = DEC_BATCH * DEC_SEQ
T_ALL = T_CTX + T_LAT
TR = 512
N_CTX_TILES = T_CTX // TR
LAT_TILES_PER_SEQ = DEC_SEQ // TR
N_TILES = T_ALL // TR
N_COND = 8
MOD_TN = 1024
TM = 512
TC_ROWS = 128
N_SORT_TILES = T_ALL * TOP_K // TM + N_EXPERTS
P_SORT = N_SORT_TILES * TM
META_LANES = 128
TE_LANES = 512
VMEM_LIMIT = 56 * 1024 * 1024


def _cparams(sem=None):
    return pltpu.CompilerParams(dimension_semantics=sem, vmem_limit_bytes=VMEM_LIMIT)


def _silu(x):
    return x / (1.0 + jnp.exp(-x))


def _sigmoid(x):
    return 1.0 / (1.0 + jnp.exp(-x))


def _rmsnorm(x, g):
    return x * lax.rsqrt(jnp.mean(x * x, axis=-1, keepdims=True) + EPS) * g


def _cond_row(i):
    return jnp.where(i < N_CTX_TILES, 0, 1 + (i - N_CTX_TILES) // LAT_TILES_PER_SEQ)


def _pack_halves(v):
    n = v.shape[1] // 2
    hi = pltpu.bitcast(v[:, :n].astype(BF16).astype(F32), U32)
    lo = pltpu.bitcast(v[:, n:].astype(BF16).astype(F32), U32)
    return hi | (lo >> 16)


def _unpack_halves(w):
    return pltpu.bitcast(w & jnp.uint32(0xFFFF0000), F32), pltpu.bitcast(w << 16, F32)


SLAB, LANES = 8, 128


def _load_slabs(ref, n, row0=0):
    return jnp.concatenate([ref[pl.ds(row0 * SLAB + s, n, stride=SLAB), :] for s in range(SLAB)], axis=1)


def _store_slabs(ref, words, row0=0):
    n = words.shape[0]
    for s in range(SLAB):
        ref[pl.ds(row0 * SLAB + s, n, stride=SLAB), :] = words[:, s * LANES:(s + 1) * LANES]


def _dft_tables(L):
    j = np.arange(L, dtype=np.int64)
    ang = 2.0 * np.pi * ((j[:, None] * j[None, :]) % L).astype(np.float64) / L
    c, s = np.cos(ang) / np.sqrt(L), np.sin(ang) / np.sqrt(L)
    return np.concatenate([c, -s], axis=1)


def _channel_dft():
    n = FOURIER_GROUP_W
    j = np.arange(n, dtype=np.int64)
    ang = 2.0 * np.pi * ((j[:, None] * j[None, :]) % n).astype(np.float64) / n
    return np.concatenate([np.cos(ang), np.sin(ang)], axis=1) / np.sqrt(n)


def _rope_tables(S):
    pos = np.arange(S)
    row, col = (pos // GRID_W).astype(np.float64), (pos % GRID_W).astype(np.float64)
    inv = ROPE_BASE ** (-np.arange(ROPE_PAIRS, dtype=np.float64) / ROPE_PAIRS)
    ar, ac = row[:, None] * inv, col[:, None] * inv
    z = np.zeros_like(ar)
    cos = np.concatenate([np.cos(ar), np.cos(ar), np.cos(ac), np.cos(ac)], axis=1)
    sin_hi = np.concatenate([-np.sin(ar), z, -np.sin(ac), z], axis=1)
    sin_lo = np.concatenate([z, np.sin(ar), z, np.sin(ac)], axis=1)
    return cos, sin_hi, sin_lo


def _mods_kernel(cond_ref, w_ref, b_ref, o_ref):
    s = _silu(cond_ref[...]).astype(BF16)
    o_ref[0] = jnp.dot(s, w_ref[0].astype(BF16), preferred_element_type=F32) + b_ref[0]


def _mods(cond, w_mod, b_mod):
    n = N_MOD * D_MODEL
    return pl.pallas_call(
        _mods_kernel,
        grid=(DEPTH, n // MOD_TN),
        in_specs=[pl.BlockSpec((N_COND, D_MODEL), lambda l, j: (0, 0)),
                  pl.BlockSpec((1, D_MODEL, MOD_TN), lambda l, j: (l, 0, j)),
                  pl.BlockSpec((1, 1, MOD_TN), lambda l, j: (l, 0, j))],
        out_specs=pl.BlockSpec((1, N_COND, MOD_TN), lambda l, j: (l, 0, j)),
        out_shape=jax.ShapeDtypeStruct((DEPTH, N_COND, n), F32),
        compiler_params=_cparams(("arbitrary", "arbitrary")),
        name="mods",
    )(cond, w_mod, b_mod.reshape(DEPTH, 1, n))


_CTX_ROWS = pl.BlockSpec((TR, D_MODEL), lambda i: (jnp.minimum(i, N_CTX_TILES - 1), 0))
_LAT_ROWS = pl.BlockSpec((TR, D_MODEL), lambda i: (jnp.maximum(i - N_CTX_TILES, 0), 0))


def _mod_spec(layer, tiles_per_tr=1):
    return pl.BlockSpec((1, 1, N_MOD, D_MODEL), lambda i: (layer, _cond_row(i // tiles_per_tr), 0, 0))


def _inproj_kernel(n_alias, xc_ref, xl_ref, mod_ref, g_ref, w_ref, cos_ref, shi_ref, slo_ref, *rest):
    q_ref, kv_ref, fm_ref, nk_ref, nv_ref = rest[n_alias:]
    i = pl.program_id(0)
    shift, scale = mod_ref[0, 0, 0:1, :], mod_ref[0, 0, 1:2, :]
    x = jnp.where(i < N_CTX_TILES, xc_ref[...], xl_ref[...])
    h = (_rmsnorm(x, g_ref[0]) * (1.0 + scale) + shift).astype(BF16)
    fm_ref[...] = jnp.dot(h, w_ref[0, :, ATTN_W + 2 * KV_W:], preferred_element_type=F32).astype(BF16)
    qkv = jnp.dot(h, w_ref[0, :, :ATTN_W + 2 * KV_W], preferred_element_type=F32)
    kv_ref[:, KV_W:] = qkv[:, ATTN_W + KV_W:]

    @pl.when(i < N_CTX_TILES)
    def _():
        q_ref[...] = qkv[:, :ATTN_W].astype(BF16)
        kv_ref[:, :KV_W] = qkv[:, ATTN_W:ATTN_W + KV_W]
        nk_ref[:, 0] = qkv[:, ATTN_W:ATTN_W + KV_W].reshape(TR // SEQ, SEQ, KV_W)
        nv_ref[:, 0] = qkv[:, ATTN_W + KV_W:].reshape(TR // SEQ, SEQ, KV_W)

    @pl.when(i >= N_CTX_TILES)
    def _():
        cos, shi, slo = cos_ref[...], shi_ref[...], slo_ref[...]
        for hd in range(N_HEADS + N_KV_HEADS):
            xh = qkv[:, hd * HEAD_DIM:(hd + 1) * HEAD_DIM]
            r = xh * cos + pltpu.roll(xh, HEAD_DIM - 32, 1) * shi + pltpu.roll(xh, 32, 1) * slo
            if hd < N_HEADS:
                q_ref[:, hd * HEAD_DIM:(hd + 1) * HEAD_DIM] = r.astype(BF16)
            else:
                kv_ref[:, (hd - N_HEADS) * HEAD_DIM:(hd - N_HEADS + 1) * HEAD_DIM] = r


def _inproj(layer, x, mods, g_mix, w_in_bf, rope, new_kv):
    lat_blk = lambda i: (jnp.maximum(i - N_CTX_TILES, 0) % LAT_TILES_PER_SEQ, 0)
    row = lambda i: (i, 0)
    kv_blk = pl.BlockSpec((TR // SEQ, 1, SEQ, KV_W), lambda i: (jnp.minimum(i, N_CTX_TILES - 1), layer, 0, 0))
    kv_shape = jax.ShapeDtypeStruct((BATCH, DEPTH, SEQ, KV_W), F32)
    n_in = 8
    return pl.pallas_call(
        functools.partial(_inproj_kernel, len(new_kv)),
        grid=(N_TILES,),
        in_specs=[_CTX_ROWS, _LAT_ROWS,
                  _mod_spec(layer),
                  pl.BlockSpec((1, 1, D_MODEL), lambda i: (layer, 0, 0)),
                  pl.BlockSpec((1, D_MODEL, IN_W), lambda i: (layer, 0, 0), pipeline_mode=pl.Buffered(1)),
                  pl.BlockSpec((TR, HEAD_DIM), lat_blk),
                  pl.BlockSpec((TR, HEAD_DIM), lat_blk),
                  pl.BlockSpec((TR, HEAD_DIM), lat_blk)] + [pl.BlockSpec(memory_space=pl.ANY)] * len(new_kv),
        out_specs=[pl.BlockSpec((TR, ATTN_W), row),
                   pl.BlockSpec((TR, 2 * KV_W), row),
                   pl.BlockSpec((TR, FOURIER_W + 3 * CONV_W), row),
                   kv_blk, kv_blk],
        out_shape=[jax.ShapeDtypeStruct((T_ALL, ATTN_W), BF16),
                   jax.ShapeDtypeStruct((T_ALL, 2 * KV_W), F32),
                   jax.ShapeDtypeStruct((T_ALL, FOURIER_W + 3 * CONV_W), BF16),
                   kv_shape, kv_shape],
        input_output_aliases={n_in + a: 3 + a for a in range(len(new_kv))},
        compiler_params=_cparams(("arbitrary",)),
        name="inproj",
    )(*x, mods, g_mix.reshape(DEPTH, 1, D_MODEL), w_in_bf, *rope, *new_kv)


def _fourier_conv(fm_ref, ccsc_ref, csl_ref, cw_ref, o_ref):
    L = fm_ref.shape[0]
    for g in range(FOURIER_GROUPS):
        xg = fm_ref[:, g * FOURIER_GROUP_W:(g + 1) * FOURIER_GROUP_W]
        z = jnp.dot(xg, ccsc_ref[...], preferred_element_type=F32)
        zz = jnp.concatenate([z[:, :FOURIER_GROUP_W], z[:, FOURIER_GROUP_W:]], axis=0).astype(BF16)
        f = jnp.dot(csl_ref[...], zz, preferred_element_type=F32)
        o_ref[:, ATTN_W + g * FOURIER_GROUP_W:ATTN_W + (g + 1) * FOURIER_GROUP_W] = f.astype(BF16)
    xc = fm_ref[:, FOURIER_W:FOURIER_W + CONV_W].astype(F32)
    bg = fm_ref[:, FOURIER_W + CONV_W:FOURIER_W + 2 * CONV_W].astype(F32)
    cg = fm_ref[:, FOURIER_W + 2 * CONV_W:].astype(F32)
    u = cg * xc
    rows = lax.broadcasted_iota(I32, u.shape, 0)
    prev = jnp.where(rows == 0, 0.0, pltpu.roll(u, 1, 0))
    nxt = jnp.where(rows == L - 1, 0.0, pltpu.roll(u, L - 1, 0))
    y = prev * cw_ref[0, 0:1, :] + u * cw_ref[0, 1:2, :] + nxt * cw_ref[0, 2:3, :]
    o_ref[:, ATTN_W + FOURIER_W:] = (bg * y).astype(BF16)


def _ctx_mix_kernel(layer, sink_ref, q_ref, kv_ref, fm_ref, ccsc_ref, csl_ref, cw_ref, o_ref):
    for h in range(N_HEADS):
        g = h // Q_PER_KV
        q = q_ref[:, h * HEAD_DIM:(h + 1) * HEAD_DIM]
        k = kv_ref[:, g * HEAD_DIM:(g + 1) * HEAD_DIM].astype(BF16)
        v = kv_ref[:, KV_W + g * HEAD_DIM:KV_W + (g + 1) * HEAD_DIM].astype(BF16)
        s = lax.dot_general(q, k, (((1,), (1,)), ((), ())), preferred_element_type=F32) * SCALE
        sink = sink_ref[layer, h]
        m = jnp.maximum(jnp.max(s, axis=-1, keepdims=True), sink)
        e = jnp.exp(s - m)
        den = jnp.sum(e, axis=-1, keepdims=True) + jnp.exp(sink - m)
        o = jnp.dot(e.astype(BF16), v, preferred_element_type=F32) / den
        o_ref[:, h * HEAD_DIM:(h + 1) * HEAD_DIM] = o.astype(BF16)
    _fourier_conv(fm_ref, ccsc_ref, csl_ref, cw_ref, o_ref)


def _ctx_mix(layer, sink, q, kv, fm, ccsc, csl, conv_w):
    row = lambda b: (b, 0)
    const = lambda b: (0, 0)
    return pl.pallas_call(
        functools.partial(_ctx_mix_kernel, layer),
        grid=(BATCH,),
        in_specs=[pl.BlockSpec(memory_space=pltpu.SMEM),
                  pl.BlockSpec((SEQ, ATTN_W), row),
                  pl.BlockSpec((SEQ, 2 * KV_W), row),
                  pl.BlockSpec((SEQ, FOURIER_W + 3 * CONV_W), row),
                  pl.BlockSpec((FOURIER_GROUP_W, 2 * FOURIER_GROUP_W), const),
                  pl.BlockSpec((SEQ, 2 * SEQ), const),
                  pl.BlockSpec((1, 3, CONV_W), lambda b: (layer, 0, 0))],
        out_specs=pl.BlockSpec((SEQ, D_MODEL), row),
        out_shape=jax.ShapeDtypeStruct((T_CTX, D_MODEL), BF16),
        compiler_params=_cparams(("arbitrary",)),
        name="ctx_mix",
    )(sink, q, kv, fm, ccsc, csl, conv_w)


def _lat_mix_kernel(layer, sink_ref, q_ref, kv_ref, fm_ref, ck_ref, cv_ref, ccsc_ref, csl_ref, cw_ref, o_ref):
    n = pl.program_id(1)
    nb = DEC_SEQ // BLOCK
    rows = Q_PER_KV * BLOCK
    band = 3 * BLOCK

    @pl.when(n == 0)
    def _():
        _fourier_conv(fm_ref, ccsc_ref, csl_ref, cw_ref, o_ref)

    q0 = pl.multiple_of(n * BLOCK, BLOCK)
    k0 = pl.multiple_of(jnp.clip(n - 1, 0, nb - 3) * BLOCK, BLOCK)
    qpos = q0 + lax.broadcasted_iota(I32, (rows, band), 0) % BLOCK
    kpos = k0 + lax.broadcasted_iota(I32, (rows, band), 1)
    valid = jnp.abs(qpos - kpos) <= WINDOW
    for g in range(N_KV_HEADS):
        kb = kv_ref[pl.ds(k0, band), g * HEAD_DIM:(g + 1) * HEAD_DIM].astype(BF16)
        vb = kv_ref[pl.ds(k0, band), KV_W + g * HEAD_DIM:KV_W + (g + 1) * HEAD_DIM].astype(BF16)
        k_ctx = ck_ref[0, 0, :, g * HEAD_DIM:(g + 1) * HEAD_DIM].astype(BF16)
        v_ctx = cv_ref[0, 0, :, g * HEAD_DIM:(g + 1) * HEAD_DIM].astype(BF16)
        sink = jnp.concatenate(
            [jnp.full((BLOCK, 1), sink_ref[layer, g * Q_PER_KV + j], F32) for j in range(Q_PER_KV)], axis=0)
        q = jnp.concatenate(
            [q_ref[pl.ds(q0, BLOCK), (g * Q_PER_KV + j) * HEAD_DIM:(g * Q_PER_KV + j + 1) * HEAD_DIM]
             for j in range(Q_PER_KV)], axis=0)
        s_loc = lax.dot_general(q, kb, (((1,), (1,)), ((), ())), preferred_element_type=F32) * SCALE
        s_loc = jnp.where(valid, s_loc, NEG)
        s_ctx = lax.dot_general(q, k_ctx, (((1,), (1,)), ((), ())), preferred_element_type=F32) * SCALE
        m = jnp.maximum(jnp.maximum(jnp.max(s_loc, axis=-1, keepdims=True),
                                    jnp.max(s_ctx, axis=-1, keepdims=True)), sink)
        e_loc, e_ctx = jnp.exp(s_loc - m), jnp.exp(s_ctx - m)
        den = (jnp.sum(e_loc, axis=-1, keepdims=True) + jnp.sum(e_ctx, axis=-1, keepdims=True)
               + jnp.exp(sink - m))
        o = (jnp.dot(e_loc.astype(BF16), vb, preferred_element_type=F32)
             + jnp.dot(e_ctx.astype(BF16), v_ctx, preferred_element_type=F32)) / den
        for j in range(Q_PER_KV):
            h = g * Q_PER_KV + j
            o_ref[pl.ds(q0, BLOCK), h * HEAD_DIM:(h + 1) * HEAD_DIM] = o[j * BLOCK:(j + 1) * BLOCK].astype(BF16)


def _lat_mix(layer, sink, q, kv, fm, cache_k, cache_v, ccsc, csl, conv_w):
    off = T_CTX // DEC_SEQ
    row = lambda b, n: (off + b, 0)
    const = lambda b, n: (0, 0)
    cache = lambda b, n: (b, layer, 0, 0)
    return pl.pallas_call(
        functools.partial(_lat_mix_kernel, layer),
        grid=(DEC_BATCH, DEC_SEQ // BLOCK),
        in_specs=[pl.BlockSpec(memory_space=pltpu.SMEM),
                  pl.BlockSpec((DEC_SEQ, ATTN_W), row),
                  pl.BlockSpec((DEC_SEQ, 2 * KV_W), row),
                  pl.BlockSpec((DEC_SEQ, FOURIER_W + 3 * CONV_W), row),
                  pl.BlockSpec((1, 1, SEQ, KV_W), cache),
                  pl.BlockSpec((1, 1, SEQ, KV_W), cache),
                  pl.BlockSpec((FOURIER_GROUP_W, 2 * FOURIER_GROUP_W), const),
                  pl.BlockSpec((DEC_SEQ, 2 * DEC_SEQ), const, pipeline_mode=pl.Buffered(1)),
                  pl.BlockSpec((1, 3, CONV_W), lambda b, n: (layer, 0, 0))],
        out_specs=pl.BlockSpec((DEC_SEQ, D_MODEL), lambda b, n: (b, 0)),
        out_shape=jax.ShapeDtypeStruct((T_LAT, D_MODEL), BF16),
        compiler_params=_cparams(("arbitrary", "arbitrary")),
        name="lat_mix",
    )(sink, q, kv, fm, cache_k, cache_v, ccsc, csl, conv_w)


def _first_index_of_max(vals, iota, sentinel):
    mx = jnp.max(vals, axis=0, keepdims=True)
    idx = jnp.min(jnp.where(vals == mx, iota, sentinel), axis=0, keepdims=True)
    return mx, idx


def _outproj_kernel(mixc_ref, mixl_ref, xc_ref, xl_ref, mod_ref, g_ref, w_ref, wr_ref, rb_ref, tri_ref,
                    x1_ref, h2_ref, ids_ref, wts_ref, rank_ref, cnt_ref, run_ref):
    i = pl.program_id(0)

    @pl.when(i == 0)
    def _():
        run_ref[...] = jnp.zeros_like(run_ref)

    gate1, shift2, scale2 = mod_ref[0, 0, 2:3, :], mod_ref[0, 0, 3:4, :], mod_ref[0, 0, 4:5, :]
    mix = jnp.where(i < N_CTX_TILES, mixc_ref[...], mixl_ref[...])
    x = jnp.where(i < N_CTX_TILES, xc_ref[...], xl_ref[...])
    x1 = x + gate1 * jnp.dot(mix, w_ref[0], preferred_element_type=F32)
    x1_ref[...] = x1
    h2 = _rmsnorm(x1, g_ref[0]) * (1.0 + scale2) + shift2
    _store_slabs(h2_ref, _pack_halves(h2))

    nt = (((1,), (1,)), ((), ()))
    h_hi = h2.astype(BF16)
    h_lo = (h2 - h_hi.astype(F32)).astype(BF16)
    part = lax.dot_general(wr_ref[0], h_hi, nt, preferred_element_type=F32)
    logits = (part[:N_EXPERTS] + part[N_EXPERTS:]
              + lax.dot_general(wr_ref[0, :N_EXPERTS, :], h_lo, nt, preferred_element_type=F32))
    scores = _sigmoid(logits)
    biased = scores + rb_ref[0]
    ninf = -jnp.inf
    mem = lax.broadcasted_iota(I32, (EXPERTS_PER_GROUP, TR), 0)
    grp_rows = []
    for g in range(N_GROUPS):
        bgp = biased[g * EXPERTS_PER_GROUP:(g + 1) * EXPERTS_PER_GROUP]
        m1, i1 = _first_index_of_max(bgp, mem, EXPERTS_PER_GROUP)
        m2 = jnp.max(jnp.where(mem == i1, ninf, bgp), axis=0, keepdims=True)
        grp_rows.append(m1 + m2)
    cur = jnp.concatenate(grp_rows, axis=0)
    gio = lax.broadcasted_iota(I32, (N_GROUPS, TR), 0)
    gsel = jnp.zeros((N_GROUPS, TR), F32)
    for _ in range(TOPK_GROUPS):
        _, gi = _first_index_of_max(cur, gio, N_GROUPS)
        hit = gio == gi
        gsel = jnp.where(hit, 1.0, gsel)
        cur = jnp.where(hit, ninf, cur)
    cur = jnp.concatenate(
        [jnp.where(gsel[g:g + 1] > 0.0, biased[g * EXPERTS_PER_GROUP:(g + 1) * EXPERTS_PER_GROUP], ninf)
         for g in range(N_GROUPS)], axis=0)
    eio = lax.broadcasted_iota(I32, (N_EXPERTS, TR), 0)
    chosen = jnp.zeros((N_EXPERTS, TR), F32)
    ids, sel = [], []
    for _ in range(TOP_K):
        _, ei = _first_index_of_max(cur, eio, N_EXPERTS)
        hit = eio == ei
        ids.append(ei)
        sel.append(jnp.sum(jnp.where(hit, scores, 0.0), axis=0, keepdims=True))
        chosen = jnp.where(hit, 1.0, chosen)
        cur = jnp.where(hit, ninf, cur)
    total = functools.reduce(lambda a, b: a + b, sel)
    ids_ref[...] = jnp.concatenate(ids, axis=0)
    wts_ref[...] = jnp.concatenate([s / total * ROUTE_SCALE for s in sel], axis=0)

    before = jnp.dot(chosen.astype(BF16), tri_ref[...], preferred_element_type=F32) + run_ref[:, 0:1]
    rank_ref[...] = jnp.concatenate(
        [jnp.sum(jnp.where(eio == ei, before, 0.0), axis=0, keepdims=True) for ei in ids], axis=0).astype(I32)
    run = run_ref[...] + jnp.sum(chosen, axis=1, keepdims=True)
    run_ref[...] = run
    cnt_ref[...] = run


def _outproj(layer, mix_ctx, mix_lat, x, mods, g_ffn, w_out_bf, wr_split, router_bias, tri):
    row = lambda i: (i, 0)
    col = lambda i: (0, i)
    const = lambda i: (0, 0)
    lay = lambda i: (layer, 0, 0)
    return pl.pallas_call(
        _outproj_kernel,
        grid=(N_TILES,),
        in_specs=[_CTX_ROWS, _LAT_ROWS, _CTX_ROWS, _LAT_ROWS,
                  _mod_spec(layer),
                  pl.BlockSpec((1, 1, D_MODEL), lay),
                  pl.BlockSpec((1, D_MODEL, D_MODEL), lay, pipeline_mode=pl.Buffered(1)),
                  pl.BlockSpec((1, 2 * N_EXPERTS, D_MODEL), lay),
                  pl.BlockSpec((1, N_EXPERTS, 1), lay),
                  pl.BlockSpec((TR, TR), const)],
        out_specs=[pl.BlockSpec((TR, D_MODEL), row),
                   pl.BlockSpec((TR * SLAB, LANES), row),
                   pl.BlockSpec((TOP_K, TR), col),
                   pl.BlockSpec((TOP_K, TR), col),
                   pl.BlockSpec((TOP_K, TR), col),
                   pl.BlockSpec((N_EXPERTS, META_LANES), const)],
        out_shape=[jax.ShapeDtypeStruct((T_ALL, D_MODEL), F32),
                   jax.ShapeDtypeStruct((T_ALL * SLAB, LANES), U32),
                   jax.ShapeDtypeStruct((TOP_K, T_ALL), I32),
                   jax.ShapeDtypeStruct((TOP_K, T_ALL), F32),
                   jax.ShapeDtypeStruct((TOP_K, T_ALL), I32),
                   jax.ShapeDtypeStruct((N_EXPERTS, META_LANES), F32)],
        scratch_shapes=[pltpu.VMEM((N_EXPERTS, META_LANES), F32)],
        compiler_params=_cparams(("arbitrary",)),
        name="outproj",
    )(mix_ctx, mix_lat, *x, mods, g_ffn.reshape(DEPTH, 1, D_MODEL), w_out_bf, wr_split,
      router_bias.reshape(DEPTH, N_EXPERTS, 1), tri)


def _positions_kernel(ids_ref, rank_ref, cnt_ref, pos_ref, te_ref, meta_ref):
    ids = ids_ref[...]
    pos = rank_ref[...]
    tile_io = lax.broadcasted_iota(I32, (1, TE_LANES), 1)
    lane_io = lax.broadcasted_iota(I32, (1, META_LANES), 1)
    te = jnp.zeros((1, TE_LANES), I32)
    starts = jnp.zeros((1, META_LANES), I32)
    counts = jnp.zeros((1, META_LANES), I32)
    start = jnp.zeros((1, 1), I32)
    for e in range(N_EXPERTS):
        c = cnt_ref[e:e + 1, 0:1].astype(I32)
        pos = pos + jnp.where(ids == e, start, 0)
        starts = starts + jnp.where(lane_io == e, start, 0)
        counts = counts + jnp.where(lane_io == e, c, 0)
        start = start + ((c + (TM - 1)) // TM) * TM
        te = te + jnp.where(tile_io * TM >= start, 1, 0)
    pos_ref[...] = pos
    te_ref[...] = jnp.minimum(te, N_EXPERTS - 1)
    meta_ref[0:1, :] = starts
    meta_ref[1:2, :] = counts
    meta_ref[2:3, :] = jnp.broadcast_to(start // TM, (1, META_LANES))
    meta_ref[3:8, :] = jnp.zeros((5, META_LANES), I32)


def _positions(ids, rank, cnt):
    return pl.pallas_call(
        _positions_kernel,
        out_shape=[jax.ShapeDtypeStruct((TOP_K, T_ALL), I32),
                   jax.ShapeDtypeStruct((1, TE_LANES), I32),
                   jax.ShapeDtypeStruct((8, META_LANES), I32)],
        compiler_params=_cparams(),
        name="positions",
    )(ids, rank, cnt)


SC_CORES, SC_SUBCORES = 2, 16
SC_WORKERS = SC_CORES * SC_SUBCORES
SC_CHUNK = 32


def _sc_kernel(body, out_type, n_idx_rows, name):
    return pl.kernel(
        body,
        out_type=out_type,
        mesh=plsc.VectorSubcoreMesh(core_axis_name="c", subcore_axis_name="s",
                                    num_cores=SC_CORES, num_subcores=SC_SUBCORES),
        scratch_types=[pltpu.VMEM((n_idx_rows, SC_CHUNK), I32),
                       pltpu.VMEM((2, SC_CHUNK, SLAB, LANES), U32),
                       pltpu.SemaphoreType.DMA,
                       pltpu.SemaphoreType.DMA((2,)),
                       pltpu.SemaphoreType.DMA((2,))],
        compiler_params=pltpu.CompilerParams(use_tc_tiling_on_sc=True),
        name=name,
    )


def _sc_worker():
    return lax.axis_index("s") * SC_CORES + lax.axis_index("c")


def _sc_load_indices(n_rows, src_of_row, idx_v, sem):
    @pl.loop(0, n_rows)
    def _(r):
        src, dst = src_of_row(r), idx_v.at[r]
        pltpu.make_async_copy(src, dst, sem).start()

    @pl.loop(0, n_rows)
    def _(r):
        src, dst = src_of_row(r), idx_v.at[r]
        pltpu.make_async_copy(src, dst, sem).wait()


def _sc_two_slot_pipeline(n_chunks, fill, fill_wait, drain, drain_wait):
    assert n_chunks % 2 == 0 and n_chunks >= 4
    fill(0, 0)

    @pl.loop(0, n_chunks // 2)
    def _(pair):
        c = 2 * pair

        @pl.when(pair >= 1)
        def _():
            drain_wait(c - 1, 1)

        fill(c + 1, 1)
        fill_wait(c, 0)
        drain(c, 0)

        @pl.when(pair < n_chunks // 2 - 1)
        def _():
            drain_wait(c, 0)
            fill(c + 2, 0)

        fill_wait(c + 1, 1)
        drain(c + 1, 1)

    drain_wait(n_chunks - 2, 0)
    drain_wait(n_chunks - 1, 1)


def _dispatch(pos_flat, h3):
    n_chunks = T_ALL // SC_WORKERS // SC_CHUNK

    def body(pos_hbm, h_hbm, xs_hbm, idx_v, rows_v, idx_sem, fill_sem, drain_sem):
        t_base = _sc_worker() * (n_chunks * SC_CHUNK)

        def index_src(r):
            t0 = t_base + (r // TOP_K) * SC_CHUNK
            return pos_hbm.at[pl.ds(pl.multiple_of((r % TOP_K) * T_ALL + t0, SC_CHUNK), SC_CHUNK)]

        _sc_load_indices(n_chunks * TOP_K, index_src, idx_v, idx_sem)

        def load(c, slot):
            t0 = pl.multiple_of(t_base + c * SC_CHUNK, SC_CHUNK)
            return pltpu.make_async_copy(h_hbm.at[pl.ds(t0, SC_CHUNK)], rows_v.at[slot], fill_sem.at[slot])

        def scatter(c, slot, k):
            return pltpu.make_async_copy(rows_v.at[slot], xs_hbm.at[idx_v.at[c * TOP_K + k]], drain_sem.at[slot])

        def drain(c, slot):
            for k in range(TOP_K):
                scatter(c, slot, k).start()

        def drain_wait(c, slot):
            for k in range(TOP_K):
                scatter(c, slot, k).wait()

        _sc_two_slot_pipeline(n_chunks, lambda c, s: load(c, s).start(), lambda c, s: load(c, s).wait(),
                              drain, drain_wait)

    return _sc_kernel(body, jax.ShapeDtypeStruct((P_SORT, SLAB, LANES), U32), n_chunks * TOP_K,
                      "sc_dispatch")(pos_flat, h3)


def _gather(pos_flat, y3):
    n_items = TOP_K * T_ALL
    n_chunks = n_items // SC_WORKERS // SC_CHUNK

    def body(pos_hbm, y_hbm, out_hbm, idx_v, rows_v, idx_sem, fill_sem, drain_sem):
        base = _sc_worker() * (n_chunks * SC_CHUNK)

        def index_src(r):
            return pos_hbm.at[pl.ds(pl.multiple_of(base + r * SC_CHUNK, SC_CHUNK), SC_CHUNK)]

        _sc_load_indices(n_chunks, index_src, idx_v, idx_sem)

        def gather(c, slot):
            return pltpu.make_async_copy(y_hbm.at[idx_v.at[c]], rows_v.at[slot], fill_sem.at[slot])

        def store(c, slot):
            i0 = pl.multiple_of(base + c * SC_CHUNK, SC_CHUNK)
            return pltpu.make_async_copy(rows_v.at[slot], out_hbm.at[pl.ds(i0, SC_CHUNK)], drain_sem.at[slot])

        _sc_two_slot_pipeline(n_chunks, lambda c, s: gather(c, s).start(), lambda c, s: gather(c, s).wait(),
                              lambda c, s: store(c, s).start(), lambda c, s: store(c, s).wait())

    return _sc_kernel(body, jax.ShapeDtypeStruct((n_items, SLAB, LANES), U32), n_chunks,
                      "sc_gather")(pos_flat, y3)


def _experts_kernel(te_ref, nt_ref, end_ref, x_ref, wg_ref, wu_ref, wd_ref, y_ref, wg_bf, wu_bf, wd_bf):
    j = pl.program_id(0)
    used = j < nt_ref[0]
    prev = te_ref[jnp.maximum(j - 1, 0)]

    @pl.when(used & ((j == 0) | (te_ref[j] != prev)))
    def _():
        wg_bf[...] = wg_ref[0, 0].astype(BF16)
        wu_bf[...] = wu_ref[0, 0].astype(BF16)
        wd_bf[...] = wd_ref[0, 0].astype(BF16)

    valid = end_ref[te_ref[jnp.minimum(j, nt_ref[0] - 1)]] - j * TM
    sub = TM // 2

    def half(row0):
        rows = lax.broadcasted_iota(I32, (sub, HALF), 0) + row0
        words = jnp.where(rows < valid, _load_slabs(x_ref, sub, row0), jnp.uint32(0))
        xa, xb = _unpack_halves(words)
        xa, xb = xa.astype(BF16), xb.astype(BF16)
        g = (jnp.dot(xa, wg_bf[:HALF], preferred_element_type=F32)
             + jnp.dot(xb, wg_bf[HALF:], preferred_element_type=F32))
        u = (jnp.dot(xa, wu_bf[:HALF], preferred_element_type=F32)
             + jnp.dot(xb, wu_bf[HALF:], preferred_element_type=F32))
        a = (_silu(g) * u).astype(BF16)
        _store_slabs(y_ref, _pack_halves(jnp.dot(a, wd_bf[...], preferred_element_type=F32)), row0)

    @pl.when(used)
    def _():
        half(0)

    @pl.when(used & (valid > sub))
    def _():
        half(sub)


def _experts(layer, te, nt, end, xs, w_gate, w_up, w_down):
    n_tiles = xs.shape[0] // (TM * SLAB)
    tile = lambda j, te, nt, end: (jnp.minimum(j, nt[0] - 1), 0)
    wsel = lambda j, te, nt, end: (layer, te[jnp.minimum(j, nt[0] - 1)], 0, 0)
    return pl.pallas_call(
        _experts_kernel,
        grid_spec=pltpu.PrefetchScalarGridSpec(
            num_scalar_prefetch=3,
            grid=(n_tiles,),
            in_specs=[pl.BlockSpec((TM * SLAB, LANES), tile),
                      pl.BlockSpec((1, 1, D_MODEL, EXPERT_FF), wsel),
                      pl.BlockSpec((1, 1, D_MODEL, EXPERT_FF), wsel),
                      pl.BlockSpec((1, 1, EXPERT_FF, D_MODEL), wsel)],
            out_specs=pl.BlockSpec((TM * SLAB, LANES), tile),
            scratch_shapes=[pltpu.VMEM((D_MODEL, EXPERT_FF), BF16),
                            pltpu.VMEM((D_MODEL, EXPERT_FF), BF16),
                            pltpu.VMEM((EXPERT_FF, D_MODEL), BF16)]),
        out_shape=jax.ShapeDtypeStruct(xs.shape, U32),
        compiler_params=_cparams(("arbitrary",)),
        name="experts",
    )(te, nt, end, xs, w_gate, w_up, w_down)


def _combine_kernel(final, wts_ref, x1_ref, ysh_ref, mod_ref, g_ref, buf_ref, *rest):
    oc_ref, ol_ref, x2_ref, wb_ref = rest
    i = pl.program_id(0)
    w = jnp.transpose(wts_ref[...])
    for k in range(TOP_K):
        wb_ref[k] = jnp.broadcast_to(w[:, k:k + 1], (TC_ROWS, LANES))
    sq = jnp.zeros((TC_ROWS, LANES), F32)
    for s in range(SLAB):
        acc_a, acc_b = _unpack_halves(ysh_ref[pl.ds(s, TC_ROWS, stride=SLAB), :])
        for k in range(TOP_K):
            ya, yb = _unpack_halves(buf_ref[k, pl.ds(s, TC_ROWS, stride=SLAB), :])
            acc_a = acc_a + wb_ref[k] * ya
            acc_b = acc_b + wb_ref[k] * yb
        for half, acc in ((0, acc_a), (1, acc_b)):
            cols = slice(half * HALF + s * LANES, half * HALF + (s + 1) * LANES)
            x2 = x1_ref[:, cols] + mod_ref[0, 0, 5:6, cols] * acc
            x2_ref[:, cols] = x2
            if final:
                sq = sq + x2 * x2
    out = x2_ref[...]
    if final:
        ms = jnp.sum(sq, axis=-1, keepdims=True) * (1.0 / D_MODEL)
        out = out * lax.rsqrt(ms + EPS) * g_ref[...]

    @pl.when(i < T_CTX // TC_ROWS)
    def _():
        oc_ref[...] = out

    @pl.when(i >= T_CTX // TC_ROWS)
    def _():
        ol_ref[...] = out


def _combine(layer, wts, x1, ysh, mods, final_norm, yg):
    final = layer == DEPTH - 1
    row = lambda i: (i, 0)
    col = lambda i: (0, i)
    n_ctx = T_CTX // TC_ROWS
    out_specs = [pl.BlockSpec((TC_ROWS, D_MODEL), lambda i: (jnp.minimum(i, n_ctx - 1), 0)),
                 pl.BlockSpec((TC_ROWS, D_MODEL), lambda i: (jnp.maximum(i - n_ctx, 0), 0))]
    out_shape = [jax.ShapeDtypeStruct((T_CTX, D_MODEL), F32), jax.ShapeDtypeStruct((T_LAT, D_MODEL), F32)]
    return pl.pallas_call(
        functools.partial(_combine_kernel, final),
        grid=(T_ALL // TC_ROWS,),
        in_specs=[pl.BlockSpec((TOP_K, TC_ROWS), col),
                  pl.BlockSpec((TC_ROWS, D_MODEL), row),
                  pl.BlockSpec((TC_ROWS * SLAB, LANES), row),
                  _mod_spec(layer, TR // TC_ROWS),
                  pl.BlockSpec((1, D_MODEL), lambda i: (0, 0)),
                  pl.BlockSpec((TOP_K, TC_ROWS * SLAB, LANES), lambda i: (0, i, 0))],
        out_specs=out_specs,
        out_shape=out_shape,
        scratch_shapes=[pltpu.VMEM((TC_ROWS, D_MODEL), F32),
                        pltpu.VMEM((TOP_K, TC_ROWS, LANES), F32)],
        compiler_params=_cparams(("arbitrary",)),
        name="combine",
    )(wts, x1, ysh, mods, final_norm.reshape(1, D_MODEL), yg)


def kernel(x_prompt, x_sample, cache_k, cache_v, c, c_ctx, w_mod, b_mod, g_mix, g_ffn, w_in, w_out, conv_w,
           attn_sink, w_router, router_bias, w_gate_e, w_up_e, w_down_e, w_gate_s, w_up_s, w_down_s, final_norm):
    ccsc = jnp.asarray(_channel_dft(), BF16)
    csl_ctx = jnp.asarray(_dft_tables(SEQ), BF16)
    csl_lat = jnp.asarray(_dft_tables(DEC_SEQ), BF16)
    rope = tuple(jnp.asarray(t, F32) for t in _rope_tables(DEC_SEQ))
    tri = jnp.asarray(np.triu(np.ones((TR, TR)), 1), BF16)
    shared_te = jnp.zeros((T_ALL // TM,), I32)
    shared_nt = jnp.full((1,), T_ALL // TM, I32)
    shared_end = jnp.full((1,), T_ALL, I32)

    cond = jnp.concatenate([c_ctx[None], c, jnp.zeros((N_COND - 1 - DEC_BATCH, D_MODEL), F32)], axis=0)
    mods = _mods(cond, w_mod, b_mod).reshape(DEPTH, N_COND, N_MOD, D_MODEL)
    cache_k = cache_k.reshape(DEC_BATCH, DEPTH, SEQ, KV_W)
    cache_v = cache_v.reshape(DEC_BATCH, DEPTH, SEQ, KV_W)
    w_in_bf, w_out_bf = w_in.astype(BF16), w_out.astype(BF16)
    wr_t = jnp.swapaxes(w_router, 1, 2)
    wr_hi = wr_t.astype(BF16)
    wr_split = jnp.concatenate([wr_hi, (wr_t - wr_hi.astype(F32)).astype(BF16)], axis=1)
    ws_gate, ws_up, ws_down = w_gate_s[:, None], w_up_s[:, None], w_down_s[:, None]
    x = (x_prompt.reshape(T_CTX, D_MODEL), x_sample.reshape(T_LAT, D_MODEL))

    new_kv = ()
    for l in range(DEPTH):
        q, kv, fm, *new_kv = _inproj(l, x, mods, g_mix, w_in_bf, rope, tuple(new_kv))
        mix_ctx = _ctx_mix(l, attn_sink, q, kv, fm, ccsc, csl_ctx, conv_w)
        mix_lat = _lat_mix(l, attn_sink, q, kv, fm, cache_k, cache_v, ccsc, csl_lat, conv_w)
        x1, h2, ids, wts, rank, cnt = _outproj(l, mix_ctx, mix_lat, x, mods, g_ffn, w_out_bf, wr_split,
                                               router_bias, tri)
        pos, te, meta = _positions(ids, rank, cnt)
        pos_flat = pos.reshape(TOP_K * T_ALL)
        xs = _dispatch(pos_flat, h2.reshape(T_ALL, SLAB, LANES)).reshape(P_SORT * SLAB, LANES)
        y = _experts(l, te[0, :N_SORT_TILES], meta[2, :1], meta[0, :N_EXPERTS] + meta[1, :N_EXPERTS], xs,
                     w_gate_e, w_up_e, w_down_e)
        ysh = _experts(l, shared_te, shared_nt, shared_end, h2, ws_gate, ws_up, ws_down)
        yg = _gather(pos_flat, y.reshape(P_SORT, SLAB, LANES)).reshape(TOP_K, T_ALL * SLAB, LANES)
        x = _combine(l, wts, x1, ysh, mods, final_norm, yg)

    y_prompt = x[0].reshape(BATCH, SEQ, D_MODEL)
    y_sample = x[1].reshape(DEC_BATCH, DEC_SEQ, D_MODEL)
    new_k, new_v = (a.reshape(BATCH, DEPTH, SEQ, N_KV_HEADS, HEAD_DIM) for a in new_kv)
    return y_prompt, y_sample, new_k, new_v
```

```python
import functools

import numpy as np
import jax
import jax.numpy as jnp
from jax import lax
from jax.experimental import pallas as pl
from jax.experimental.pallas import tpu as pltpu
from jax.experimental.pallas import tpu_sc as plsc

F32 = jnp.float32
BF16 = jnp.bfloat16
I32 = jnp.int32
U32 = jnp.uint32

D_MODEL = 2048
HALF = D_MODEL // 2
BATCH, SEQ = 32, 256
DEC_BATCH, DEC_SEQ = 4, 1024
DEPTH = 2
GRID_W = 64
HEAD_DIM = 128
N_HEADS, N_KV_HEADS, Q_PER_KV = 8, 2, 4
ATTN_W, KV_W = 1024, 256
WINDOW, BLOCK = 128, 128
SCALE = HEAD_DIM ** -0.5
ROPE_BASE = 10000.0
ROPE_PAIRS = 32
FOURIER_W, FOURIER_GROUPS, FOURIER_GROUP_W = 512, 4, 128
CONV_W = 512
IN_W = 3584
N_EXPERTS, TOP_K, N_GROUPS, EXPERTS_PER_GROUP, TOPK_GROUPS = 64, 8, 8, 8, 4
EXPERT_FF = 512
ROUTE_SCALE = 2.5
N_MOD = 6
EPS = 1e-6
NEG = -1e30

T_CTX = BATCH * SEQ
T_LAT = DEC_BATCH * DEC_SEQ
T_ALL = T_CTX + T_LAT
TR = 512
N_CTX_TILES = T_CTX // TR
LAT_TILES_PER_SEQ = DEC_SEQ // TR
N_TILES = T_ALL // TR
N_COND = 8
MOD_TN = 1024
TM = 512
TC_ROWS = 128
N_SORT_TILES = T_ALL * TOP_K // TM + N_EXPERTS
P_SORT = N_SORT_TILES * TM
META_LANES = 128
TE_LANES = 512
VMEM_LIMIT = 56 * 1024 * 1024


def _cparams(sem=None):
    return pltpu.CompilerParams(dimension_semantics=sem, vmem_limit_bytes=VMEM_LIMIT)


def _silu(x):
    return x / (1.0 + jnp.exp(-x))


def _sigmoid(x):
    return 1.0 / (1.0 + jnp.exp(-x))


def _rmsnorm(x, g):
    return x * lax.rsqrt(jnp.mean(x * x, axis=-1, keepdims=True) + EPS) * g


def _cond_row(i):
    return jnp.where(i < N_CTX_TILES, 0, 1 + (i - N_CTX_TILES) // LAT_TILES_PER_SEQ)


def _pack_halves(v):
    n = v.shape[1] // 2
    hi = pltpu.bitcast(v[:, :n].astype(BF16).astype(F32), U32)
    lo = pltpu.bitcast(v[:, n:].astype(BF16).astype(F32), U32)
    return hi | (lo >> 16)


def _unpack_halves(w):
    return pltpu.bitcast(w & jnp.uint32(0xFFFF0000), F32), pltpu.bitcast(w << 16, F32)


SLAB, LANES = 8, 128


def _load_slabs(ref, n, row0=0):
    return jnp.concatenate([ref[pl.ds(row0 * SLAB + s, n, stride=SLAB), :] for s in range(SLAB)], axis=1)


def _store_slabs(ref, words, row0=0):
    n = words.shape[0]
    for s in range(SLAB):
        ref[pl.ds(row0 * SLAB + s, n, stride=SLAB), :] = words[:, s * LANES:(s + 1) * LANES]


def _dft_tables(L):
    j = np.arange(L, dtype=np.int64)
    ang = 2.0 * np.pi * ((j[:, None] * j[None, :]) % L).astype(np.float64) / L
    c, s = np.cos(ang) / np.sqrt(L), np.sin(ang) / np.sqrt(L)
    return np.concatenate([c, -s], axis=1)


def _channel_dft():
    n = FOURIER_GROUP_W
    j = np.arange(n, dtype=np.int64)
    ang = 2.0 * np.pi * ((j[:, None] * j[None, :]) % n).astype(np.float64) / n
    return np.concatenate([np.cos(ang), np.sin(ang)], axis=1) / np.sqrt(n)


def _rope_tables(S):
    pos = np.arange(S)
    row, col = (pos // GRID_W).astype(np.float64), (pos % GRID_W).astype(np.float64)
    inv = ROPE_BASE ** (-np.arange(ROPE_PAIRS, dtype=np.float64) / ROPE_PAIRS)
    ar, ac = row[:, None] * inv, col[:, None] * inv
    z = np.zeros_like(ar)
    cos = np.concatenate([np.cos(ar), np.cos(ar), np.cos(ac), np.cos(ac)], axis=1)
    sin_hi = np.concatenate([-np.sin(ar), z, -np.sin(ac), z], axis=1)
    sin_lo = np.concatenate([z, np.sin(ar), z, np.sin(ac)], axis=1)
    return cos, sin_hi, sin_lo


def _mods_kernel(cond_ref, w_ref, b_ref, o_ref):
    s = _silu(cond_ref[...]).astype(BF16)
    o_ref[0] = jnp.dot(s, w_ref[0].astype(BF16), preferred_element_type=F32) + b_ref[0]


def _mods(cond, w_mod, b_mod):
    n = N_MOD * D_MODEL
    return pl.pallas_call(
        _mods_kernel,
        grid=(DEPTH, n // MOD_TN),
        in_specs=[pl.BlockSpec((N_COND, D_MODEL), lambda l, j: (0, 0)),
                  pl.BlockSpec((1, D_MODEL, MOD_TN), lambda l, j: (l, 0, j)),
                  pl.BlockSpec((1, 1, MOD_TN), lambda l, j: (l, 0, j))],
        out_specs=pl.BlockSpec((1, N_COND, MOD_TN), lambda l, j: (l, 0, j)),
        out_shape=jax.ShapeDtypeStruct((DEPTH, N_COND, n), F32),
        compiler_params=_cparams(("arbitrary", "arbitrary")),
        name="mods",
    )(cond, w_mod, b_mod.reshape(DEPTH, 1, n))


_CTX_ROWS = pl.BlockSpec((TR, D_MODEL), lambda i: (jnp.minimum(i, N_CTX_TILES - 1), 0))
_LAT_ROWS = pl.BlockSpec((TR, D_MODEL), lambda i: (jnp.maximum(i - N_CTX_TILES, 0), 0))


def _mod_spec(layer):
    return pl.BlockSpec((1, 1, N_MOD, D_MODEL), lambda i: (layer, _cond_row(i), 0, 0))


def _inproj_kernel(n_alias, xc_ref, xl_ref, mod_ref, g_ref, w_ref, cos_ref, shi_ref, slo_ref, *rest):
    q_ref, kv_ref, fm_ref, nk_ref, nv_ref = rest[n_alias:]
    i = pl.program_id(0)
    shift, scale = mod_ref[0, 0, 0:1, :], mod_ref[0, 0, 1:2, :]
    x = jnp.where(i < N_CTX_TILES, xc_ref[...], xl_ref[...])
    h = (_rmsnorm(x, g_ref[0]) * (1.0 + scale) + shift).astype(BF16)
    fm_ref[...] = jnp.dot(h, w_ref[0, :, ATTN_W + 2 * KV_W:], preferred_element_type=F32).astype(BF16)
    qkv = jnp.dot(h, w_ref[0, :, :ATTN_W + 2 * KV_W], preferred_element_type=F32)
    kv_ref[:, KV_W:] = qkv[:, ATTN_W + KV_W:]

    @pl.when(i < N_CTX_TILES)
    def _():
        q_ref[...] = qkv[:, :ATTN_W].astype(BF16)
        kv_ref[:, :KV_W] = qkv[:, ATTN_W:ATTN_W + KV_W]
        nk_ref[:, 0] = qkv[:, ATTN_W:ATTN_W + KV_W].reshape(TR // SEQ, SEQ, KV_W)
        nv_ref[:, 0] = qkv[:, ATTN_W + KV_W:].reshape(TR // SEQ, SEQ, KV_W)

    @pl.when(i >= N_CTX_TILES)
    def _():
        cos, shi, slo = cos_ref[...], shi_ref[...], slo_ref[...]
        for hd in range(N_HEADS + N_KV_HEADS):
            xh = qkv[:, hd * HEAD_DIM:(hd + 1) * HEAD_DIM]
            r = xh * cos + pltpu.roll(xh, HEAD_DIM - 32, 1) * shi + pltpu.roll(xh, 32, 1) * slo
            if hd < N_HEADS:
                q_ref[:, hd * HEAD_DIM:(hd + 1) * HEAD_DIM] = r.astype(BF16)
            else:
                kv_ref[:, (hd - N_HEADS) * HEAD_DIM:(hd - N_HEADS + 1) * HEAD_DIM] = r


def _inproj(layer, x, mods, g_mix, w_in_bf, rope, new_kv):
    lat_blk = lambda i: (jnp.maximum(i - N_CTX_TILES, 0) % LAT_TILES_PER_SEQ, 0)
    row = lambda i: (i, 0)
    kv_blk = pl.BlockSpec((TR // SEQ, 1, SEQ, KV_W), lambda i: (jnp.minimum(i, N_CTX_TILES - 1), layer, 0, 0))
    kv_shape = jax.ShapeDtypeStruct((BATCH, DEPTH, SEQ, KV_W), F32)
    n_in = 8
    return pl.pallas_call(
        functools.partial(_inproj_kernel, len(new_kv)),
        grid=(N_TILES,),
        in_specs=[_CTX_ROWS, _LAT_ROWS,
                  _mod_spec(layer),
                  pl.BlockSpec((1, 1, D_MODEL), lambda i: (layer, 0, 0)),
                  pl.BlockSpec((1, D_MODEL, IN_W), lambda i: (layer, 0, 0), pipeline_mode=pl.Buffered(1)),
                  pl.BlockSpec((TR, HEAD_DIM), lat_blk),
                  pl.BlockSpec((TR, HEAD_DIM), lat_blk),
                  pl.BlockSpec((TR, HEAD_DIM), lat_blk)] + [pl.BlockSpec(memory_space=pl.ANY)] * len(new_kv),
        out_specs=[pl.BlockSpec((TR, ATTN_W), row),
                   pl.BlockSpec((TR, 2 * KV_W), row),
                   pl.BlockSpec((TR, FOURIER_W + 3 * CONV_W), row),
                   kv_blk, kv_blk],
        out_shape=[jax.ShapeDtypeStruct((T_ALL, ATTN_W), BF16),
                   jax.ShapeDtypeStruct((T_ALL, 2 * KV_W), F32),
                   jax.ShapeDtypeStruct((T_ALL, FOURIER_W + 3 * CONV_W), BF16),
                   kv_shape, kv_shape],
        input_output_aliases={n_in + a: 3 + a for a in range(len(new_kv))},
        compiler_params=_cparams(("arbitrary",)),
        name="inproj",
    )(*x, mods, g_mix.reshape(DEPTH, 1, D_MODEL), w_in_bf, *rope, *new_kv)


def _fourier_conv(fm_ref, ccsc_ref, csl_ref, cw_ref, o_ref):
    L = fm_ref.shape[0]
    for g in range(FOURIER_GROUPS):
        xg = fm_ref[:, g * FOURIER_GROUP_W:(g + 1) * FOURIER_GROUP_W]
        z = jnp.dot(xg, ccsc_ref[...], preferred_element_type=F32)
        zz = jnp.concatenate([z[:, :FOURIER_GROUP_W], z[:, FOURIER_GROUP_W:]], axis=0).astype(BF16)
        f = jnp.dot(csl_ref[...], zz, preferred_element_type=F32)
        o_ref[:, ATTN_W + g * FOURIER_GROUP_W:ATTN_W + (g + 1) * FOURIER_GROUP_W] = f.astype(BF16)
    xc = fm_ref[:, FOURIER_W:FOURIER_W + CONV_W].astype(F32)
    bg = fm_ref[:, FOURIER_W + CONV_W:FOURIER_W + 2 * CONV_W].astype(F32)
    cg = fm_ref[:, FOURIER_W + 2 * CONV_W:].astype(F32)
    u = cg * xc
    rows = lax.broadcasted_iota(I32, u.shape, 0)
    prev = jnp.where(rows == 0, 0.0, pltpu.roll(u, 1, 0))
    nxt = jnp.where(rows == L - 1, 0.0, pltpu.roll(u, L - 1, 0))
    y = prev * cw_ref[0, 0:1, :] + u * cw_ref[0, 1:2, :] + nxt * cw_ref[0, 2:3, :]
    o_ref[:, ATTN_W + FOURIER_W:] = (bg * y).astype(BF16)


def _ctx_mix_kernel(layer, sink_ref, q_ref, kv_ref, fm_ref, ccsc_ref, csl_ref, cw_ref, o_ref):
    for h in range(N_HEADS):
        g = h // Q_PER_KV
        q = q_ref[:, h * HEAD_DIM:(h + 1) * HEAD_DIM]
        k = kv_ref[:, g * HEAD_DIM:(g + 1) * HEAD_DIM].astype(BF16)
        v = kv_ref[:, KV_W + g * HEAD_DIM:KV_W + (g + 1) * HEAD_DIM].astype(BF16)
        s = lax.dot_general(q, k, (((1,), (1,)), ((), ())), preferred_element_type=F32) * SCALE
        sink = sink_ref[layer, h]
        m = jnp.maximum(jnp.max(s, axis=-1, keepdims=True), sink)
        e = jnp.exp(s - m)
        den = jnp.sum(e, axis=-1, keepdims=True) + jnp.exp(sink - m)
        o = jnp.dot(e.astype(BF16), v, preferred_element_type=F32) / den
        o_ref[:, h * HEAD_DIM:(h + 1) * HEAD_DIM] = o.astype(BF16)
    _fourier_conv(fm_ref, ccsc_ref, csl_ref, cw_ref, o_ref)


def _ctx_mix(layer, sink, q, kv, fm, ccsc, csl, conv_w):
    row = lambda b: (b, 0)
    const = lambda b: (0, 0)
    return pl.pallas_call(
        functools.partial(_ctx_mix_kernel, layer),
        grid=(BATCH,),
        in_specs=[pl.BlockSpec(memory_space=pltpu.SMEM),
                  pl.BlockSpec((SEQ, ATTN_W), row),
                  pl.BlockSpec((SEQ, 2 * KV_W), row),
                  pl.BlockSpec((SEQ, FOURIER_W + 3 * CONV_W), row),
                  pl.BlockSpec((FOURIER_GROUP_W, 2 * FOURIER_GROUP_W), const),
                  pl.BlockSpec((SEQ, 2 * SEQ), const),
                  pl.BlockSpec((1, 3, CONV_W), lambda b: (layer, 0, 0))],
        out_specs=pl.BlockSpec((SEQ, D_MODEL), row),
        out_shape=jax.ShapeDtypeStruct((T_CTX, D_MODEL), BF16),
        compiler_params=_cparams(("arbitrary",)),
        name="ctx_mix",
    )(sink, q, kv, fm, ccsc, csl, conv_w)


def _lat_mix_kernel(layer, sink_ref, q_ref, kv_ref, fm_ref, ck_ref, cv_ref, ccsc_ref, csl_ref, cw_ref, o_ref):
    n = pl.program_id(1)
    nb = DEC_SEQ // BLOCK
    rows = Q_PER_KV * BLOCK
    band = 3 * BLOCK

    @pl.when(n == 0)
    def _():
        _fourier_conv(fm_ref, ccsc_ref, csl_ref, cw_ref, o_ref)

    q0 = pl.multiple_of(n * BLOCK, BLOCK)
    k0 = pl.multiple_of(jnp.clip(n - 1, 0, nb - 3) * BLOCK, BLOCK)
    qpos = q0 + lax.broadcasted_iota(I32, (rows, band), 0) % BLOCK
    kpos = k0 + lax.broadcasted_iota(I32, (rows, band), 1)
    valid = jnp.abs(qpos - kpos) <= WINDOW
    for g in range(N_KV_HEADS):
        kb = kv_ref[pl.ds(k0, band), g * HEAD_DIM:(g + 1) * HEAD_DIM].astype(BF16)
        vb = kv_ref[pl.ds(k0, band), KV_W + g * HEAD_DIM:KV_W + (g + 1) * HEAD_DIM].astype(BF16)
        k_ctx = ck_ref[0, 0, :, g * HEAD_DIM:(g + 1) * HEAD_DIM].astype(BF16)
        v_ctx = cv_ref[0, 0, :, g * HEAD_DIM:(g + 1) * HEAD_DIM].astype(BF16)
        sink = jnp.concatenate(
            [jnp.full((BLOCK, 1), sink_ref[layer, g * Q_PER_KV + j], F32) for j in range(Q_PER_KV)], axis=0)
        q = jnp.concatenate(
            [q_ref[pl.ds(q0, BLOCK), (g * Q_PER_KV + j) * HEAD_DIM:(g * Q_PER_KV + j + 1) * HEAD_DIM]
             for j in range(Q_PER_KV)], axis=0)
        s_loc = lax.dot_general(q, kb, (((1,), (1,)), ((), ())), preferred_element_type=F32) * SCALE
        s_loc = jnp.where(valid, s_loc, NEG)
        s_ctx = lax.dot_general(q, k_ctx, (((1,), (1,)), ((), ())), preferred_element_type=F32) * SCALE
        m = jnp.maximum(jnp.maximum(jnp.max(s_loc, axis=-1, keepdims=True),
                                    jnp.max(s_ctx, axis=-1, keepdims=True)), sink)
        e_loc, e_ctx = jnp.exp(s_loc - m), jnp.exp(s_ctx - m)
        den = (jnp.sum(e_loc, axis=-1, keepdims=True) + jnp.sum(e_ctx, axis=-1, keepdims=True)
               + jnp.exp(sink - m))
        o = (jnp.dot(e_loc.astype(BF16), vb, preferred_element_type=F32)
             + jnp.dot(e_ctx.astype(BF16), v_ctx, preferred_element_type=F32)) / den
        for j in range(Q_PER_KV):
            h = g * Q_PER_KV + j
            o_ref[pl.ds(q0, BLOCK), h * HEAD_DIM:(h + 1) * HEAD_DIM] = o[j * BLOCK:(j + 1) * BLOCK].astype(BF16)


def _lat_mix(layer, sink, q, kv, fm, cache_k, cache_v, ccsc, csl, conv_w):
    off = T_CTX // DEC_SEQ
    row = lambda b, n: (off + b, 0)
    const = lambda b, n: (0, 0)
    cache = lambda b, n: (b, layer, 0, 0)
    return pl.pallas_call(
        functools.partial(_lat_mix_kernel, layer),
        grid=(DEC_BATCH, DEC_SEQ // BLOCK),
        in_specs=[pl.BlockSpec(memory_space=pltpu.SMEM),
                  pl.BlockSpec((DEC_SEQ, ATTN_W), row),
                  pl.BlockSpec((DEC_SEQ, 2 * KV_W), row),
                  pl.BlockSpec((DEC_SEQ, FOURIER_W + 3 * CONV_W), row),
                  pl.BlockSpec((1, 1, SEQ, KV_W), cache),
                  pl.BlockSpec((1, 1, SEQ, KV_W), cache),
                  pl.BlockSpec((FOURIER_GROUP_W, 2 * FOURIER_GROUP_W), const),
                  pl.BlockSpec((DEC_SEQ, 2 * DEC_SEQ), const, pipeline_mode=pl.Buffered(1)),
                  pl.BlockSpec((1, 3, CONV_W), lambda b, n: (layer, 0, 0))],
        out_specs=pl.BlockSpec((DEC_SEQ, D_MODEL), lambda b, n: (b, 0)),
        out_shape=jax.ShapeDtypeStruct((T_LAT, D_MODEL), BF16),
        compiler_params=_cparams(("arbitrary", "arbitrary")),
        name="lat_mix",
    )(sink, q, kv, fm, cache_k, cache_v, ccsc, csl, conv_w)


def _first_index_of_max(vals, iota, sentinel):
    mx = jnp.max(vals, axis=0, keepdims=True)
    idx = jnp.min(jnp.where(vals == mx, iota, sentinel), axis=0, keepdims=True)
    return mx, idx


def _outproj_kernel(mixc_ref, mixl_ref, xc_ref, xl_ref, mod_ref, g_ref, w_ref, wr_ref, rb_ref, tri_ref,
                    x1_ref, h2_ref, ids_ref, wts_ref, rank_ref, cnt_ref, run_ref):
    i = pl.program_id(0)

    @pl.when(i == 0)
    def _():
        run_ref[...] = jnp.zeros_like(run_ref)

    gate1, shift2, scale2 = mod_ref[0, 0, 2:3, :], mod_ref[0, 0, 3:4, :], mod_ref[0, 0, 4:5, :]
    mix = jnp.where(i < N_CTX_TILES, mixc_ref[...], mixl_ref[...])
    x = jnp.where(i < N_CTX_TILES, xc_ref[...], xl_ref[...])
    x1 = x + gate1 * jnp.dot(mix, w_ref[0], preferred_element_type=F32)
    x1_ref[...] = x1
    h2 = _rmsnorm(x1, g_ref[0]) * (1.0 + scale2) + shift2
    _store_slabs(h2_ref, _pack_halves(h2))

    nt = (((1,), (1,)), ((), ()))
    h_hi = h2.astype(BF16)
    h_lo = (h2 - h_hi.astype(F32)).astype(BF16)
    part = lax.dot_general(wr_ref[0], h_hi, nt, preferred_element_type=F32)
    logits = (part[:N_EXPERTS] + part[N_EXPERTS:]
              + lax.dot_general(wr_ref[0, :N_EXPERTS, :], h_lo, nt, preferred_element_type=F32))
    scores = _sigmoid(logits)
    biased = scores + rb_ref[0]
    ninf = -jnp.inf
    mem = lax.broadcasted_iota(I32, (EXPERTS_PER_GROUP, TR), 0)
    grp_rows = []
    for g in range(N_GROUPS):
        bgp = biased[g * EXPERTS_PER_GROUP:(g + 1) * EXPERTS_PER_GROUP]
        m1, i1 = _first_index_of_max(bgp, mem, EXPERTS_PER_GROUP)
        m2 = jnp.max(jnp.where(mem == i1, ninf, bgp), axis=0, keepdims=True)
        grp_rows.append(m1 + m2)
    cur = jnp.concatenate(grp_rows, axis=0)
    gio = lax.broadcasted_iota(I32, (N_GROUPS, TR), 0)
    gsel = jnp.zeros((N_GROUPS, TR), F32)
    for _ in range(TOPK_GROUPS):
        _, gi = _first_index_of_max(cur, gio, N_GROUPS)
        hit = gio == gi
        gsel = jnp.where(hit, 1.0, gsel)
        cur = jnp.where(hit, ninf, cur)
    cur = jnp.concatenate(
        [jnp.where(gsel[g:g + 1] > 0.0, biased[g * EXPERTS_PER_GROUP:(g + 1) * EXPERTS_PER_GROUP], ninf)
         for g in range(N_GROUPS)], axis=0)
    eio = lax.broadcasted_iota(I32, (N_EXPERTS, TR), 0)
    chosen = jnp.zeros((N_EXPERTS, TR), F32)
    ids, sel = [], []
    for _ in range(TOP_K):
        _, ei = _first_index_of_max(cur, eio, N_EXPERTS)
        hit = eio == ei
        ids.append(ei)
        sel.append(jnp.sum(jnp.where(hit, scores, 0.0), axis=0, keepdims=True))
        chosen = jnp.where(hit, 1.0, chosen)
        cur = jnp.where(hit, ninf, cur)
    total = functools.reduce(lambda a, b: a + b, sel)
    ids_ref[...] = jnp.concatenate(ids, axis=0)
    wts_ref[...] = jnp.concatenate([s / total * ROUTE_SCALE for s in sel], axis=0)

    before = jnp.dot(chosen.astype(BF16), tri_ref[...], preferred_element_type=F32) + run_ref[:, 0:1]
    rank_ref[...] = jnp.concatenate(
        [jnp.sum(jnp.where(eio == ei, before, 0.0), axis=0, keepdims=True) for ei in ids], axis=0).astype(I32)
    run = run_ref[...] + jnp.sum(chosen, axis=1, keepdims=True)
    run_ref[...] = run
    cnt_ref[...] = run


def _outproj(layer, mix_ctx, mix_lat, x, mods, g_ffn, w_out_bf, wr_split, router_bias, tri):
    row = lambda i: (i, 0)
    col = lambda i: (0, i)
    const = lambda i: (0, 0)
    lay = lambda i: (layer, 0, 0)
    return pl.pallas_call(
        _outproj_kernel,
        grid=(N_TILES,),
        in_specs=[_CTX_ROWS, _LAT_ROWS, _CTX_ROWS, _LAT_ROWS,
                  _mod_spec(layer),
                  pl.BlockSpec((1, 1, D_MODEL), lay),
                  pl.BlockSpec((1, D_MODEL, D_MODEL), lay, pipeline_mode=pl.Buffered(1)),
                  pl.BlockSpec((1, 2 * N_EXPERTS, D_MODEL), lay),
                  pl.BlockSpec((1, N_EXPERTS, 1), lay),
                  pl.BlockSpec((TR, TR), const)],
        out_specs=[pl.BlockSpec((TR, D_MODEL), row),
                   pl.BlockSpec((TR * SLAB, LANES), row),
                   pl.BlockSpec((TOP_K, TR), col),
                   pl.BlockSpec((TOP_K, TR), col),
                   pl.BlockSpec((TOP_K, TR), col),
                   pl.BlockSpec((N_EXPERTS, META_LANES), const)],
        out_shape=[jax.ShapeDtypeStruct((T_ALL, D_MODEL), F32),
                   jax.ShapeDtypeStruct((T_ALL * SLAB, LANES), U32),
                   jax.ShapeDtypeStruct((TOP_K, T_ALL), I32),
                   jax.ShapeDtypeStruct((TOP_K, T_ALL), F32),
                   jax.ShapeDtypeStruct((TOP_K, T_ALL), I32),
                   jax.ShapeDtypeStruct((N_EXPERTS, META_LANES), F32)],
        scratch_shapes=[pltpu.VMEM((N_EXPERTS, META_LANES), F32)],
        compiler_params=_cparams(("arbitrary",)),
        name="outproj",
    )(mix_ctx, mix_lat, *x, mods, g_ffn.reshape(DEPTH, 1, D_MODEL), w_out_bf, wr_split,
      router_bias.reshape(DEPTH, N_EXPERTS, 1), tri)


def _positions_kernel(ids_ref, rank_ref, cnt_ref, pos_ref, te_ref, meta_ref):
    ids = ids_ref[...]
    pos = rank_ref[...]
    tile_io = lax.broadcasted_iota(I32, (1, TE_LANES), 1)
    lane_io = lax.broadcasted_iota(I32, (1, META_LANES), 1)
    te = jnp.zeros((1, TE_LANES), I32)
    starts = jnp.zeros((1, META_LANES), I32)
    counts = jnp.zeros((1, META_LANES), I32)
    start = jnp.zeros((1, 1), I32)
    for e in range(N_EXPERTS):
        c = cnt_ref[e:e + 1, 0:1].astype(I32)
        pos = pos + jnp.where(ids == e, start, 0)
        starts = starts + jnp.where(lane_io == e, start, 0)
        counts = counts + jnp.where(lane_io == e, c, 0)
        start = start + ((c + (TM - 1)) // TM) * TM
        te = te + jnp.where(tile_io * TM >= start, 1, 0)
    pos_ref[...] = pos
    te_ref[...] = jnp.minimum(te, N_EXPERTS - 1)
    meta_ref[0:1, :] = starts
    meta_ref[1:2, :] = counts
    meta_ref[2:3, :] = jnp.broadcast_to(start // TM, (1, META_LANES))
    meta_ref[3:8, :] = jnp.zeros((5, META_LANES), I32)


def _positions(ids, rank, cnt):
    return pl.pallas_call(
        _positions_kernel,
        out_shape=[jax.ShapeDtypeStruct((TOP_K, T_ALL), I32),
                   jax.ShapeDtypeStruct((1, TE_LANES), I32),
                   jax.ShapeDtypeStruct((8, META_LANES), I32)],
        compiler_params=_cparams(),
        name="positions",
    )(ids, rank, cnt)


SC_CORES, SC_SUBCORES = 2, 16
SC_WORKERS = SC_CORES * SC_SUBCORES
SC_CHUNK = 32


def _sc_kernel(body, out_type, n_idx_rows, name):
    return pl.kernel(
        body,
        out_type=out_type,
        mesh=plsc.VectorSubcoreMesh(core_axis_name="c", subcore_axis_name="s",
                                    num_cores=SC_CORES, num_subcores=SC_SUBCORES),
        scratch_types=[pltpu.VMEM((n_idx_rows, SC_CHUNK), I32),
                       pltpu.VMEM((2, SC_CHUNK, SLAB, LANES), U32),
                       pltpu.SemaphoreType.DMA,
                       pltpu.SemaphoreType.DMA((2,)),
                       pltpu.SemaphoreType.DMA((2,))],
        compiler_params=pltpu.CompilerParams(use_tc_tiling_on_sc=True),
        name=name,
    )


def _sc_worker():
    return lax.axis_index("s") * SC_CORES + lax.axis_index("c")


def _sc_load_indices(n_rows, src_of_row, idx_v, sem):
    @pl.loop(0, n_rows)
    def _(r):
        src, dst = src_of_row(r), idx_v.at[r]
        pltpu.make_async_copy(src, dst, sem).start()

    @pl.loop(0, n_rows)
    def _(r):
        src, dst = src_of_row(r), idx_v.at[r]
        pltpu.make_async_copy(src, dst, sem).wait()


def _sc_two_slot_pipeline(n_chunks, fill, fill_wait, drain, drain_wait):
    assert n_chunks % 2 == 0 and n_chunks >= 4
    fill(0, 0)

    @pl.loop(0, n_chunks // 2)
    def _(pair):
        c = 2 * pair

        @pl.when(pair >= 1)
        def _():
            drain_wait(c - 1, 1)

        fill(c + 1, 1)
        fill_wait(c, 0)
        drain(c, 0)

        @pl.when(pair < n_chunks // 2 - 1)
        def _():
            drain_wait(c, 0)
            fill(c + 2, 0)

        fill_wait(c + 1, 1)
        drain(c + 1, 1)

    drain_wait(n_chunks - 2, 0)
    drain_wait(n_chunks - 1, 1)


def _dispatch(pos_flat, h3):
    n_chunks = T_ALL // SC_WORKERS // SC_CHUNK

    def body(pos_hbm, h_hbm, xs_hbm, idx_v, rows_v, idx_sem, fill_sem, drain_sem):
        t_base = _sc_worker() * (n_chunks * SC_CHUNK)

        def index_src(r):
            t0 = t_base + (r // TOP_K) * SC_CHUNK
            return pos_hbm.at[pl.ds(pl.multiple_of((r % TOP_K) * T_ALL + t0, SC_CHUNK), SC_CHUNK)]

        _sc_load_indices(n_chunks * TOP_K, index_src, idx_v, idx_sem)

        def load(c, slot):
            t0 = pl.multiple_of(t_base + c * SC_CHUNK, SC_CHUNK)
            return pltpu.make_async_copy(h_hbm.at[pl.ds(t0, SC_CHUNK)], rows_v.at[slot], fill_sem.at[slot])

        def scatter(c, slot, k):
            return pltpu.make_async_copy(rows_v.at[slot], xs_hbm.at[idx_v.at[c * TOP_K + k]], drain_sem.at[slot])

        def drain(c, slot):
            for k in range(TOP_K):
                scatter(c, slot, k).start()

        def drain_wait(c, slot):
            for k in range(TOP_K):
                scatter(c, slot, k).wait()

        _sc_two_slot_pipeline(n_chunks, lambda c, s: load(c, s).start(), lambda c, s: load(c, s).wait(),
                              drain, drain_wait)

    return _sc_kernel(body, jax.ShapeDtypeStruct((P_SORT, SLAB, LANES), U32), n_chunks * TOP_K,
                      "sc_dispatch")(pos_flat, h3)


def _gather(pos_flat, y3):
    n_items = pos_flat.shape[0]
    n_chunks = n_items // SC_WORKERS // SC_CHUNK

    def body(pos_hbm, y_hbm, out_hbm, idx_v, rows_v, idx_sem, fill_sem, drain_sem):
        base = _sc_worker() * (n_chunks * SC_CHUNK)

        def index_src(r):
            return pos_hbm.at[pl.ds(pl.multiple_of(base + r * SC_CHUNK, SC_CHUNK), SC_CHUNK)]

        _sc_load_indices(n_chunks, index_src, idx_v, idx_sem)

        def gather(c, slot):
            return pltpu.make_async_copy(y_hbm.at[idx_v.at[c]], rows_v.at[slot], fill_sem.at[slot])

        def store(c, slot):
            i0 = pl.multiple_of(base + c * SC_CHUNK, SC_CHUNK)
            return pltpu.make_async_copy(rows_v.at[slot], out_hbm.at[pl.ds(i0, SC_CHUNK)], drain_sem.at[slot])

        _sc_two_slot_pipeline(n_chunks, lambda c, s: gather(c, s).start(), lambda c, s: gather(c, s).wait(),
                              lambda c, s: store(c, s).start(), lambda c, s: store(c, s).wait())

    return _sc_kernel(body, jax.ShapeDtypeStruct((n_items, SLAB, LANES), U32), n_chunks,
                      "sc_gather")(pos_flat, y3)


def _experts_kernel(te_ref, nt_ref, end_ref, x_ref, wg_ref, wu_ref, wd_ref, y_ref, wg_bf, wu_bf, wd_bf):
    j = pl.program_id(0)
    used = j < nt_ref[0]
    prev = te_ref[jnp.maximum(j - 1, 0)]

    @pl.when(used & ((j == 0) | (te_ref[j] != prev)))
    def _():
        wg_bf[...] = wg_ref[0, 0].astype(BF16)
        wu_bf[...] = wu_ref[0, 0].astype(BF16)
        wd_bf[...] = wd_ref[0, 0].astype(BF16)

    @pl.when(used)
    def _():
        valid = end_ref[te_ref[j]] - j * TM
        rows = lax.broadcasted_iota(I32, (TM, HALF), 0)
        words = jnp.where(rows < valid, _load_slabs(x_ref, TM), jnp.uint32(0))
        xa, xb = _unpack_halves(words)
        xa, xb = xa.astype(BF16), xb.astype(BF16)
        g = (jnp.dot(xa, wg_bf[:HALF], preferred_element_type=F32)
             + jnp.dot(xb, wg_bf[HALF:], preferred_element_type=F32))
        u = (jnp.dot(xa, wu_bf[:HALF], preferred_element_type=F32)
             + jnp.dot(xb, wu_bf[HALF:], preferred_element_type=F32))
        a = (_silu(g) * u).astype(BF16)
        _store_slabs(y_ref, _pack_halves(jnp.dot(a, wd_bf[...], preferred_element_type=F32)))


def _experts(layer, te, nt, end, xs, w_gate, w_up, w_down):
    n_tiles = xs.shape[0] // (TM * SLAB)
    tile = lambda j, te, nt, end: (jnp.minimum(j, nt[0] - 1), 0)
    wsel = lambda j, te, nt, end: (layer, te[jnp.minimum(j, nt[0] - 1)], 0, 0)
    return pl.pallas_call(
        _experts_kernel,
        grid_spec=pltpu.PrefetchScalarGridSpec(
            num_scalar_prefetch=3,
            grid=(n_tiles,),
            in_specs=[pl.BlockSpec((TM * SLAB, LANES), tile),
                      pl.BlockSpec((1, 1, D_MODEL, EXPERT_FF), wsel),
                      pl.BlockSpec((1, 1, D_MODEL, EXPERT_FF), wsel),
                      pl.BlockSpec((1, 1, EXPERT_FF, D_MODEL), wsel)],
            out_specs=pl.BlockSpec((TM * SLAB, LANES), tile),
            scratch_shapes=[pltpu.VMEM((D_MODEL, EXPERT_FF), BF16),
                            pltpu.VMEM((D_MODEL, EXPERT_FF), BF16),
                            pltpu.VMEM((EXPERT_FF, D_MODEL), BF16)]),
        out_shape=jax.ShapeDtypeStruct(xs.shape, U32),
        compiler_params=_cparams(("arbitrary",)),
        name="experts",
    )(te, nt, end, xs, w_gate, w_up, w_down)


def _combine_kernel(final, wts_ref, x1_ref, ysh_ref, mod_ref, g_ref, buf_ref, *rest):
    o_ref, x2_ref, wb_ref = rest
    w = jnp.transpose(wts_ref[...])
    for k in range(TOP_K):
        wb_ref[k] = jnp.broadcast_to(w[:, k:k + 1], (TC_ROWS, LANES))
    sq = jnp.zeros((TC_ROWS, LANES), F32)
    for s in range(SLAB):
        acc_a, acc_b = _unpack_halves(ysh_ref[pl.ds(s, TC_ROWS, stride=SLAB), :])
        for k in range(TOP_K):
            ya, yb = _unpack_halves(buf_ref[k, pl.ds(s, TC_ROWS, stride=SLAB), :])
            acc_a = acc_a + wb_ref[k] * ya
            acc_b = acc_b + wb_ref[k] * yb
        for half, acc in ((0, acc_a), (1, acc_b)):
            cols = slice(half * HALF + s * LANES, half * HALF + (s + 1) * LANES)
            x2 = x1_ref[:, cols] + mod_ref[0, 0, 5:6, cols] * acc
            x2_ref[:, cols] = x2
            if final:
                sq = sq + x2 * x2
    out = x2_ref[...]
    if final:
        ms = jnp.sum(sq, axis=-1, keepdims=True) * (1.0 / D_MODEL)
        out = out * lax.rsqrt(ms + EPS) * g_ref[...]

    o_ref[...] = out


def _combine(layer, row0, n_rows, wts, x1, ysh, mods, final_norm, yg):
    final = layer == DEPTH - 1
    t0 = row0 // TC_ROWS
    row = lambda i: (t0 + i, 0)
    return pl.pallas_call(
        functools.partial(_combine_kernel, final),
        grid=(n_rows // TC_ROWS,),
        in_specs=[pl.BlockSpec((TOP_K, TC_ROWS), lambda i: (0, t0 + i)),
                  pl.BlockSpec((TC_ROWS, D_MODEL), row),
                  pl.BlockSpec((TC_ROWS * SLAB, LANES), row),
                  pl.BlockSpec((1, 1, N_MOD, D_MODEL),
                               lambda i: (layer, _cond_row((t0 + i) // (TR // TC_ROWS)), 0, 0)),
                  pl.BlockSpec((1, D_MODEL), lambda i: (0, 0)),
                  pl.BlockSpec((TOP_K, TC_ROWS * SLAB, LANES), lambda i: (0, i, 0))],
        out_specs=pl.BlockSpec((TC_ROWS, D_MODEL), lambda i: (i, 0)),
        out_shape=jax.ShapeDtypeStruct((n_rows, D_MODEL), F32),
        scratch_shapes=[pltpu.VMEM((TC_ROWS, D_MODEL), F32),
                        pltpu.VMEM((TOP_K, TC_ROWS, LANES), F32)],
        compiler_params=_cparams(("arbitrary",)),
        name="combine",
    )(wts, x1, ysh, mods, final_norm.reshape(1, D_MODEL), yg)


def kernel(x_prompt, x_sample, cache_k, cache_v, c, c_ctx, w_mod, b_mod, g_mix, g_ffn, w_in, w_out, conv_w,
           attn_sink, w_router, router_bias, w_gate_e, w_up_e, w_down_e, w_gate_s, w_up_s, w_down_s, final_norm):
    ccsc = jnp.asarray(_channel_dft(), BF16)
    csl_ctx = jnp.asarray(_dft_tables(SEQ), BF16)
    csl_lat = jnp.asarray(_dft_tables(DEC_SEQ), BF16)
    rope = tuple(jnp.asarray(t, F32) for t in _rope_tables(DEC_SEQ))
    tri = jnp.asarray(np.triu(np.ones((TR, TR)), 1), BF16)
    shared_te = jnp.zeros((T_ALL // TM,), I32)
    shared_nt = jnp.full((1,), T_ALL // TM, I32)
    shared_end = jnp.full((1,), T_ALL, I32)

    cond = jnp.concatenate([c_ctx[None], c, jnp.zeros((N_COND - 1 - DEC_BATCH, D_MODEL), F32)], axis=0)
    mods = _mods(cond, w_mod, b_mod).reshape(DEPTH, N_COND, N_MOD, D_MODEL)
    cache_k = cache_k.reshape(DEC_BATCH, DEPTH, SEQ, KV_W)
    cache_v = cache_v.reshape(DEC_BATCH, DEPTH, SEQ, KV_W)
    w_in_bf, w_out_bf = w_in.astype(BF16), w_out.astype(BF16)
    wr_t = jnp.swapaxes(w_router, 1, 2)
    wr_hi = wr_t.astype(BF16)
    wr_split = jnp.concatenate([wr_hi, (wr_t - wr_hi.astype(F32)).astype(BF16)], axis=1)
    ws_gate, ws_up, ws_down = w_gate_s[:, None], w_up_s[:, None], w_down_s[:, None]
    x = (x_prompt.reshape(T_CTX, D_MODEL), x_sample.reshape(T_LAT, D_MODEL))

    new_kv = ()
    for l in range(DEPTH):
        q, kv, fm, *new_kv = _inproj(l, x, mods, g_mix, w_in_bf, rope, tuple(new_kv))
        mix_ctx = _ctx_mix(l, attn_sink, q, kv, fm, ccsc, csl_ctx, conv_w)
        mix_lat = _lat_mix(l, attn_sink, q, kv, fm, cache_k, cache_v, ccsc, csl_lat, conv_w)
        x1, h2, ids, wts, rank, cnt = _outproj(l, mix_ctx, mix_lat, x, mods, g_ffn, w_out_bf, wr_split,
                                               router_bias, tri)
        pos, te, meta = _positions(ids, rank, cnt)
        pos_flat = pos.reshape(TOP_K * T_ALL)
        xs = _dispatch(pos_flat, h2.reshape(T_ALL, SLAB, LANES)).reshape(P_SORT * SLAB, LANES)
        y = _experts(l, te[0, :N_SORT_TILES], meta[2, :1], meta[0, :N_EXPERTS] + meta[1, :N_EXPERTS], xs,
                     w_gate_e, w_up_e, w_down_e)
        ysh = _experts(l, shared_te, shared_nt, shared_end, h2, ws_gate, ws_up, ws_down)
        y3 = y.reshape(P_SORT, SLAB, LANES)
        x = []
        for row0, n_rows in ((0, T_CTX), (T_CTX, T_LAT)):
            part_pos = pos[:, row0:row0 + n_rows].reshape(TOP_K * n_rows)
            yg = _gather(part_pos, y3).reshape(TOP_K, n_rows * SLAB, LANES)
            x.append(_combine(l, row0, n_rows, wts, x1, ysh, mods, final_norm, yg))

    y_prompt = x[0].reshape(BATCH, SEQ, D_MODEL)
    y_sample = x[1].reshape(DEC_BATCH, DEC_SEQ, D_MODEL)
    new_k, new_v = (a.reshape(BATCH, DEPTH, SEQ, N_KV_HEADS, HEAD_DIM) for a in new_kv)
    return y_prompt, y_sample, new_k, new_v
```

```python
import functools

import numpy as np
import jax
import jax.numpy as jnp
from jax import lax
from jax.experimental import pallas as pl
from jax.experimental.pallas import tpu as pltpu
from jax.experimental.pallas import tpu_sc as plsc

F32 = jnp.float32
BF16 = jnp.bfloat16
I32 = jnp.int32
U32 = jnp.uint32

D_MODEL = 2048
HALF = D_MODEL // 2
BATCH, SEQ = 32, 256
DEC_BATCH, DEC_SEQ = 4, 1024
DEPTH = 2
GRID_W = 64
HEAD_DIM = 128
N_HEADS, N_KV_HEADS, Q_PER_KV = 8, 2, 4
ATTN_W, KV_W = 1024, 256
WINDOW, BLOCK = 128, 128
SCALE = HEAD_DIM ** -0.5
ROPE_BASE = 10000.0
ROPE_PAIRS = 32
FOURIER_W, FOURIER_GROUPS, FOURIER_GROUP_W = 512, 4, 128
CONV_W = 512
IN_W = 3584
N_EXPERTS, TOP_K, N_GROUPS, EXPERTS_PER_GROUP, TOPK_GROUPS = 64, 8, 8, 8, 4
EXPERT_FF = 512
ROUTE_SCALE = 2.5
N_MOD = 6
EPS = 1e-6
NEG = -1e30

T_CTX = BATCH * SEQ
T_LAT = DEC_BATCH * DEC_SEQ
T_ALL = T_CTX + T_LAT
TR = 512
N_CTX_TILES = T_CTX // TR
LAT_TILES_PER_SEQ = DEC_SEQ // TR
N_TILES = T_ALL // TR
N_COND = 8
MOD_TN = 1024
TM = 512
TC_ROWS = 128
N_SORT_TILES = T_ALL * TOP_K // TM + N_EXPERTS
P_SORT = N_SORT_TILES * TM
META_LANES = 128
TE_LANES = 512
VMEM_LIMIT = 56 * 1024 * 1024


def _cparams(sem=None):
    return pltpu.CompilerParams(dimension_semantics=sem, vmem_limit_bytes=VMEM_LIMIT)


def _silu(x):
    return x / (1.0 + jnp.exp(-x))


def _sigmoid(x):
    return 1.0 / (1.0 + jnp.exp(-x))


def _rmsnorm(x, g):
    return x * lax.rsqrt(jnp.mean(x * x, axis=-1, keepdims=True) + EPS) * g


def _cond_row(i):
    return jnp.where(i < N_CTX_TILES, 0, 1 + (i - N_CTX_TILES) // LAT_TILES_PER_SEQ)


def _pack_halves(v):
    n = v.shape[1] // 2
    hi = pltpu.bitcast(v[:, :n].astype(BF16).astype(F32), U32)
    lo = pltpu.bitcast(v[:, n:].astype(BF16).astype(F32), U32)
    return hi | (lo >> 16)


def _unpack_halves(w):
    return pltpu.bitcast(w & jnp.uint32(0xFFFF0000), F32), pltpu.bitcast(w << 16, F32)


SLAB, LANES = 8, 128


def _load_slabs(ref, n, row0=0):
    return jnp.concatenate([ref[pl.ds(row0 * SLAB + s, n, stride=SLAB), :] for s in range(SLAB)], axis=1)


def _store_slabs(ref, words, row0=0):
    n = words.shape[0]
    for s in range(SLAB):
        ref[pl.ds(row0 * SLAB + s, n, stride=SLAB), :] = words[:, s * LANES:(s + 1) * LANES]


def _dft_tables(L):
    j = np.arange(L, dtype=np.int64)
    ang = 2.0 * np.pi * ((j[:, None] * j[None, :]) % L).astype(np.float64) / L
    c, s = np.cos(ang) / np.sqrt(L), np.sin(ang) / np.sqrt(L)
    return np.concatenate([c, -s], axis=1)


def _channel_dft():
    n = FOURIER_GROUP_W
    j = np.arange(n, dtype=np.int64)
    ang = 2.0 * np.pi * ((j[:, None] * j[None, :]) % n).astype(np.float64) / n
    return np.concatenate([np.cos(ang), np.sin(ang)], axis=1) / np.sqrt(n)


def _rope_tables(S):
    pos = np.arange(S)
    row, col = (pos // GRID_W).astype(np.float64), (pos % GRID_W).astype(np.float64)
    inv = ROPE_BASE ** (-np.arange(ROPE_PAIRS, dtype=np.float64) / ROPE_PAIRS)
    ar, ac = row[:, None] * inv, col[:, None] * inv
    z = np.zeros_like(ar)
    cos = np.concatenate([np.cos(ar), np.cos(ar), np.cos(ac), np.cos(ac)], axis=1)
    sin_hi = np.concatenate([-np.sin(ar), z, -np.sin(ac), z], axis=1)
    sin_lo = np.concatenate([z, np.sin(ar), z, np.sin(ac)], axis=1)
    return cos, sin_hi, sin_lo


def _mods_kernel(cond_ref, w_ref, b_ref, o_ref):
    s = _silu(cond_ref[...]).astype(BF16)
    o_ref[0] = jnp.dot(s, w_ref[0].astype(BF16), preferred_element_type=F32) + b_ref[0]


def _mods(cond, w_mod, b_mod):
    n = N_MOD * D_MODEL
    return pl.pallas_call(
        _mods_kernel,
        grid=(DEPTH, n // MOD_TN),
        in_specs=[pl.BlockSpec((N_COND, D_MODEL), lambda l, j: (0, 0)),
                  pl.BlockSpec((1, D_MODEL, MOD_TN), lambda l, j: (l, 0, j)),
                  pl.BlockSpec((1, 1, MOD_TN), lambda l, j: (l, 0, j))],
        out_specs=pl.BlockSpec((1, N_COND, MOD_TN), lambda l, j: (l, 0, j)),
        out_shape=jax.ShapeDtypeStruct((DEPTH, N_COND, n), F32),
        compiler_params=_cparams(("arbitrary", "arbitrary")),
        name="mods",
    )(cond, w_mod, b_mod.reshape(DEPTH, 1, n))


_CTX_ROWS = pl.BlockSpec((TR, D_MODEL), lambda i: (jnp.minimum(i, N_CTX_TILES - 1), 0))
_LAT_ROWS = pl.BlockSpec((TR, D_MODEL), lambda i: (jnp.maximum(i - N_CTX_TILES, 0), 0))


def _mod_spec(layer):
    return pl.BlockSpec((1, 1, N_MOD, D_MODEL), lambda i: (layer, _cond_row(i), 0, 0))


def _inproj_kernel(n_alias, xc_ref, xl_ref, mod_ref, g_ref, w_ref, cos_ref, shi_ref, slo_ref, *rest):
    q_ref, kv_ref, fm_ref, nk_ref, nv_ref = rest[n_alias:]
    i = pl.program_id(0)
    shift, scale = mod_ref[0, 0, 0:1, :], mod_ref[0, 0, 1:2, :]
    x = jnp.where(i < N_CTX_TILES, xc_ref[...], xl_ref[...])
    h = (_rmsnorm(x, g_ref[0]) * (1.0 + scale) + shift).astype(BF16)
    fm_ref[...] = jnp.dot(h, w_ref[0, :, ATTN_W + 2 * KV_W:], preferred_element_type=F32).astype(BF16)
    qkv = jnp.dot(h, w_ref[0, :, :ATTN_W + 2 * KV_W], preferred_element_type=F32)
    kv_ref[:, KV_W:] = qkv[:, ATTN_W + KV_W:]

    @pl.when(i < N_CTX_TILES)
    def _():
        q_ref[...] = qkv[:, :ATTN_W].astype(BF16)
        kv_ref[:, :KV_W] = qkv[:, ATTN_W:ATTN_W + KV_W]
        for b in range(TR // SEQ):
            for hd in range(N_KV_HEADS):
                c0 = ATTN_W + hd * HEAD_DIM
                nk_ref[b, 0, :, hd, :] = qkv[b * SEQ:(b + 1) * SEQ, c0:c0 + HEAD_DIM]
                nv_ref[b, 0, :, hd, :] = qkv[b * SEQ:(b + 1) * SEQ, c0 + KV_W:c0 + KV_W + HEAD_DIM]

    @pl.when(i >= N_CTX_TILES)
    def _():
        cos, shi, slo = cos_ref[...], shi_ref[...], slo_ref[...]
        for hd in range(N_HEADS + N_KV_HEADS):
            xh = qkv[:, hd * HEAD_DIM:(hd + 1) * HEAD_DIM]
            r = xh * cos + pltpu.roll(xh, HEAD_DIM - 32, 1) * shi + pltpu.roll(xh, 32, 1) * slo
            if hd < N_HEADS:
                q_ref[:, hd * HEAD_DIM:(hd + 1) * HEAD_DIM] = r.astype(BF16)
            else:
                kv_ref[:, (hd - N_HEADS) * HEAD_DIM:(hd - N_HEADS + 1) * HEAD_DIM] = r


def _inproj(layer, x, mods, g_mix, w_in_bf, rope, new_kv):
    lat_blk = lambda i: (jnp.maximum(i - N_CTX_TILES, 0) % LAT_TILES_PER_SEQ, 0)
    row = lambda i: (i, 0)
    kv_blk = pl.BlockSpec((TR // SEQ, 1, SEQ, N_KV_HEADS, HEAD_DIM),
                          lambda i: (jnp.minimum(i, N_CTX_TILES - 1), layer, 0, 0, 0))
    kv_shape = jax.ShapeDtypeStruct((BATCH, DEPTH, SEQ, N_KV_HEADS, HEAD_DIM), F32)
    n_in = 8
    return pl.pallas_call(
        functools.partial(_inproj_kernel, len(new_kv)),
        grid=(N_TILES,),
        in_specs=[_CTX_ROWS, _LAT_ROWS,
                  _mod_spec(layer),
                  pl.BlockSpec((1, 1, D_MODEL), lambda i: (layer, 0, 0)),
                  pl.BlockSpec((1, D_MODEL, IN_W), lambda i: (layer, 0, 0), pipeline_mode=pl.Buffered(1)),
                  pl.BlockSpec((TR, HEAD_DIM), lat_blk),
                  pl.BlockSpec((TR, HEAD_DIM), lat_blk),
                  pl.BlockSpec((TR, HEAD_DIM), lat_blk)] + [pl.BlockSpec(memory_space=pl.ANY)] * len(new_kv),
        out_specs=[pl.BlockSpec((TR, ATTN_W), row),
                   pl.BlockSpec((TR, 2 * KV_W), row),
                   pl.BlockSpec((TR, FOURIER_W + 3 * CONV_W), row),
                   kv_blk, kv_blk],
        out_shape=[jax.ShapeDtypeStruct((T_ALL, ATTN_W), BF16),
                   jax.ShapeDtypeStruct((T_ALL, 2 * KV_W), F32),
                   jax.ShapeDtypeStruct((T_ALL, FOURIER_W + 3 * CONV_W), BF16),
                   kv_shape, kv_shape],
        input_output_aliases={n_in + a: 3 + a for a in range(len(new_kv))},
        compiler_params=_cparams(("arbitrary",)),
        name="inproj",
    )(*x, mods, g_mix.reshape(DEPTH, 1, D_MODEL), w_in_bf, *rope, *new_kv)


def _fourier_conv(fm_ref, ccsc_ref, csl_ref, cw_ref, o_ref):
    L = fm_ref.shape[0]
    for g in range(FOURIER_GROUPS):
        xg = fm_ref[:, g * FOURIER_GROUP_W:(g + 1) * FOURIER_GROUP_W]
        z = jnp.dot(xg, ccsc_ref[...], preferred_element_type=F32)
        zz = jnp.concatenate([z[:, :FOURIER_GROUP_W], z[:, FOURIER_GROUP_W:]], axis=0).astype(BF16)
        f = jnp.dot(csl_ref[...], zz, preferred_element_type=F32)
        o_ref[:, ATTN_W + g * FOURIER_GROUP_W:ATTN_W + (g + 1) * FOURIER_GROUP_W] = f.astype(BF16)
    xc = fm_ref[:, FOURIER_W:FOURIER_W + CONV_W].astype(F32)
    bg = fm_ref[:, FOURIER_W + CONV_W:FOURIER_W + 2 * CONV_W].astype(F32)
    cg = fm_ref[:, FOURIER_W + 2 * CONV_W:].astype(F32)
    u = cg * xc
    rows = lax.broadcasted_iota(I32, u.shape, 0)
    prev = jnp.where(rows == 0, 0.0, pltpu.roll(u, 1, 0))
    nxt = jnp.where(rows == L - 1, 0.0, pltpu.roll(u, L - 1, 0))
    y = prev * cw_ref[0, 0:1, :] + u * cw_ref[0, 1:2, :] + nxt * cw_ref[0, 2:3, :]
    o_ref[:, ATTN_W + FOURIER_W:] = (bg * y).astype(BF16)


def _ctx_mix_kernel(layer, sink_ref, q_ref, kv_ref, fm_ref, ccsc_ref, csl_ref, cw_ref, o_ref):
    for h in range(N_HEADS):
        g = h // Q_PER_KV
        q = q_ref[:, h * HEAD_DIM:(h + 1) * HEAD_DIM]
        k = kv_ref[:, g * HEAD_DIM:(g + 1) * HEAD_DIM].astype(BF16)
        v = kv_ref[:, KV_W + g * HEAD_DIM:KV_W + (g + 1) * HEAD_DIM].astype(BF16)
        s = lax.dot_general(q, k, (((1,), (1,)), ((), ())), preferred_element_type=F32) * SCALE
        sink = sink_ref[layer, h]
        m = jnp.maximum(jnp.max(s, axis=-1, keepdims=True), sink)
        e = jnp.exp(s - m)
        den = jnp.sum(e, axis=-1, keepdims=True) + jnp.exp(sink - m)
        o = jnp.dot(e.astype(BF16), v, preferred_element_type=F32) / den
        o_ref[:, h * HEAD_DIM:(h + 1) * HEAD_DIM] = o.astype(BF16)
    _fourier_conv(fm_ref, ccsc_ref, csl_ref, cw_ref, o_ref)


def _ctx_mix(layer, sink, q, kv, fm, ccsc, csl, conv_w):
    row = lambda b: (b, 0)
    const = lambda b: (0, 0)
    return pl.pallas_call(
        functools.partial(_ctx_mix_kernel, layer),
        grid=(BATCH,),
        in_specs=[pl.BlockSpec(memory_space=pltpu.SMEM),
                  pl.BlockSpec((SEQ, ATTN_W), row),
                  pl.BlockSpec((SEQ, 2 * KV_W), row),
                  pl.BlockSpec((SEQ, FOURIER_W + 3 * CONV_W), row),
                  pl.BlockSpec((FOURIER_GROUP_W, 2 * FOURIER_GROUP_W), const),
                  pl.BlockSpec((SEQ, 2 * SEQ), const),
                  pl.BlockSpec((1, 3, CONV_W), lambda b: (layer, 0, 0))],
        out_specs=pl.BlockSpec((SEQ, D_MODEL), row),
        out_shape=jax.ShapeDtypeStruct((T_CTX, D_MODEL), BF16),
        compiler_params=_cparams(("arbitrary",)),
        name="ctx_mix",
    )(sink, q, kv, fm, ccsc, csl, conv_w)


def _lat_mix_kernel(layer, sink_ref, q_ref, kv_ref, fm_ref, ck_ref, cv_ref, ccsc_ref, csl_ref, cw_ref, o_ref):
    n = pl.program_id(1)
    nb = DEC_SEQ // BLOCK
    rows = Q_PER_KV * BLOCK
    band = 3 * BLOCK

    @pl.when(n == 0)
    def _():
        _fourier_conv(fm_ref, ccsc_ref, csl_ref, cw_ref, o_ref)

    q0 = pl.multiple_of(n * BLOCK, BLOCK)
    k0 = pl.multiple_of(jnp.clip(n - 1, 0, nb - 3) * BLOCK, BLOCK)
    qpos = q0 + lax.broadcasted_iota(I32, (rows, band), 0) % BLOCK
    kpos = k0 + lax.broadcasted_iota(I32, (rows, band), 1)
    valid = jnp.abs(qpos - kpos) <= WINDOW
    for g in range(N_KV_HEADS):
        kb = kv_ref[pl.ds(k0, band), g * HEAD_DIM:(g + 1) * HEAD_DIM].astype(BF16)
        vb = kv_ref[pl.ds(k0, band), KV_W + g * HEAD_DIM:KV_W + (g + 1) * HEAD_DIM].astype(BF16)
        k_ctx = ck_ref[0, 0, :, g * HEAD_DIM:(g + 1) * HEAD_DIM].astype(BF16)
        v_ctx = cv_ref[0, 0, :, g * HEAD_DIM:(g + 1) * HEAD_DIM].astype(BF16)
        sink = jnp.concatenate(
            [jnp.full((BLOCK, 1), sink_ref[layer, g * Q_PER_KV + j], F32) for j in range(Q_PER_KV)], axis=0)
        q = jnp.concatenate(
            [q_ref[pl.ds(q0, BLOCK), (g * Q_PER_KV + j) * HEAD_DIM:(g * Q_PER_KV + j + 1) * HEAD_DIM]
             for j in range(Q_PER_KV)], axis=0)
        s_loc = lax.dot_general(q, kb, (((1,), (1,)), ((), ())), preferred_element_type=F32) * SCALE
        s_loc = jnp.where(valid, s_loc, NEG)
        s_ctx = lax.dot_general(q, k_ctx, (((1,), (1,)), ((), ())), preferred_element_type=F32) * SCALE
        m = jnp.maximum(jnp.maximum(jnp.max(s_loc, axis=-1, keepdims=True),
                                    jnp.max(s_ctx, axis=-1, keepdims=True)), sink)
        e_loc, e_ctx = jnp.exp(s_loc - m), jnp.exp(s_ctx - m)
        den = (jnp.sum(e_loc, axis=-1, keepdims=True) + jnp.sum(e_ctx, axis=-1, keepdims=True)
               + jnp.exp(sink - m))
        o = (jnp.dot(e_loc.astype(BF16), vb, preferred_element_type=F32)
             + jnp.dot(e_ctx.astype(BF16), v_ctx, preferred_element_type=F32)) / den
        for j in range(Q_PER_KV):
            h = g * Q_PER_KV + j
            o_ref[pl.ds(q0, BLOCK), h * HEAD_DIM:(h + 1) * HEAD_DIM] = o[j * BLOCK:(j + 1) * BLOCK].astype(BF16)


def _lat_mix(layer, sink, q, kv, fm, cache_k, cache_v, ccsc, csl, conv_w):
    off = T_CTX // DEC_SEQ
    row = lambda b, n: (off + b, 0)
    const = lambda b, n: (0, 0)
    cache = lambda b, n: (b, layer, 0, 0)
    return pl.pallas_call(
        functools.partial(_lat_mix_kernel, layer),
        grid=(DEC_BATCH, DEC_SEQ // BLOCK),
        in_specs=[pl.BlockSpec(memory_space=pltpu.SMEM),
                  pl.BlockSpec((DEC_SEQ, ATTN_W), row),
                  pl.BlockSpec((DEC_SEQ, 2 * KV_W), row),
                  pl.BlockSpec((DEC_SEQ, FOURIER_W + 3 * CONV_W), row),
                  pl.BlockSpec((1, 1, SEQ, KV_W), cache),
                  pl.BlockSpec((1, 1, SEQ, KV_W), cache),
                  pl.BlockSpec((FOURIER_GROUP_W, 2 * FOURIER_GROUP_W), const),
                  pl.BlockSpec((DEC_SEQ, 2 * DEC_SEQ), const, pipeline_mode=pl.Buffered(1)),
                  pl.BlockSpec((1, 3, CONV_W), lambda b, n: (layer, 0, 0))],
        out_specs=pl.BlockSpec((DEC_SEQ, D_MODEL), lambda b, n: (b, 0)),
        out_shape=jax.ShapeDtypeStruct((T_LAT, D_MODEL), BF16),
        compiler_params=_cparams(("arbitrary", "arbitrary")),
        name="lat_mix",
    )(sink, q, kv, fm, cache_k, cache_v, ccsc, csl, conv_w)


def _first_index_of_max(vals, iota, sentinel):
    mx = jnp.max(vals, axis=0, keepdims=True)
    idx = jnp.min(jnp.where(vals == mx, iota, sentinel), axis=0, keepdims=True)
    return mx, idx


def _outproj_kernel(mixc_ref, mixl_ref, xc_ref, xl_ref, mod_ref, g_ref, w_ref, wr_ref, rb_ref, tri_ref,
                    x1_ref, h2_ref, ids_ref, wts_ref, rank_ref, cnt_ref, run_ref):
    i = pl.program_id(0)

    @pl.when(i == 0)
    def _():
        run_ref[...] = jnp.zeros_like(run_ref)

    gate1, shift2, scale2 = mod_ref[0, 0, 2:3, :], mod_ref[0, 0, 3:4, :], mod_ref[0, 0, 4:5, :]
    mix = jnp.where(i < N_CTX_TILES, mixc_ref[...], mixl_ref[...])
    x = jnp.where(i < N_CTX_TILES, xc_ref[...], xl_ref[...])
    x1 = x + gate1 * jnp.dot(mix, w_ref[0], preferred_element_type=F32)
    x1_ref[...] = x1
    h2 = _rmsnorm(x1, g_ref[0]) * (1.0 + scale2) + shift2
    _store_slabs(h2_ref, _pack_halves(h2))

    nt = (((1,), (1,)), ((), ()))
    h_hi = h2.astype(BF16)
    h_lo = (h2 - h_hi.astype(F32)).astype(BF16)
    part = lax.dot_general(wr_ref[0], h_hi, nt, preferred_element_type=F32)
    logits = (part[:N_EXPERTS] + part[N_EXPERTS:]
              + lax.dot_general(wr_ref[0, :N_EXPERTS, :], h_lo, nt, preferred_element_type=F32))
    scores = _sigmoid(logits)
    biased = scores + rb_ref[0]
    ninf = -jnp.inf
    mem = lax.broadcasted_iota(I32, (EXPERTS_PER_GROUP, TR), 0)
    grp_rows = []
    for g in range(N_GROUPS):
        bgp = biased[g * EXPERTS_PER_GROUP:(g + 1) * EXPERTS_PER_GROUP]
        m1, i1 = _first_index_of_max(bgp, mem, EXPERTS_PER_GROUP)
        m2 = jnp.max(jnp.where(mem == i1, ninf, bgp), axis=0, keepdims=True)
        grp_rows.append(m1 + m2)
    cur = jnp.concatenate(grp_rows, axis=0)
    gio = lax.broadcasted_iota(I32, (N_GROUPS, TR), 0)
    gsel = jnp.zeros((N_GROUPS, TR), F32)
    for _ in range(TOPK_GROUPS):
        _, gi = _first_index_of_max(cur, gio, N_GROUPS)
        hit = gio == gi
        gsel = jnp.where(hit, 1.0, gsel)
        cur = jnp.where(hit, ninf, cur)
    cur = jnp.concatenate(
        [jnp.where(gsel[g:g + 1] > 0.0, biased[g * EXPERTS_PER_GROUP:(g + 1) * EXPERTS_PER_GROUP], ninf)
         for g in range(N_GROUPS)], axis=0)
    eio = lax.broadcasted_iota(I32, (N_EXPERTS, TR), 0)
    chosen = jnp.zeros((N_EXPERTS, TR), F32)
    ids, sel = [], []
    for _ in range(TOP_K):
        _, ei = _first_index_of_max(cur, eio, N_EXPERTS)
        hit = eio == ei
        ids.append(ei)
        sel.append(jnp.sum(jnp.where(hit, scores, 0.0), axis=0, keepdims=True))
        chosen = jnp.where(hit, 1.0, chosen)
        cur = jnp.where(hit, ninf, cur)
    total = functools.reduce(lambda a, b: a + b, sel)
    ids_ref[...] = jnp.concatenate(ids, axis=0)
    wts_ref[...] = jnp.concatenate([s / total * ROUTE_SCALE for s in sel], axis=0)

    before = jnp.dot(chosen.astype(BF16), tri_ref[...], preferred_element_type=F32) + run_ref[:, 0:1]
    rank_ref[...] = jnp.concatenate(
        [jnp.sum(jnp.where(eio == ei, before, 0.0), axis=0, keepdims=True) for ei in ids], axis=0).astype(I32)
    run = run_ref[...] + jnp.sum(chosen, axis=1, keepdims=True)
    run_ref[...] = run
    cnt_ref[...] = run


def _outproj(layer, mix_ctx, mix_lat, x, mods, g_ffn, w_out_bf, wr_split, router_bias, tri):
    row = lambda i: (i, 0)
    col = lambda i: (0, i)
    const = lambda i: (0, 0)
    lay = lambda i: (layer, 0, 0)
    return pl.pallas_call(
        _outproj_kernel,
        grid=(N_TILES,),
        in_specs=[_CTX_ROWS, _LAT_ROWS, _CTX_ROWS, _LAT_ROWS,
                  _mod_spec(layer),
                  pl.BlockSpec((1, 1, D_MODEL), lay),
                  pl.BlockSpec((1, D_MODEL, D_MODEL), lay, pipeline_mode=pl.Buffered(1)),
                  pl.BlockSpec((1, 2 * N_EXPERTS, D_MODEL), lay),
                  pl.BlockSpec((1, N_EXPERTS, 1), lay),
                  pl.BlockSpec((TR, TR), const)],
        out_specs=[pl.BlockSpec((TR, D_MODEL), row),
                   pl.BlockSpec((TR * SLAB, LANES), row),
                   pl.BlockSpec((TOP_K, TR), col),
                   pl.BlockSpec((TOP_K, TR), col),
                   pl.BlockSpec((TOP_K, TR), col),
                   pl.BlockSpec((N_EXPERTS, META_LANES), const)],
        out_shape=[jax.ShapeDtypeStruct((T_ALL, D_MODEL), F32),
                   jax.ShapeDtypeStruct((T_ALL * SLAB, LANES), U32),
                   jax.ShapeDtypeStruct((TOP_K, T_ALL), I32),
                   jax.ShapeDtypeStruct((TOP_K, T_ALL), F32),
                   jax.ShapeDtypeStruct((TOP_K, T_ALL), I32),
                   jax.ShapeDtypeStruct((N_EXPERTS, META_LANES), F32)],
        scratch_shapes=[pltpu.VMEM((N_EXPERTS, META_LANES), F32)],
        compiler_params=_cparams(("arbitrary",)),
        name="outproj",
    )(mix_ctx, mix_lat, *x, mods, g_ffn.reshape(DEPTH, 1, D_MODEL), w_out_bf, wr_split,
      router_bias.reshape(DEPTH, N_EXPERTS, 1), tri)


def _positions_kernel(ids_ref, rank_ref, cnt_ref, pos_ref, te_ref, meta_ref):
    ids = ids_ref[...]
    pos = rank_ref[...]
    tile_io = lax.broadcasted_iota(I32, (1, TE_LANES), 1)
    lane_io = lax.broadcasted_iota(I32, (1, META_LANES), 1)
    te = jnp.zeros((1, TE_LANES), I32)
    starts = jnp.zeros((1, META_LANES), I32)
    counts = jnp.zeros((1, META_LANES), I32)
    tile_ends = jnp.zeros((1, META_LANES), I32)
    parities = jnp.zeros((1, META_LANES), I32)
    start = jnp.zeros((1, 1), I32)
    n_seg = jnp.zeros((1, 1), I32)
    for e in range(N_EXPERTS):
        c = cnt_ref[e:e + 1, 0:1].astype(I32)
        pos = pos + jnp.where(ids == e, start, 0)
        starts = starts + jnp.where(lane_io == e, start, 0)
        counts = counts + jnp.where(lane_io == e, c, 0)
        parities = parities + jnp.where(lane_io == e, n_seg % 2, 0)
        start = start + ((c + (TM - 1)) // TM) * TM
        n_seg = n_seg + jnp.where(c > 0, 1, 0)
        tile_ends = tile_ends + jnp.where(lane_io == e, start // TM, 0)
        te = te + jnp.where(tile_io * TM >= start, 1, 0)
    pos_ref[...] = pos
    te_ref[...] = jnp.minimum(te, N_EXPERTS - 1)
    meta_ref[0:1, :] = starts
    meta_ref[1:2, :] = counts
    meta_ref[2:3, :] = jnp.broadcast_to(start // TM, (1, META_LANES))
    meta_ref[3:4, :] = tile_ends
    meta_ref[4:5, :] = parities
    meta_ref[5:8, :] = jnp.zeros((3, META_LANES), I32)


def _positions(ids, rank, cnt):
    return pl.pallas_call(
        _positions_kernel,
        out_shape=[jax.ShapeDtypeStruct((TOP_K, T_ALL), I32),
                   jax.ShapeDtypeStruct((1, TE_LANES), I32),
                   jax.ShapeDtypeStruct((8, META_LANES), I32)],
        compiler_params=_cparams(),
        name="positions",
    )(ids, rank, cnt)


SC_CORES, SC_SUBCORES = 2, 16
SC_WORKERS = SC_CORES * SC_SUBCORES
SC_CHUNK = 32


def _sc_kernel(body, out_type, n_idx_rows, name):
    return pl.kernel(
        body,
        out_type=out_type,
        mesh=plsc.VectorSubcoreMesh(core_axis_name="c", subcore_axis_name="s",
                                    num_cores=SC_CORES, num_subcores=SC_SUBCORES),
        scratch_types=[pltpu.VMEM((n_idx_rows, SC_CHUNK), I32),
                       pltpu.VMEM((2, SC_CHUNK, SLAB, LANES), U32),
                       pltpu.SemaphoreType.DMA,
                       pltpu.SemaphoreType.DMA((2,)),
                       pltpu.SemaphoreType.DMA((2,))],
        compiler_params=pltpu.CompilerParams(use_tc_tiling_on_sc=True),
        name=name,
    )


def _sc_worker():
    return lax.axis_index("s") * SC_CORES + lax.axis_index("c")


def _sc_load_indices(n_rows, src_of_row, idx_v, sem):
    @pl.loop(0, n_rows)
    def _(r):
        src, dst = src_of_row(r), idx_v.at[r]
        pltpu.make_async_copy(src, dst, sem).start()

    @pl.loop(0, n_rows)
    def _(r):
        src, dst = src_of_row(r), idx_v.at[r]
        pltpu.make_async_copy(src, dst, sem).wait()


def _sc_two_slot_pipeline(n_chunks, fill, fill_wait, drain, drain_wait):
    assert n_chunks % 2 == 0 and n_chunks >= 4
    fill(0, 0)

    @pl.loop(0, n_chunks // 2)
    def _(pair):
        c = 2 * pair

        @pl.when(pair >= 1)
        def _():
            drain_wait(c - 1, 1)

        fill(c + 1, 1)
        fill_wait(c, 0)
        drain(c, 0)

        @pl.when(pair < n_chunks // 2 - 1)
        def _():
            drain_wait(c, 0)
            fill(c + 2, 0)

        fill_wait(c + 1, 1)
        drain(c + 1, 1)

    drain_wait(n_chunks - 2, 0)
    drain_wait(n_chunks - 1, 1)


def _dispatch(pos_flat, h3):
    n_chunks = T_ALL // SC_WORKERS // SC_CHUNK

    def body(pos_hbm, h_hbm, xs_hbm, idx_v, rows_v, idx_sem, fill_sem, drain_sem):
        t_base = _sc_worker() * (n_chunks * SC_CHUNK)

        def index_src(r):
            t0 = t_base + (r // TOP_K) * SC_CHUNK
            return pos_hbm.at[pl.ds(pl.multiple_of((r % TOP_K) * T_ALL + t0, SC_CHUNK), SC_CHUNK)]

        _sc_load_indices(n_chunks * TOP_K, index_src, idx_v, idx_sem)

        def load(c, slot):
            t0 = pl.multiple_of(t_base + c * SC_CHUNK, SC_CHUNK)
            return pltpu.make_async_copy(h_hbm.at[pl.ds(t0, SC_CHUNK)], rows_v.at[slot], fill_sem.at[slot])

        def scatter(c, slot, k):
            return pltpu.make_async_copy(rows_v.at[slot], xs_hbm.at[idx_v.at[c * TOP_K + k]], drain_sem.at[slot])

        def drain(c, slot):
            for k in range(TOP_K):
                scatter(c, slot, k).start()

        def drain_wait(c, slot):
            for k in range(TOP_K):
                scatter(c, slot, k).wait()

        _sc_two_slot_pipeline(n_chunks, lambda c, s: load(c, s).start(), lambda c, s: load(c, s).wait(),
                              drain, drain_wait)

    return _sc_kernel(body, jax.ShapeDtypeStruct((P_SORT, SLAB, LANES), U32), n_chunks * TOP_K,
                      "sc_dispatch")(pos_flat, h3)


def _gather(pos_flat, y3):
    n_items = pos_flat.shape[0]
    n_chunks = n_items // SC_WORKERS // SC_CHUNK

    def body(pos_hbm, y_hbm, out_hbm, idx_v, rows_v, idx_sem, fill_sem, drain_sem):
        base = _sc_worker() * (n_chunks * SC_CHUNK)

        def index_src(r):
            return pos_hbm.at[pl.ds(pl.multiple_of(base + r * SC_CHUNK, SC_CHUNK), SC_CHUNK)]

        _sc_load_indices(n_chunks, index_src, idx_v, idx_sem)

        def gather(c, slot):
            return pltpu.make_async_copy(y_hbm.at[idx_v.at[c]], rows_v.at[slot], fill_sem.at[slot])

        def store(c, slot):
            i0 = pl.multiple_of(base + c * SC_CHUNK, SC_CHUNK)
            return pltpu.make_async_copy(rows_v.at[slot], out_hbm.at[pl.ds(i0, SC_CHUNK)], drain_sem.at[slot])

        _sc_two_slot_pipeline(n_chunks, lambda c, s: gather(c, s).start(), lambda c, s: gather(c, s).wait(),
                              lambda c, s: store(c, s).start(), lambda c, s: store(c, s).wait())

    return _sc_kernel(body, jax.ShapeDtypeStruct((n_items, SLAB, LANES), U32), n_chunks,
                      "sc_gather")(pos_flat, y3)


def _experts_kernel(layer, te_ref, nt_ref, end_ref, tend_ref, par_ref, x_ref, wg_hbm, wu_hbm, wd_hbm, y_ref,
                    wg_f, wu_f, wd_f, wg_bf, wu_bf, wd_bf, sem):
    j = pl.program_id(0)
    used = j < nt_ref[0]
    e = te_ref[j]
    first = (j == 0) | (te_ref[jnp.maximum(j - 1, 0)] != e)
    slot = par_ref[e]

    def fetch(expert, s):
        return [pltpu.make_async_copy(src.at[layer, expert], dst.at[s], sem.at[s])
                for src, dst in ((wg_hbm, wg_f), (wu_hbm, wu_f), (wd_hbm, wd_f))]

    @pl.when(used & (j == 0))
    def _():
        for cp in fetch(e, slot):
            cp.start()

    @pl.when(used & first)
    def _():
        for cp in fetch(e, slot):
            cp.wait()
        wg_bf[...] = wg_f[slot].astype(BF16)
        wu_bf[...] = wu_f[slot].astype(BF16)
        wd_bf[...] = wd_f[slot].astype(BF16)
        nxt = tend_ref[e]

        @pl.when(nxt < nt_ref[0])
        def _():
            for cp in fetch(te_ref[jnp.minimum(nxt, pl.num_programs(0) - 1)], 1 - slot):
                cp.start()

    @pl.when(used)
    def _():
        valid = end_ref[te_ref[j]] - j * TM
        rows = lax.broadcasted_iota(I32, (TM, HALF), 0)
        words = jnp.where(rows < valid, _load_slabs(x_ref, TM), jnp.uint32(0))
        xa, xb = _unpack_halves(words)
        xa, xb = xa.astype(BF16), xb.astype(BF16)
        g = (jnp.dot(xa, wg_bf[:HALF], preferred_element_type=F32)
             + jnp.dot(xb, wg_bf[HALF:], preferred_element_type=F32))
        u = (jnp.dot(xa, wu_bf[:HALF], preferred_element_type=F32)
             + jnp.dot(xb, wu_bf[HALF:], preferred_element_type=F32))
        a = (_silu(g) * u).astype(BF16)
        _store_slabs(y_ref, _pack_halves(jnp.dot(a, wd_bf[...], preferred_element_type=F32)))


def _experts(layer, te, nt, end, tile_end, parity, xs, w_gate, w_up, w_down):
    n_tiles = xs.shape[0] // (TM * SLAB)
    tile = lambda j, te, nt, *_: (jnp.minimum(j, nt[0] - 1), 0)
    return pl.pallas_call(
        functools.partial(_experts_kernel, layer),
        grid_spec=pltpu.PrefetchScalarGridSpec(
            num_scalar_prefetch=5,
            grid=(n_tiles,),
            in_specs=[pl.BlockSpec((TM * SLAB, LANES), tile),
                      pl.BlockSpec(memory_space=pl.ANY),
                      pl.BlockSpec(memory_space=pl.ANY),
                      pl.BlockSpec(memory_space=pl.ANY)],
            out_specs=pl.BlockSpec((TM * SLAB, LANES), tile),
            scratch_shapes=[pltpu.VMEM((2, D_MODEL, EXPERT_FF), F32),
                            pltpu.VMEM((2, D_MODEL, EXPERT_FF), F32),
                            pltpu.VMEM((2, EXPERT_FF, D_MODEL), F32),
                            pltpu.VMEM((D_MODEL, EXPERT_FF), BF16),
                            pltpu.VMEM((D_MODEL, EXPERT_FF), BF16),
                            pltpu.VMEM((EXPERT_FF, D_MODEL), BF16),
                            pltpu.SemaphoreType.DMA((2,))]),
        out_shape=jax.ShapeDtypeStruct(xs.shape, U32),
        compiler_params=_cparams(("arbitrary",)),
        name="experts",
    )(te, nt, end, tile_end, parity, xs, w_gate, w_up, w_down)


def _combine_kernel(final, wts_ref, x1_ref, ysh_ref, mod_ref, g_ref, buf_ref, *rest):
    o_ref, x2_ref, wb_ref = rest
    w = jnp.transpose(wts_ref[...])
    for k in range(TOP_K):
        wb_ref[k] = jnp.broadcast_to(w[:, k:k + 1], (TC_ROWS, LANES))
    sq = jnp.zeros((TC_ROWS, LANES), F32)
    for s in range(SLAB):
        acc_a, acc_b = _unpack_halves(ysh_ref[pl.ds(s, TC_ROWS, stride=SLAB), :])
        for k in range(TOP_K):
            ya, yb = _unpack_halves(buf_ref[k, pl.ds(s, TC_ROWS, stride=SLAB), :])
            acc_a = acc_a + wb_ref[k] * ya
            acc_b = acc_b + wb_ref[k] * yb
        for half, acc in ((0, acc_a), (1, acc_b)):
            cols = slice(half * HALF + s * LANES, half * HALF + (s + 1) * LANES)
            x2 = x1_ref[:, cols] + mod_ref[0, 0, 5:6, cols] * acc
            x2_ref[:, cols] = x2
            if final:
                sq = sq + x2 * x2
    out = x2_ref[...]
    if final:
        ms = jnp.sum(sq, axis=-1, keepdims=True) * (1.0 / D_MODEL)
        out = out * lax.rsqrt(ms + EPS) * g_ref[...]

    o_ref[...] = out


def _combine(layer, row0, n_rows, wts, x1, ysh, mods, final_norm, yg):
    final = layer == DEPTH - 1
    t0 = row0 // TC_ROWS
    row = lambda i: (t0 + i, 0)
    return pl.pallas_call(
        functools.partial(_combine_kernel, final),
        grid=(n_rows // TC_ROWS,),
        in_specs=[pl.BlockSpec((TOP_K, TC_ROWS), lambda i: (0, t0 + i)),
                  pl.BlockSpec((TC_ROWS, D_MODEL), row),
                  pl.BlockSpec((TC_ROWS * SLAB, LANES), row),
                  pl.BlockSpec((1, 1, N_MOD, D_MODEL),
                               lambda i: (layer, _cond_row((t0 + i) // (TR // TC_ROWS)), 0, 0)),
                  pl.BlockSpec((1, D_MODEL), lambda i: (0, 0)),
                  pl.BlockSpec((TOP_K, TC_ROWS * SLAB, LANES), lambda i: (0, i, 0))],
        out_specs=pl.BlockSpec((TC_ROWS, D_MODEL), lambda i: (i, 0)),
        out_shape=jax.ShapeDtypeStruct((n_rows, D_MODEL), F32),
        scratch_shapes=[pltpu.VMEM((TC_ROWS, D_MODEL), F32),
                        pltpu.VMEM((TOP_K, TC_ROWS, LANES), F32)],
        compiler_params=_cparams(("arbitrary",)),
        name="combine",
    )(wts, x1, ysh, mods, final_norm.reshape(1, D_MODEL), yg)


def kernel(x_prompt, x_sample, cache_k, cache_v, c, c_ctx, w_mod, b_mod, g_mix, g_ffn, w_in, w_out, conv_w,
           attn_sink, w_router, router_bias, w_gate_e, w_up_e, w_down_e, w_gate_s, w_up_s, w_down_s, final_norm):
    ccsc = jnp.asarray(_channel_dft(), BF16)
    csl_ctx = jnp.asarray(_dft_tables(SEQ), BF16)
    csl_lat = jnp.asarray(_dft_tables(DEC_SEQ), BF16)
    rope = tuple(jnp.asarray(t, F32) for t in _rope_tables(DEC_SEQ))
    tri = jnp.asarray(np.triu(np.ones((TR, TR)), 1), BF16)
    shared_te = jnp.zeros((T_ALL // TM,), I32)
    shared_nt = jnp.full((1,), T_ALL // TM, I32)
    shared_end = jnp.full((1,), T_ALL, I32)

    cond = jnp.concatenate([c_ctx[None], c, jnp.zeros((N_COND - 1 - DEC_BATCH, D_MODEL), F32)], axis=0)
    mods = _mods(cond, w_mod, b_mod).reshape(DEPTH, N_COND, N_MOD, D_MODEL)
    cache_k = cache_k.reshape(DEC_BATCH, DEPTH, SEQ, KV_W)
    cache_v = cache_v.reshape(DEC_BATCH, DEPTH, SEQ, KV_W)
    w_in_bf, w_out_bf = w_in.astype(BF16), w_out.astype(BF16)
    wr_t = jnp.swapaxes(w_router, 1, 2)
    wr_hi = wr_t.astype(BF16)
    wr_split = jnp.concatenate([wr_hi, (wr_t - wr_hi.astype(F32)).astype(BF16)], axis=1)
    ws_gate, ws_up, ws_down = w_gate_s[:, None], w_up_s[:, None], w_down_s[:, None]
    x = (x_prompt.reshape(T_CTX, D_MODEL), x_sample.reshape(T_LAT, D_MODEL))

    new_kv = ()
    for l in range(DEPTH):
        q, kv, fm, *new_kv = _inproj(l, x, mods, g_mix, w_in_bf, rope, tuple(new_kv))
        mix_ctx = _ctx_mix(l, attn_sink, q, kv, fm, ccsc, csl_ctx, conv_w)
        mix_lat = _lat_mix(l, attn_sink, q, kv, fm, cache_k, cache_v, ccsc, csl_lat, conv_w)
        x1, h2, ids, wts, rank, cnt = _outproj(l, mix_ctx, mix_lat, x, mods, g_ffn, w_out_bf, wr_split,
                                               router_bias, tri)
        pos, te, meta = _positions(ids, rank, cnt)
        pos_flat = pos.reshape(TOP_K * T_ALL)
        xs = _dispatch(pos_flat, h2.reshape(T_ALL, SLAB, LANES)).reshape(P_SORT * SLAB, LANES)
        y = _experts(l, te[0, :N_SORT_TILES], meta[2, :1], meta[0, :N_EXPERTS] + meta[1, :N_EXPERTS],
                     meta[3, :N_EXPERTS], meta[4, :N_EXPERTS], xs, w_gate_e, w_up_e, w_down_e)
        ysh = _experts(l, shared_te, shared_nt, shared_end, shared_nt, shared_te[:1], h2,
                       ws_gate, ws_up, ws_down)
        y3 = y.reshape(P_SORT, SLAB, LANES)
        x = []
        for row0, n_rows in ((0, T_CTX), (T_CTX, T_LAT)):
            part_pos = pos[:, row0:row0 + n_rows].reshape(TOP_K * n_rows)
            yg = _gather(part_pos, y3).reshape(TOP_K, n_rows * SLAB, LANES)
            x.append(_combine(l, row0, n_rows, wts, x1, ysh, mods, final_norm, yg))

    y_prompt = x[0].reshape(BATCH, SEQ, D_MODEL)
    y_sample = x[1].reshape(DEC_BATCH, DEC_SEQ, D_MODEL)
    new_k, new_v = new_kv
    return y_prompt, y_sample, new_k, new_v
```

```python
import functools

import numpy as np
import jax
import jax.numpy as jnp
from jax import lax
from jax.experimental import pallas as pl
from jax.experimental.pallas import tpu as pltpu
from jax.experimental.pallas import tpu_sc as plsc

F32 = jnp.float32
BF16 = jnp.bfloat16
I32 = jnp.int32
U32 = jnp.uint32

D_MODEL = 2048
HALF = D_MODEL // 2
BATCH, SEQ = 32, 256
DEC_BATCH, DEC_SEQ = 4, 1024
DEPTH = 2
GRID_W = 64
HEAD_DIM = 128
N_HEADS, N_KV_HEADS, Q_PER_KV = 8, 2, 4
ATTN_W, KV_W = 1024, 256
WINDOW, BLOCK = 128, 128
SCALE = HEAD_DIM ** -0.5
ROPE_BASE = 10000.0
ROPE_PAIRS = 32
FOURIER_W, FOURIER_GROUPS, FOURIER_GROUP_W = 512, 4, 128
CONV_W = 512
IN_W = 3584
N_EXPERTS, TOP_K, N_GROUPS, EXPERTS_PER_GROUP, TOPK_GROUPS = 64, 8, 8, 8, 4
EXPERT_FF = 512
ROUTE_SCALE = 2.5
N_MOD = 6
EPS = 1e-6
NEG = -1e30

T_CTX = BATCH * SEQ
T_LAT = DEC_BATCH * DEC_SEQ
T_ALL = T_CTX + T_LAT
TR = 512
N_CTX_TILES = T_CTX // TR
LAT_TILES_PER_SEQ = DEC_SEQ // TR
N_TILES = T_ALL // TR
N_COND = 8
MOD_TN = 1024
TM = 512
TC_ROWS = 128
N_SORT_TILES = T_ALL * TOP_K // TM + N_EXPERTS
P_SORT = N_SORT_TILES * TM
META_LANES = 128
TE_LANES = 512
VMEM_LIMIT = 56 * 1024 * 1024


def _cparams(sem=None):
    return pltpu.CompilerParams(dimension_semantics=sem, vmem_limit_bytes=VMEM_LIMIT)


def _silu(x):
    return x / (1.0 + jnp.exp(-x))


def _sigmoid(x):
    return 1.0 / (1.0 + jnp.exp(-x))


def _rmsnorm(x, g):
    return x * lax.rsqrt(jnp.mean(x * x, axis=-1, keepdims=True) + EPS) * g


def _cond_row(i):
    return jnp.where(i < N_CTX_TILES, 0, 1 + (i - N_CTX_TILES) // LAT_TILES_PER_SEQ)


def _pack_halves(v):
    n = v.shape[1] // 2
    hi = pltpu.bitcast(v[:, :n].astype(BF16).astype(F32), U32)
    lo = pltpu.bitcast(v[:, n:].astype(BF16).astype(F32), U32)
    return hi | (lo >> 16)


def _unpack_halves(w):
    return pltpu.bitcast(w & jnp.uint32(0xFFFF0000), F32), pltpu.bitcast(w << 16, F32)


SLAB, LANES = 8, 128


def _load_slabs(ref, n, row0=0):
    return jnp.concatenate([ref[pl.ds(row0 * SLAB + s, n, stride=SLAB), :] for s in range(SLAB)], axis=1)


def _store_slabs(ref, words, row0=0):
    n = words.shape[0]
    for s in range(SLAB):
        ref[pl.ds(row0 * SLAB + s, n, stride=SLAB), :] = words[:, s * LANES:(s + 1) * LANES]


def _dft_tables(L):
    j = np.arange(L, dtype=np.int64)
    ang = 2.0 * np.pi * ((j[:, None] * j[None, :]) % L).astype(np.float64) / L
    c, s = np.cos(ang) / np.sqrt(L), np.sin(ang) / np.sqrt(L)
    return np.concatenate([c, -s], axis=1)


def _channel_dft():
    n = FOURIER_GROUP_W
    j = np.arange(n, dtype=np.int64)
    ang = 2.0 * np.pi * ((j[:, None] * j[None, :]) % n).astype(np.float64) / n
    return np.concatenate([np.cos(ang), np.sin(ang)], axis=1) / np.sqrt(n)


def _rope_tables(S, n_identity):
    pos = np.arange(S + n_identity)
    pos[S:] = 0
    row, col = (pos // GRID_W).astype(np.float64), (pos % GRID_W).astype(np.float64)
    inv = ROPE_BASE ** (-np.arange(ROPE_PAIRS, dtype=np.float64) / ROPE_PAIRS)
    ar, ac = row[:, None] * inv, col[:, None] * inv
    z = np.zeros_like(ar)
    cos = np.concatenate([np.cos(ar), np.cos(ar), np.cos(ac), np.cos(ac)], axis=1)
    sin_hi = np.concatenate([-np.sin(ar), z, -np.sin(ac), z], axis=1)
    sin_lo = np.concatenate([z, np.sin(ar), z, np.sin(ac)], axis=1)
    return cos, sin_hi, sin_lo


def _mods_kernel(cond_ref, w_ref, b_ref, o_ref):
    s = _silu(cond_ref[...]).astype(BF16)
    o_ref[0] = jnp.dot(s, w_ref[0].astype(BF16), preferred_element_type=F32) + b_ref[0]


def _mods(cond, w_mod, b_mod):
    n = N_MOD * D_MODEL
    return pl.pallas_call(
        _mods_kernel,
        grid=(DEPTH, n // MOD_TN),
        in_specs=[pl.BlockSpec((N_COND, D_MODEL), lambda l, j: (0, 0)),
                  pl.BlockSpec((1, D_MODEL, MOD_TN), lambda l, j: (l, 0, j)),
                  pl.BlockSpec((1, 1, MOD_TN), lambda l, j: (l, 0, j))],
        out_specs=pl.BlockSpec((1, N_COND, MOD_TN), lambda l, j: (l, 0, j)),
        out_shape=jax.ShapeDtypeStruct((DEPTH, N_COND, n), F32),
        compiler_params=_cparams(("arbitrary", "arbitrary")),
        name="mods",
    )(cond, w_mod, b_mod.reshape(DEPTH, 1, n))


_CTX_ROWS = pl.BlockSpec((TR, D_MODEL), lambda i: (jnp.minimum(i, N_CTX_TILES - 1), 0))
_LAT_ROWS = pl.BlockSpec((TR, D_MODEL), lambda i: (jnp.maximum(i - N_CTX_TILES, 0), 0))


def _mod_spec(layer):
    return pl.BlockSpec((1, 1, N_MOD, D_MODEL), lambda i: (layer, _cond_row(i), 0, 0))


def _inproj_kernel(n_alias, xc_ref, xl_ref, mod_ref, g_ref, w_ref, cos_ref, shi_ref, slo_ref, *rest):
    q_ref, kv_ref, fm_ref, nk_ref, nv_ref = rest[n_alias:]
    i = pl.program_id(0)
    shift, scale = mod_ref[0, 0, 0:1, :], mod_ref[0, 0, 1:2, :]
    for r0 in range(0, TR, SEQ):
        rows = slice(r0, r0 + SEQ)
        x = jnp.where(i < N_CTX_TILES, xc_ref[rows, :], xl_ref[rows, :])
        h = (_rmsnorm(x, g_ref[0]) * (1.0 + scale) + shift).astype(BF16)
        fm_ref[rows, :] = jnp.dot(h, w_ref[0, :, ATTN_W + 2 * KV_W:], preferred_element_type=F32).astype(BF16)
        qkv = jnp.dot(h, w_ref[0, :, :ATTN_W + 2 * KV_W], preferred_element_type=F32)
        kv_ref[rows, KV_W:] = qkv[:, ATTN_W + KV_W:]
        cos, shi, slo = cos_ref[rows, :], shi_ref[rows, :], slo_ref[rows, :]
        for hd in range(N_HEADS + N_KV_HEADS):
            xh = qkv[:, hd * HEAD_DIM:(hd + 1) * HEAD_DIM]
            r = xh * cos + pltpu.roll(xh, HEAD_DIM - 32, 1) * shi + pltpu.roll(xh, 32, 1) * slo
            if hd < N_HEADS:
                q_ref[rows, hd * HEAD_DIM:(hd + 1) * HEAD_DIM] = r.astype(BF16)
            else:
                kv_ref[rows, (hd - N_HEADS) * HEAD_DIM:(hd - N_HEADS + 1) * HEAD_DIM] = r

    @pl.when(i < N_CTX_TILES)
    def _():
        for b in range(TR // SEQ):
            for hd in range(N_KV_HEADS):
                c0 = hd * HEAD_DIM
                nk_ref[b, 0, :, hd, :] = kv_ref[b * SEQ:(b + 1) * SEQ, c0:c0 + HEAD_DIM]
                nv_ref[b, 0, :, hd, :] = kv_ref[b * SEQ:(b + 1) * SEQ, KV_W + c0:KV_W + c0 + HEAD_DIM]


def _inproj(layer, x, mods, g_mix, w_in_bf, rope, new_kv):
    lat_blk = lambda i: (jnp.where(i < N_CTX_TILES, LAT_TILES_PER_SEQ, (i - N_CTX_TILES) % LAT_TILES_PER_SEQ), 0)
    row = lambda i: (i, 0)
    kv_blk = pl.BlockSpec((TR // SEQ, 1, SEQ, N_KV_HEADS, HEAD_DIM),
                          lambda i: (jnp.minimum(i, N_CTX_TILES - 1), layer, 0, 0, 0))
    kv_shape = jax.ShapeDtypeStruct((BATCH, DEPTH, SEQ, N_KV_HEADS, HEAD_DIM), F32)
    n_in = 8
    return pl.pallas_call(
        functools.partial(_inproj_kernel, len(new_kv)),
        grid=(N_TILES,),
        in_specs=[_CTX_ROWS, _LAT_ROWS,
                  _mod_spec(layer),
                  pl.BlockSpec((1, 1, D_MODEL), lambda i: (layer, 0, 0)),
                  pl.BlockSpec((1, D_MODEL, IN_W), lambda i: (layer, 0, 0), pipeline_mode=pl.Buffered(1)),
                  pl.BlockSpec((TR, HEAD_DIM), lat_blk),
                  pl.BlockSpec((TR, HEAD_DIM), lat_blk),
                  pl.BlockSpec((TR, HEAD_DIM), lat_blk)] + [pl.BlockSpec(memory_space=pl.ANY)] * len(new_kv),
        out_specs=[pl.BlockSpec((TR, ATTN_W), row),
                   pl.BlockSpec((TR, 2 * KV_W), row),
                   pl.BlockSpec((TR, FOURIER_W + 3 * CONV_W), row),
                   kv_blk, kv_blk],
        out_shape=[jax.ShapeDtypeStruct((T_ALL, ATTN_W), BF16),
                   jax.ShapeDtypeStruct((T_ALL, 2 * KV_W), F32),
                   jax.ShapeDtypeStruct((T_ALL, FOURIER_W + 3 * CONV_W), BF16),
                   kv_shape, kv_shape],
        input_output_aliases={n_in + a: 3 + a for a in range(len(new_kv))},
        compiler_params=_cparams(("arbitrary",)),
        name="inproj",
    )(*x, mods, g_mix.reshape(DEPTH, 1, D_MODEL), w_in_bf, *rope, *new_kv)


def _fourier_conv(fm_ref, ccsc_ref, csl_ref, cw_ref, o_ref):
    L = fm_ref.shape[0]
    for g in range(FOURIER_GROUPS):
        xg = fm_ref[:, g * FOURIER_GROUP_W:(g + 1) * FOURIER_GROUP_W]
        z = jnp.dot(xg, ccsc_ref[...], preferred_element_type=F32)
        zz = jnp.concatenate([z[:, :FOURIER_GROUP_W], z[:, FOURIER_GROUP_W:]], axis=0).astype(BF16)
        f = jnp.dot(csl_ref[...], zz, preferred_element_type=F32)
        o_ref[:, ATTN_W + g * FOURIER_GROUP_W:ATTN_W + (g + 1) * FOURIER_GROUP_W] = f.astype(BF16)
    xc = fm_ref[:, FOURIER_W:FOURIER_W + CONV_W].astype(F32)
    bg = fm_ref[:, FOURIER_W + CONV_W:FOURIER_W + 2 * CONV_W].astype(F32)
    cg = fm_ref[:, FOURIER_W + 2 * CONV_W:].astype(F32)
    u = cg * xc
    rows = lax.broadcasted_iota(I32, u.shape, 0)
    prev = jnp.where(rows == 0, 0.0, pltpu.roll(u, 1, 0))
    nxt = jnp.where(rows == L - 1, 0.0, pltpu.roll(u, L - 1, 0))
    y = prev * cw_ref[0, 0:1, :] + u * cw_ref[0, 1:2, :] + nxt * cw_ref[0, 2:3, :]
    o_ref[:, ATTN_W + FOURIER_W:] = (bg * y).astype(BF16)


def _ctx_mix_kernel(layer, sink_ref, q_ref, kv_ref, fm_ref, ccsc_ref, csl_ref, cw_ref, o_ref):
    for h in range(N_HEADS):
        g = h // Q_PER_KV
        q = q_ref[:, h * HEAD_DIM:(h + 1) * HEAD_DIM]
        k = kv_ref[:, g * HEAD_DIM:(g + 1) * HEAD_DIM].astype(BF16)
        v = kv_ref[:, KV_W + g * HEAD_DIM:KV_W + (g + 1) * HEAD_DIM].astype(BF16)
        s = lax.dot_general(q, k, (((1,), (1,)), ((), ())), preferred_element_type=F32) * SCALE
        sink = sink_ref[layer, h]
        m = jnp.maximum(jnp.max(s, axis=-1, keepdims=True), sink)
        e = jnp.exp(s - m)
        den = jnp.sum(e, axis=-1, keepdims=True) + jnp.exp(sink - m)
        o = jnp.dot(e.astype(BF16), v, preferred_element_type=F32) / den
        o_ref[:, h * HEAD_DIM:(h + 1) * HEAD_DIM] = o.astype(BF16)
    _fourier_conv(fm_ref, ccsc_ref, csl_ref, cw_ref, o_ref)


def _ctx_mix(layer, sink, q, kv, fm, ccsc, csl, conv_w):
    row = lambda b: (b, 0)
    const = lambda b: (0, 0)
    return pl.pallas_call(
        functools.partial(_ctx_mix_kernel, layer),
        grid=(BATCH,),
        in_specs=[pl.BlockSpec(memory_space=pltpu.SMEM),
                  pl.BlockSpec((SEQ, ATTN_W), row),
                  pl.BlockSpec((SEQ, 2 * KV_W), row),
                  pl.BlockSpec((SEQ, FOURIER_W + 3 * CONV_W), row),
                  pl.BlockSpec((FOURIER_GROUP_W, 2 * FOURIER_GROUP_W), const),
                  pl.BlockSpec((SEQ, 2 * SEQ), const),
                  pl.BlockSpec((1, 3, CONV_W), lambda b: (layer, 0, 0))],
        out_specs=pl.BlockSpec((SEQ, D_MODEL), row),
        out_shape=jax.ShapeDtypeStruct((T_CTX, D_MODEL), BF16),
        compiler_params=_cparams(("arbitrary",)),
        name="ctx_mix",
    )(sink, q, kv, fm, ccsc, csl, conv_w)


def _lat_mix_kernel(layer, sink_ref, q_ref, kv_ref, fm_ref, ck_ref, cv_ref, ccsc_ref, csl_ref, cw_ref, o_ref):
    n = pl.program_id(1)
    nb = DEC_SEQ // BLOCK
    rows = Q_PER_KV * BLOCK
    band = 3 * BLOCK

    @pl.when(n == 0)
    def _():
        _fourier_conv(fm_ref, ccsc_ref, csl_ref, cw_ref, o_ref)

    q0 = pl.multiple_of(n * BLOCK, BLOCK)
    k0 = pl.multiple_of(jnp.clip(n - 1, 0, nb - 3) * BLOCK, BLOCK)
    qpos = q0 + lax.broadcasted_iota(I32, (rows, band), 0) % BLOCK
    kpos = k0 + lax.broadcasted_iota(I32, (rows, band), 1)
    valid = jnp.abs(qpos - kpos) <= WINDOW
    for g in range(N_KV_HEADS):
        kb = kv_ref[pl.ds(k0, band), g * HEAD_DIM:(g + 1) * HEAD_DIM].astype(BF16)
        vb = kv_ref[pl.ds(k0, band), KV_W + g * HEAD_DIM:KV_W + (g + 1) * HEAD_DIM].astype(BF16)
        k_ctx = ck_ref[0, 0, :, g * HEAD_DIM:(g + 1) * HEAD_DIM].astype(BF16)
        v_ctx = cv_ref[0, 0, :, g * HEAD_DIM:(g + 1) * HEAD_DIM].astype(BF16)
        sink = jnp.concatenate(
            [jnp.full((BLOCK, 1), sink_ref[layer, g * Q_PER_KV + j], F32) for j in range(Q_PER_KV)], axis=0)
        q = jnp.concatenate(
            [q_ref[pl.ds(q0, BLOCK), (g * Q_PER_KV + j) * HEAD_DIM:(g * Q_PER_KV + j + 1) * HEAD_DIM]
             for j in range(Q_PER_KV)], axis=0)
        s_loc = lax.dot_general(q, kb, (((1,), (1,)), ((), ())), preferred_element_type=F32) * SCALE
        s_loc = jnp.where(valid, s_loc, NEG)
        s_ctx = lax.dot_general(q, k_ctx, (((1,), (1,)), ((), ())), preferred_element_type=F32) * SCALE
        m = jnp.maximum(jnp.maximum(jnp.max(s_loc, axis=-1, keepdims=True),
                                    jnp.max(s_ctx, axis=-1, keepdims=True)), sink)
        e_loc, e_ctx = jnp.exp(s_loc - m), jnp.exp(s_ctx - m)
        den = (jnp.sum(e_loc, axis=-1, keepdims=True) + jnp.sum(e_ctx, axis=-1, keepdims=True)
               + jnp.exp(sink - m))
        o = (jnp.dot(e_loc.astype(BF16), vb, preferred_element_type=F32)
             + jnp.dot(e_ctx.astype(BF16), v_ctx, preferred_element_type=F32)) / den
        for j in range(Q_PER_KV):
            h = g * Q_PER_KV + j
            o_ref[pl.ds(q0, BLOCK), h * HEAD_DIM:(h + 1) * HEAD_DIM] = o[j * BLOCK:(j + 1) * BLOCK].astype(BF16)


def _lat_mix(layer, sink, q, kv, fm, cache_k, cache_v, ccsc, csl, conv_w):
    off = T_CTX // DEC_SEQ
    row = lambda b, n: (off + b, 0)
    const = lambda b, n: (0, 0)
    cache = lambda b, n: (b, layer, 0, 0)
    return pl.pallas_call(
        functools.partial(_lat_mix_kernel, layer),
        grid=(DEC_BATCH, DEC_SEQ // BLOCK),
        in_specs=[pl.BlockSpec(memory_space=pltpu.SMEM),
                  pl.BlockSpec((DEC_SEQ, ATTN_W), row),
                  pl.BlockSpec((DEC_SEQ, 2 * KV_W), row),
                  pl.BlockSpec((DEC_SEQ, FOURIER_W + 3 * CONV_W), row),
                  pl.BlockSpec((1, 1, SEQ, KV_W), cache),
                  pl.BlockSpec((1, 1, SEQ, KV_W), cache),
                  pl.BlockSpec((FOURIER_GROUP_W, 2 * FOURIER_GROUP_W), const),
                  pl.BlockSpec((DEC_SEQ, 2 * DEC_SEQ), const, pipeline_mode=pl.Buffered(1)),
                  pl.BlockSpec((1, 3, CONV_W), lambda b, n: (layer, 0, 0))],
        out_specs=pl.BlockSpec((DEC_SEQ, D_MODEL), lambda b, n: (b, 0)),
        out_shape=jax.ShapeDtypeStruct((T_LAT, D_MODEL), BF16),
        compiler_params=_cparams(("arbitrary", "arbitrary")),
        name="lat_mix",
    )(sink, q, kv, fm, cache_k, cache_v, ccsc, csl, conv_w)


def _first_index_of_max(vals, iota, sentinel):
    mx = jnp.max(vals, axis=0, keepdims=True)
    idx = jnp.min(jnp.where(vals == mx, iota, sentinel), axis=0, keepdims=True)
    return mx, idx


def _outproj_kernel(mixc_ref, mixl_ref, xc_ref, xl_ref, mod_ref, g_ref, w_ref, wr_ref, rb_ref, tri_ref,
                    x1_ref, h2_ref, ids_ref, wts_ref, rank_ref, cnt_ref, run_ref):
    i = pl.program_id(0)

    @pl.when(i == 0)
    def _():
        run_ref[...] = jnp.zeros_like(run_ref)

    gate1, shift2, scale2 = mod_ref[0, 0, 2:3, :], mod_ref[0, 0, 3:4, :], mod_ref[0, 0, 4:5, :]
    mix = jnp.where(i < N_CTX_TILES, mixc_ref[...], mixl_ref[...])
    x = jnp.where(i < N_CTX_TILES, xc_ref[...], xl_ref[...])
    x1 = x + gate1 * jnp.dot(mix, w_ref[0], preferred_element_type=F32)
    x1_ref[...] = x1
    h2 = _rmsnorm(x1, g_ref[0]) * (1.0 + scale2) + shift2
    _store_slabs(h2_ref, _pack_halves(h2))

    nt = (((1,), (1,)), ((), ()))
    h_hi = h2.astype(BF16)
    h_lo = (h2 - h_hi.astype(F32)).astype(BF16)
    part = lax.dot_general(wr_ref[0], h_hi, nt, preferred_element_type=F32)
    logits = (part[:N_EXPERTS] + part[N_EXPERTS:]
              + lax.dot_general(wr_ref[0, :N_EXPERTS, :], h_lo, nt, preferred_element_type=F32))
    scores = _sigmoid(logits)
    biased = scores + rb_ref[0]
    ninf = -jnp.inf
    mem = lax.broadcasted_iota(I32, (EXPERTS_PER_GROUP, TR), 0)
    grp_rows = []
    for g in range(N_GROUPS):
        bgp = biased[g * EXPERTS_PER_GROUP:(g + 1) * EXPERTS_PER_GROUP]
        m1, i1 = _first_index_of_max(bgp, mem, EXPERTS_PER_GROUP)
        m2 = jnp.max(jnp.where(mem == i1, ninf, bgp), axis=0, keepdims=True)
        grp_rows.append(m1 + m2)
    cur = jnp.concatenate(grp_rows, axis=0)
    gio = lax.broadcasted_iota(I32, (N_GROUPS, TR), 0)
    gsel = jnp.zeros((N_GROUPS, TR), F32)
    for _ in range(TOPK_GROUPS):
        _, gi = _first_index_of_max(cur, gio, N_GROUPS)
        hit = gio == gi
        gsel = jnp.where(hit, 1.0, gsel)
        cur = jnp.where(hit, ninf, cur)
    cur = jnp.concatenate(
        [jnp.where(gsel[g:g + 1] > 0.0, biased[g * EXPERTS_PER_GROUP:(g + 1) * EXPERTS_PER_GROUP], ninf)
         for g in range(N_GROUPS)], axis=0)
    eio = lax.broadcasted_iota(I32, (N_EXPERTS, TR), 0)
    chosen = jnp.zeros((N_EXPERTS, TR), F32)
    ids, sel = [], []
    for _ in range(TOP_K):
        _, ei = _first_index_of_max(cur, eio, N_EXPERTS)
        hit = eio == ei
        ids.append(ei)
        sel.append(jnp.sum(jnp.where(hit, scores, 0.0), axis=0, keepdims=True))
        chosen = jnp.where(hit, 1.0, chosen)
        cur = jnp.where(hit, ninf, cur)
    total = functools.reduce(lambda a, b: a + b, sel)
    ids_ref[...] = jnp.concatenate(ids, axis=0)
    wts_ref[...] = jnp.concatenate([s / total * ROUTE_SCALE for s in sel], axis=0)

    before = jnp.dot(chosen.astype(BF16), tri_ref[...], preferred_element_type=F32) + run_ref[:, 0:1]
    rank_ref[...] = jnp.concatenate(
        [jnp.sum(jnp.where(eio == ei, before, 0.0), axis=0, keepdims=True) for ei in ids], axis=0).astype(I32)
    run = run_ref[...] + jnp.sum(chosen, axis=1, keepdims=True)
    run_ref[...] = run
    cnt_ref[...] = run


def _outproj(layer, mix_ctx, mix_lat, x, mods, g_ffn, w_out_bf, wr_split, router_bias, tri):
    row = lambda i: (i, 0)
    col = lambda i: (0, i)
    const = lambda i: (0, 0)
    lay = lambda i: (layer, 0, 0)
    return pl.pallas_call(
        _outproj_kernel,
        grid=(N_TILES,),
        in_specs=[_CTX_ROWS, _LAT_ROWS, _CTX_ROWS, _LAT_ROWS,
                  _mod_spec(layer),
                  pl.BlockSpec((1, 1, D_MODEL), lay),
                  pl.BlockSpec((1, D_MODEL, D_MODEL), lay, pipeline_mode=pl.Buffered(1)),
                  pl.BlockSpec((1, 2 * N_EXPERTS, D_MODEL), lay),
                  pl.BlockSpec((1, N_EXPERTS, 1), lay),
                  pl.BlockSpec((TR, TR), const)],
        out_specs=[pl.BlockSpec((TR, D_MODEL), row),
                   pl.BlockSpec((TR * SLAB, LANES), row),
                   pl.BlockSpec((TOP_K, TR), col),
                   pl.BlockSpec((TOP_K, TR), col),
                   pl.BlockSpec((TOP_K, TR), col),
                   pl.BlockSpec((N_EXPERTS, META_LANES), const)],
        out_shape=[jax.ShapeDtypeStruct((T_ALL, D_MODEL), F32),
                   jax.ShapeDtypeStruct((T_ALL * SLAB, LANES), U32),
                   jax.ShapeDtypeStruct((TOP_K, T_ALL), I32),
                   jax.ShapeDtypeStruct((TOP_K, T_ALL), F32),
                   jax.ShapeDtypeStruct((TOP_K, T_ALL), I32),
                   jax.ShapeDtypeStruct((N_EXPERTS, META_LANES), F32)],
        scratch_shapes=[pltpu.VMEM((N_EXPERTS, META_LANES), F32)],
        compiler_params=_cparams(("arbitrary",)),
        name="outproj",
    )(mix_ctx, mix_lat, *x, mods, g_ffn.reshape(DEPTH, 1, D_MODEL), w_out_bf, wr_split,
      router_bias.reshape(DEPTH, N_EXPERTS, 1), tri)


def _positions_kernel(ids_ref, rank_ref, cnt_ref, pos_ref, te_ref, meta_ref):
    ids = ids_ref[...]
    pos = rank_ref[...]
    tile_io = lax.broadcasted_iota(I32, (1, TE_LANES), 1)
    lane_io = lax.broadcasted_iota(I32, (1, META_LANES), 1)
    te = jnp.zeros((1, TE_LANES), I32)
    starts = jnp.zeros((1, META_LANES), I32)
    counts = jnp.zeros((1, META_LANES), I32)
    tile_ends = jnp.zeros((1, META_LANES), I32)
    parities = jnp.zeros((1, META_LANES), I32)
    start = jnp.zeros((1, 1), I32)
    n_seg = jnp.zeros((1, 1), I32)
    for e in range(N_EXPERTS):
        c = cnt_ref[e:e + 1, 0:1].astype(I32)
        pos = pos + jnp.where(ids == e, start, 0)
        starts = starts + jnp.where(lane_io == e, start, 0)
        counts = counts + jnp.where(lane_io == e, c, 0)
        parities = parities + jnp.where(lane_io == e, n_seg % 2, 0)
        start = start + ((c + (TM - 1)) // TM) * TM
        n_seg = n_seg + jnp.where(c > 0, 1, 0)
        tile_ends = tile_ends + jnp.where(lane_io == e, start // TM, 0)
        te = te + jnp.where(tile_io * TM >= start, 1, 0)
    pos_ref[...] = pos
    te_ref[...] = jnp.minimum(te, N_EXPERTS - 1)
    meta_ref[0:1, :] = starts
    meta_ref[1:2, :] = counts
    meta_ref[2:3, :] = jnp.broadcast_to(start // TM, (1, META_LANES))
    meta_ref[3:4, :] = tile_ends
    meta_ref[4:5, :] = parities
    meta_ref[5:8, :] = jnp.zeros((3, META_LANES), I32)


def _positions(ids, rank, cnt):
    return pl.pallas_call(
        _positions_kernel,
        out_shape=[jax.ShapeDtypeStruct((TOP_K, T_ALL), I32),
                   jax.ShapeDtypeStruct((1, TE_LANES), I32),
                   jax.ShapeDtypeStruct((8, META_LANES), I32)],
        compiler_params=_cparams(),
        name="positions",
    )(ids, rank, cnt)


SC_CORES, SC_SUBCORES = 2, 16
SC_WORKERS = SC_CORES * SC_SUBCORES
SC_CHUNK = 32


def _sc_kernel(body, out_type, n_idx_rows, name):
    return pl.kernel(
        body,
        out_type=out_type,
        mesh=plsc.VectorSubcoreMesh(core_axis_name="c", subcore_axis_name="s",
                                    num_cores=SC_CORES, num_subcores=SC_SUBCORES),
        scratch_types=[pltpu.VMEM((n_idx_rows, SC_CHUNK), I32),
                       pltpu.VMEM((2, SC_CHUNK, SLAB, LANES), U32),
                       pltpu.SemaphoreType.DMA,
                       pltpu.SemaphoreType.DMA((2,)),
                       pltpu.SemaphoreType.DMA((2,))],
        compiler_params=pltpu.CompilerParams(use_tc_tiling_on_sc=True),
        name=name,
    )


def _sc_worker():
    return lax.axis_index("s") * SC_CORES + lax.axis_index("c")


def _sc_load_indices(n_rows, src_of_row, idx_v, sem):
    @pl.loop(0, n_rows)
    def _(r):
        src, dst = src_of_row(r), idx_v.at[r]
        pltpu.make_async_copy(src, dst, sem).start()

    @pl.loop(0, n_rows)
    def _(r):
        src, dst = src_of_row(r), idx_v.at[r]
        pltpu.make_async_copy(src, dst, sem).wait()


def _sc_two_slot_pipeline(n_chunks, fill, fill_wait, drain, drain_wait):
    assert n_chunks % 2 == 0 and n_chunks >= 4
    fill(0, 0)

    @pl.loop(0, n_chunks // 2)
    def _(pair):
        c = 2 * pair

        @pl.when(pair >= 1)
        def _():
            drain_wait(c - 1, 1)

        fill(c + 1, 1)
        fill_wait(c, 0)
        drain(c, 0)

        @pl.when(pair < n_chunks // 2 - 1)
        def _():
            drain_wait(c, 0)
            fill(c + 2, 0)

        fill_wait(c + 1, 1)
        drain(c + 1, 1)

    drain_wait(n_chunks - 2, 0)
    drain_wait(n_chunks - 1, 1)


def _dispatch(pos_flat, h3):
    n_chunks = T_ALL // SC_WORKERS // SC_CHUNK

    def body(pos_hbm, h_hbm, xs_hbm, idx_v, rows_v, idx_sem, fill_sem, drain_sem):
        t_base = _sc_worker() * (n_chunks * SC_CHUNK)

        def index_src(r):
            t0 = t_base + (r // TOP_K) * SC_CHUNK
            return pos_hbm.at[pl.ds(pl.multiple_of((r % TOP_K) * T_ALL + t0, SC_CHUNK), SC_CHUNK)]

        _sc_load_indices(n_chunks * TOP_K, index_src, idx_v, idx_sem)

        def load(c, slot):
            t0 = pl.multiple_of(t_base + c * SC_CHUNK, SC_CHUNK)
            return pltpu.make_async_copy(h_hbm.at[pl.ds(t0, SC_CHUNK)], rows_v.at[slot], fill_sem.at[slot])

        def scatter(c, slot, k):
            return pltpu.make_async_copy(rows_v.at[slot], xs_hbm.at[idx_v.at[c * TOP_K + k]], drain_sem.at[slot])

        def drain(c, slot):
            for k in range(TOP_K):
                scatter(c, slot, k).start()

        def drain_wait(c, slot):
            for k in range(TOP_K):
                scatter(c, slot, k).wait()

        _sc_two_slot_pipeline(n_chunks, lambda c, s: load(c, s).start(), lambda c, s: load(c, s).wait(),
                              drain, drain_wait)

    return _sc_kernel(body, jax.ShapeDtypeStruct((P_SORT, SLAB, LANES), U32), n_chunks * TOP_K,
                      "sc_dispatch")(pos_flat, h3)


def _gather(pos_flat, y3):
    n_items = pos_flat.shape[0]
    n_chunks = n_items // SC_WORKERS // SC_CHUNK

    def body(pos_hbm, y_hbm, out_hbm, idx_v, rows_v, idx_sem, fill_sem, drain_sem):
        base = _sc_worker() * (n_chunks * SC_CHUNK)

        def index_src(r):
            return pos_hbm.at[pl.ds(pl.multiple_of(base + r * SC_CHUNK, SC_CHUNK), SC_CHUNK)]

        _sc_load_indices(n_chunks, index_src, idx_v, idx_sem)

        def gather(c, slot):
            return pltpu.make_async_copy(y_hbm.at[idx_v.at[c]], rows_v.at[slot], fill_sem.at[slot])

        def store(c, slot):
            i0 = pl.multiple_of(base + c * SC_CHUNK, SC_CHUNK)
            return pltpu.make_async_copy(rows_v.at[slot], out_hbm.at[pl.ds(i0, SC_CHUNK)], drain_sem.at[slot])

        _sc_two_slot_pipeline(n_chunks, lambda c, s: gather(c, s).start(), lambda c, s: gather(c, s).wait(),
                              lambda c, s: store(c, s).start(), lambda c, s: store(c, s).wait())

    return _sc_kernel(body, jax.ShapeDtypeStruct((n_items, SLAB, LANES), U32), n_chunks,
                      "sc_gather")(pos_flat, y3)


def _experts_kernel(layer, te_ref, nt_ref, end_ref, tend_ref, par_ref, x_ref, wg_hbm, wu_hbm, wd_hbm, y_ref,
                    wg_f, wu_f, wd_f, wg_bf, wu_bf, wd_bf, sem):
    j = pl.program_id(0)
    used = j < nt_ref[0]
    e = te_ref[j]
    first = (j == 0) | (te_ref[jnp.maximum(j - 1, 0)] != e)
    slot = par_ref[e]

    def fetch(expert, s):
        return [pltpu.make_async_copy(src.at[layer, expert], dst.at[s], sem.at[s])
                for src, dst in ((wg_hbm, wg_f), (wu_hbm, wu_f), (wd_hbm, wd_f))]

    @pl.when(used & (j == 0))
    def _():
        for cp in fetch(e, slot):
            cp.start()

    @pl.when(used & first)
    def _():
        for cp in fetch(e, slot):
            cp.wait()
        wg_bf[...] = wg_f[slot].astype(BF16)
        wu_bf[...] = wu_f[slot].astype(BF16)
        wd_bf[...] = wd_f[slot].astype(BF16)
        nxt = tend_ref[e]

        @pl.when(nxt < nt_ref[0])
        def _():
            for cp in fetch(te_ref[jnp.minimum(nxt, pl.num_programs(0) - 1)], 1 - slot):
                cp.start()

    @pl.when(used)
    def _():
        valid = end_ref[te_ref[j]] - j * TM
        rows = lax.broadcasted_iota(I32, (TM, HALF), 0)
        words = jnp.where(rows < valid, _load_slabs(x_ref, TM), jnp.uint32(0))
        xa, xb = _unpack_halves(words)
        xa, xb = xa.astype(BF16), xb.astype(BF16)
        g = (jnp.dot(xa, wg_bf[:HALF], preferred_element_type=F32)
             + jnp.dot(xb, wg_bf[HALF:], preferred_element_type=F32))
        u = (jnp.dot(xa, wu_bf[:HALF], preferred_element_type=F32)
             + jnp.dot(xb, wu_bf[HALF:], preferred_element_type=F32))
        a = (_silu(g) * u).astype(BF16)
        _store_slabs(y_ref, _pack_halves(jnp.dot(a, wd_bf[...], preferred_element_type=F32)))


def _experts(layer, te, nt, end, tile_end, parity, xs, w_gate, w_up, w_down):
    n_tiles = xs.shape[0] // (TM * SLAB)
    tile = lambda j, te, nt, *_: (jnp.minimum(j, nt[0] - 1), 0)
    return pl.pallas_call(
        functools.partial(_experts_kernel, layer),
        grid_spec=pltpu.PrefetchScalarGridSpec(
            num_scalar_prefetch=5,
            grid=(n_tiles,),
            in_specs=[pl.BlockSpec((TM * SLAB, LANES), tile),
                      pl.BlockSpec(memory_space=pl.ANY),
                      pl.BlockSpec(memory_space=pl.ANY),
                      pl.BlockSpec(memory_space=pl.ANY)],
            out_specs=pl.BlockSpec((TM * SLAB, LANES), tile),
            scratch_shapes=[pltpu.VMEM((2, D_MODEL, EXPERT_FF), F32),
                            pltpu.VMEM((2, D_MODEL, EXPERT_FF), F32),
                            pltpu.VMEM((2, EXPERT_FF, D_MODEL), F32),
                            pltpu.VMEM((D_MODEL, EXPERT_FF), BF16),
                            pltpu.VMEM((D_MODEL, EXPERT_FF), BF16),
                            pltpu.VMEM((EXPERT_FF, D_MODEL), BF16),
                            pltpu.SemaphoreType.DMA((2,))]),
        out_shape=jax.ShapeDtypeStruct(xs.shape, U32),
        compiler_params=_cparams(("arbitrary",)),
        name="experts",
    )(te, nt, end, tile_end, parity, xs, w_gate, w_up, w_down)


def _combine_kernel(final, wts_ref, x1_ref, ysh_ref, mod_ref, g_ref, buf_ref, *rest):
    o_ref, x2_ref, wb_ref = rest
    w = jnp.transpose(wts_ref[...])
    for k in range(TOP_K):
        wb_ref[k] = jnp.broadcast_to(w[:, k:k + 1], (TC_ROWS, LANES))
    sq = jnp.zeros((TC_ROWS, LANES), F32)
    for s in range(SLAB):
        acc_a, acc_b = _unpack_halves(ysh_ref[pl.ds(s, TC_ROWS, stride=SLAB), :])
        for k in range(TOP_K):
            ya, yb = _unpack_halves(buf_ref[k, pl.ds(s, TC_ROWS, stride=SLAB), :])
            acc_a = acc_a + wb_ref[k] * ya
            acc_b = acc_b + wb_ref[k] * yb
        for half, acc in ((0, acc_a), (1, acc_b)):
            cols = slice(half * HALF + s * LANES, half * HALF + (s + 1) * LANES)
            x2 = x1_ref[:, cols] + mod_ref[0, 0, 5:6, cols] * acc
            x2_ref[:, cols] = x2
            if final:
                sq = sq + x2 * x2
    out = x2_ref[...]
    if final:
        ms = jnp.sum(sq, axis=-1, keepdims=True) * (1.0 / D_MODEL)
        out = out * lax.rsqrt(ms + EPS) * g_ref[...]

    o_ref[...] = out


def _combine(layer, row0, n_rows, wts, x1, ysh, mods, final_norm, yg):
    final = layer == DEPTH - 1
    t0 = row0 // TC_ROWS
    row = lambda i: (t0 + i, 0)
    return pl.pallas_call(
        functools.partial(_combine_kernel, final),
        grid=(n_rows // TC_ROWS,),
        in_specs=[pl.BlockSpec((TOP_K, TC_ROWS), lambda i: (0, t0 + i)),
                  pl.BlockSpec((TC_ROWS, D_MODEL), row),
                  pl.BlockSpec((TC_ROWS * SLAB, LANES), row),
                  pl.BlockSpec((1, 1, N_MOD, D_MODEL),
                               lambda i: (layer, _cond_row((t0 + i) // (TR // TC_ROWS)), 0, 0)),
                  pl.BlockSpec((1, D_MODEL), lambda i: (0, 0)),
                  pl.BlockSpec((TOP_K, TC_ROWS * SLAB, LANES), lambda i: (0, i, 0))],
        out_specs=pl.BlockSpec((TC_ROWS, D_MODEL), lambda i: (i, 0)),
        out_shape=jax.ShapeDtypeStruct((n_rows, D_MODEL), F32),
        scratch_shapes=[pltpu.VMEM((TC_ROWS, D_MODEL), F32),
                        pltpu.VMEM((TOP_K, TC_ROWS, LANES), F32)],
        compiler_params=_cparams(("arbitrary",)),
        name="combine",
    )(wts, x1, ysh, mods, final_norm.reshape(1, D_MODEL), yg)


def kernel(x_prompt, x_sample, cache_k, cache_v, c, c_ctx, w_mod, b_mod, g_mix, g_ffn, w_in, w_out, conv_w,
           attn_sink, w_router, router_bias, w_gate_e, w_up_e, w_down_e, w_gate_s, w_up_s, w_down_s, final_norm):
    ccsc = jnp.asarray(_channel_dft(), BF16)
    csl_ctx = jnp.asarray(_dft_tables(SEQ), BF16)
    csl_lat = jnp.asarray(_dft_tables(DEC_SEQ), BF16)
    rope = tuple(jnp.asarray(t, F32) for t in _rope_tables(DEC_SEQ, TR))
    tri = jnp.asarray(np.triu(np.ones((TR, TR)), 1), BF16)
    shared_te = jnp.zeros((T_ALL // TM,), I32)
    shared_nt = jnp.full((1,), T_ALL // TM, I32)
    shared_end = jnp.full((1,), T_ALL, I32)

    cond = jnp.concatenate([c_ctx[None], c, jnp.zeros((N_COND - 1 - DEC_BATCH, D_MODEL), F32)], axis=0)
    mods = _mods(cond, w_mod, b_mod).reshape(DEPTH, N_COND, N_MOD, D_MODEL)
    cache_k = cache_k.reshape(DEC_BATCH, DEPTH, SEQ, KV_W)
    cache_v = cache_v.reshape(DEC_BATCH, DEPTH, SEQ, KV_W)
    w_in_bf, w_out_bf = w_in.astype(BF16), w_out.astype(BF16)
    wr_t = jnp.swapaxes(w_router, 1, 2)
    wr_hi = wr_t.astype(BF16)
    wr_split = jnp.concatenate([wr_hi, (wr_t - wr_hi.astype(F32)).astype(BF16)], axis=1)
    ws_gate, ws_up, ws_down = w_gate_s[:, None], w_up_s[:, None], w_down_s[:, None]
    x = (x_prompt.reshape(T_CTX, D_MODEL), x_sample.reshape(T_LAT, D_MODEL))

    new_kv = ()
    for l in range(DEPTH):
        q, kv, fm, *new_kv = _inproj(l, x, mods, g_mix, w_in_bf, rope, tuple(new_kv))
        mix_ctx = _ctx_mix(l, attn_sink, q, kv, fm, ccsc, csl_ctx, conv_w)
        mix_lat = _lat_mix(l, attn_sink, q, kv, fm, cache_k, cache_v, ccsc, csl_lat, conv_w)
        x1, h2, ids, wts, rank, cnt = _outproj(l, mix_ctx, mix_lat, x, mods, g_ffn, w_out_bf, wr_split,
                                               router_bias, tri)
        pos, te, meta = _positions(ids, rank, cnt)
        pos_flat = pos.reshape(TOP_K * T_ALL)
        xs = _dispatch(pos_flat, h2.reshape(T_ALL, SLAB, LANES)).reshape(P_SORT * SLAB, LANES)
        y = _experts(l, te[0, :N_SORT_TILES], meta[2, :1], meta[0, :N_EXPERTS] + meta[1, :N_EXPERTS],
                     meta[3, :N_EXPERTS], meta[4, :N_EXPERTS], xs, w_gate_e, w_up_e, w_down_e)
        ysh = _experts(l, shared_te, shared_nt, shared_end, shared_nt, shared_te[:1], h2,
                       ws_gate, ws_up, ws_down)
        y3 = y.reshape(P_SORT, SLAB, LANES)
        x = []
        for row0, n_rows in ((0, T_CTX), (T_CTX, T_LAT)):
            part_pos = pos[:, row0:row0 + n_rows].reshape(TOP_K * n_rows)
            yg = _gather(part_pos, y3).reshape(TOP_K, n_rows * SLAB, LANES)
            x.append(_combine(l, row0, n_rows, wts, x1, ysh, mods, final_norm, yg))

    y_prompt = x[0].reshape(BATCH, SEQ, D_MODEL)
    y_sample = x[1].reshape(DEC_BATCH, DEC_SEQ, D_MODEL)
    new_k, new_v = new_kv
    return y_prompt, y_sample, new_k, new_v
```

```python
import functools

import numpy as np
import jax
import jax.numpy as jnp
from jax import lax
from jax.experimental import pallas as pl
from jax.experimental.pallas import tpu as pltpu
from jax.experimental.pallas import tpu_sc as plsc

F32 = jnp.float32
BF16 = jnp.bfloat16
I32 = jnp.int32
U32 = jnp.uint32

D_MODEL = 2048
HALF = D_MODEL // 2
BATCH, SEQ = 32, 256
DEC_BATCH, DEC_SEQ = 4, 1024
DEPTH = 2
GRID_W = 64
HEAD_DIM = 128
N_HEADS, N_KV_HEADS, Q_PER_KV = 8, 2, 4
ATTN_W, KV_W = 1024, 256
WINDOW, BLOCK = 128, 128
SCALE = HEAD_DIM ** -0.5
ROPE_BASE = 10000.0
ROPE_PAIRS = 32
FOURIER_W, FOURIER_GROUPS, FOURIER_GROUP_W = 512, 4, 128
CONV_W = 512
IN_W = 3584
N_EXPERTS, TOP_K, N_GROUPS, EXPERTS_PER_GROUP, TOPK_GROUPS = 64, 8, 8, 8, 4
EXPERT_FF = 512
ROUTE_SCALE = 2.5
N_MOD = 6
EPS = 1e-6
NEG = -1e30

T_CTX = BATCH * SEQ
T_LAT = DEC_BATCH * DEC_SEQ
T_ALL = T_CTX + T_LAT
TR = 512
N_CTX_TILES = T_CTX // TR
LAT_TILES_PER_SEQ = DEC_SEQ // TR
N_TILES = T_ALL // TR
N_COND = 8
MOD_TN = 1024
TM = 512
TC_ROWS = 256
N_SORT_TILES = T_ALL * TOP_K // TM + N_EXPERTS
P_SORT = N_SORT_TILES * TM
META_LANES = 128
TE_LANES = 512
VMEM_LIMIT = 56 * 1024 * 1024


def _cparams(sem=None):
    return pltpu.CompilerParams(dimension_semantics=sem, vmem_limit_bytes=VMEM_LIMIT)


def _silu(x):
    return x / (1.0 + jnp.exp(-x))


def _sigmoid(x):
    return 1.0 / (1.0 + jnp.exp(-x))


def _rmsnorm(x, g):
    return x * lax.rsqrt(jnp.mean(x * x, axis=-1, keepdims=True) + EPS) * g


def _cond_row(i):
    return jnp.where(i < N_CTX_TILES, 0, 1 + (i - N_CTX_TILES) // LAT_TILES_PER_SEQ)


def _pack_halves(v):
    n = v.shape[1] // 2
    hi = pltpu.bitcast(v[:, :n].astype(BF16).astype(F32), U32)
    lo = pltpu.bitcast(v[:, n:].astype(BF16).astype(F32), U32)
    return hi | (lo >> 16)


def _unpack_halves(w):
    return pltpu.bitcast(w & jnp.uint32(0xFFFF0000), F32), pltpu.bitcast(w << 16, F32)


SLAB, LANES = 8, 128


def _load_slabs(ref, n, row0=0):
    return jnp.concatenate([ref[pl.ds(row0 * SLAB + s, n, stride=SLAB), :] for s in range(SLAB)], axis=1)


def _store_slabs(ref, words, row0=0):
    n = words.shape[0]
    for s in range(SLAB):
        ref[pl.ds(row0 * SLAB + s, n, stride=SLAB), :] = words[:, s * LANES:(s + 1) * LANES]


def _dft_tables(L):
    j = np.arange(L, dtype=np.int64)
    ang = 2.0 * np.pi * ((j[:, None] * j[None, :]) % L).astype(np.float64) / L
    c, s = np.cos(ang) / np.sqrt(L), np.sin(ang) / np.sqrt(L)
    return np.concatenate([c, -s], axis=1)


def _channel_dft():
    n = FOURIER_GROUP_W
    j = np.arange(n, dtype=np.int64)
    ang = 2.0 * np.pi * ((j[:, None] * j[None, :]) % n).astype(np.float64) / n
    return np.concatenate([np.cos(ang), np.sin(ang)], axis=1) / np.sqrt(n)


def _rope_tables(S, n_identity):
    pos = np.arange(S + n_identity)
    pos[S:] = 0
    row, col = (pos // GRID_W).astype(np.float64), (pos % GRID_W).astype(np.float64)
    inv = ROPE_BASE ** (-np.arange(ROPE_PAIRS, dtype=np.float64) / ROPE_PAIRS)
    ar, ac = row[:, None] * inv, col[:, None] * inv
    z = np.zeros_like(ar)
    cos = np.concatenate([np.cos(ar), np.cos(ar), np.cos(ac), np.cos(ac)], axis=1)
    sin_hi = np.concatenate([-np.sin(ar), z, -np.sin(ac), z], axis=1)
    sin_lo = np.concatenate([z, np.sin(ar), z, np.sin(ac)], axis=1)
    return cos, sin_hi, sin_lo


def _mods_kernel(cond_ref, w_ref, b_ref, o_ref):
    s = _silu(cond_ref[...]).astype(BF16)
    o_ref[0] = jnp.dot(s, w_ref[0].astype(BF16), preferred_element_type=F32) + b_ref[0]


def _mods(cond, w_mod, b_mod):
    n = N_MOD * D_MODEL
    return pl.pallas_call(
        _mods_kernel,
        grid=(DEPTH, n // MOD_TN),
        in_specs=[pl.BlockSpec((N_COND, D_MODEL), lambda l, j: (0, 0)),
                  pl.BlockSpec((1, D_MODEL, MOD_TN), lambda l, j: (l, 0, j)),
                  pl.BlockSpec((1, 1, MOD_TN), lambda l, j: (l, 0, j))],
        out_specs=pl.BlockSpec((1, N_COND, MOD_TN), lambda l, j: (l, 0, j)),
        out_shape=jax.ShapeDtypeStruct((DEPTH, N_COND, n), F32),
        compiler_params=_cparams(("arbitrary", "arbitrary")),
        name="mods",
    )(cond, w_mod, b_mod.reshape(DEPTH, 1, n))


_CTX_ROWS = pl.BlockSpec((TR, D_MODEL), lambda i: (jnp.minimum(i, N_CTX_TILES - 1), 0))
_LAT_ROWS = pl.BlockSpec((TR, D_MODEL), lambda i: (jnp.maximum(i - N_CTX_TILES, 0), 0))


def _mod_spec(layer):
    return pl.BlockSpec((1, 1, N_MOD, D_MODEL), lambda i: (layer, _cond_row(i), 0, 0))


def _inproj_kernel(n_alias, xc_ref, xl_ref, mod_ref, g_ref, w_ref, cos_ref, shi_ref, slo_ref, *rest):
    q_ref, kv_ref, fm_ref, nk_ref, nv_ref = rest[n_alias:]
    i = pl.program_id(0)
    shift, scale = mod_ref[0, 0, 0:1, :], mod_ref[0, 0, 1:2, :]
    for r0 in range(0, TR, SEQ):
        rows = slice(r0, r0 + SEQ)
        x = jnp.where(i < N_CTX_TILES, xc_ref[rows, :], xl_ref[rows, :])
        h = (_rmsnorm(x, g_ref[0]) * (1.0 + scale) + shift).astype(BF16)
        fm_ref[rows, :] = jnp.dot(h, w_ref[0, :, ATTN_W + 2 * KV_W:], preferred_element_type=F32).astype(BF16)
        qkv = jnp.dot(h, w_ref[0, :, :ATTN_W + 2 * KV_W], preferred_element_type=F32)
        kv_ref[rows, KV_W:] = qkv[:, ATTN_W + KV_W:]
        cos, shi, slo = cos_ref[rows, :], shi_ref[rows, :], slo_ref[rows, :]
        for hd in range(N_HEADS + N_KV_HEADS):
            xh = qkv[:, hd * HEAD_DIM:(hd + 1) * HEAD_DIM]
            r = xh * cos + pltpu.roll(xh, HEAD_DIM - 32, 1) * shi + pltpu.roll(xh, 32, 1) * slo
            if hd < N_HEADS:
                q_ref[rows, hd * HEAD_DIM:(hd + 1) * HEAD_DIM] = r.astype(BF16)
            else:
                kv_ref[rows, (hd - N_HEADS) * HEAD_DIM:(hd - N_HEADS + 1) * HEAD_DIM] = r

    @pl.when(i < N_CTX_TILES)
    def _():
        for b in range(TR // SEQ):
            for hd in range(N_KV_HEADS):
                c0 = hd * HEAD_DIM
                nk_ref[b, 0, :, hd, :] = kv_ref[b * SEQ:(b + 1) * SEQ, c0:c0 + HEAD_DIM]
                nv_ref[b, 0, :, hd, :] = kv_ref[b * SEQ:(b + 1) * SEQ, KV_W + c0:KV_W + c0 + HEAD_DIM]


def _inproj(layer, x, mods, g_mix, w_in_bf, rope, new_kv):
    lat_blk = lambda i: (jnp.where(i < N_CTX_TILES, LAT_TILES_PER_SEQ, (i - N_CTX_TILES) % LAT_TILES_PER_SEQ), 0)
    row = lambda i: (i, 0)
    kv_blk = pl.BlockSpec((TR // SEQ, 1, SEQ, N_KV_HEADS, HEAD_DIM),
                          lambda i: (jnp.minimum(i, N_CTX_TILES - 1), layer, 0, 0, 0))
    kv_shape = jax.ShapeDtypeStruct((BATCH, DEPTH, SEQ, N_KV_HEADS, HEAD_DIM), F32)
    n_in = 8
    return pl.pallas_call(
        functools.partial(_inproj_kernel, len(new_kv)),
        grid=(N_TILES,),
        in_specs=[_CTX_ROWS, _LAT_ROWS,
                  _mod_spec(layer),
                  pl.BlockSpec((1, 1, D_MODEL), lambda i: (layer, 0, 0)),
                  pl.BlockSpec((1, D_MODEL, IN_W), lambda i: (layer, 0, 0), pipeline_mode=pl.Buffered(1)),
                  pl.BlockSpec((TR, HEAD_DIM), lat_blk),
                  pl.BlockSpec((TR, HEAD_DIM), lat_blk),
                  pl.BlockSpec((TR, HEAD_DIM), lat_blk)] + [pl.BlockSpec(memory_space=pl.ANY)] * len(new_kv),
        out_specs=[pl.BlockSpec((TR, ATTN_W), row),
                   pl.BlockSpec((TR, 2 * KV_W), row),
                   pl.BlockSpec((TR, FOURIER_W + 3 * CONV_W), row),
                   kv_blk, kv_blk],
        out_shape=[jax.ShapeDtypeStruct((T_ALL, ATTN_W), BF16),
                   jax.ShapeDtypeStruct((T_ALL, 2 * KV_W), F32),
                   jax.ShapeDtypeStruct((T_ALL, FOURIER_W + 3 * CONV_W), BF16),
                   kv_shape, kv_shape],
        input_output_aliases={n_in + a: 3 + a for a in range(len(new_kv))},
        compiler_params=_cparams(("arbitrary",)),
        name="inproj",
    )(*x, mods, g_mix.reshape(DEPTH, 1, D_MODEL), w_in_bf, *rope, *new_kv)


def _fourier_conv(fm_ref, ccsc_ref, csl_ref, cw_ref, o_ref):
    L = fm_ref.shape[0]
    for g in range(FOURIER_GROUPS):
        xg = fm_ref[:, g * FOURIER_GROUP_W:(g + 1) * FOURIER_GROUP_W]
        z = jnp.dot(xg, ccsc_ref[...], preferred_element_type=F32)
        zz = jnp.concatenate([z[:, :FOURIER_GROUP_W], z[:, FOURIER_GROUP_W:]], axis=0).astype(BF16)
        f = jnp.dot(csl_ref[...], zz, preferred_element_type=F32)
        o_ref[:, ATTN_W + g * FOURIER_GROUP_W:ATTN_W + (g + 1) * FOURIER_GROUP_W] = f.astype(BF16)
    xc = fm_ref[:, FOURIER_W:FOURIER_W + CONV_W].astype(F32)
    bg = fm_ref[:, FOURIER_W + CONV_W:FOURIER_W + 2 * CONV_W].astype(F32)
    cg = fm_ref[:, FOURIER_W + 2 * CONV_W:].astype(F32)
    u = cg * xc
    rows = lax.broadcasted_iota(I32, u.shape, 0)
    prev = jnp.where(rows == 0, 0.0, pltpu.roll(u, 1, 0))
    nxt = jnp.where(rows == L - 1, 0.0, pltpu.roll(u, L - 1, 0))
    y = prev * cw_ref[0, 0:1, :] + u * cw_ref[0, 1:2, :] + nxt * cw_ref[0, 2:3, :]
    o_ref[:, ATTN_W + FOURIER_W:] = (bg * y).astype(BF16)


def _ctx_mix_kernel(layer, sink_ref, q_ref, kv_ref, fm_ref, ccsc_ref, csl_ref, cw_ref, o_ref):
    for h in range(N_HEADS):
        g = h // Q_PER_KV
        q = q_ref[:, h * HEAD_DIM:(h + 1) * HEAD_DIM]
        k = kv_ref[:, g * HEAD_DIM:(g + 1) * HEAD_DIM].astype(BF16)
        v = kv_ref[:, KV_W + g * HEAD_DIM:KV_W + (g + 1) * HEAD_DIM].astype(BF16)
        s = lax.dot_general(q, k, (((1,), (1,)), ((), ())), preferred_element_type=F32) * SCALE
        sink = sink_ref[layer, h]
        m = jnp.maximum(jnp.max(s, axis=-1, keepdims=True), sink)
        e = jnp.exp(s - m)
        den = jnp.sum(e, axis=-1, keepdims=True) + jnp.exp(sink - m)
        o = jnp.dot(e.astype(BF16), v, preferred_element_type=F32) / den
        o_ref[:, h * HEAD_DIM:(h + 1) * HEAD_DIM] = o.astype(BF16)
    _fourier_conv(fm_ref, ccsc_ref, csl_ref, cw_ref, o_ref)


def _ctx_mix(layer, sink, q, kv, fm, ccsc, csl, conv_w):
    row = lambda b: (b, 0)
    const = lambda b: (0, 0)
    return pl.pallas_call(
        functools.partial(_ctx_mix_kernel, layer),
        grid=(BATCH,),
        in_specs=[pl.BlockSpec(memory_space=pltpu.SMEM),
                  pl.BlockSpec((SEQ, ATTN_W), row),
                  pl.BlockSpec((SEQ, 2 * KV_W), row),
                  pl.BlockSpec((SEQ, FOURIER_W + 3 * CONV_W), row),
                  pl.BlockSpec((FOURIER_GROUP_W, 2 * FOURIER_GROUP_W), const),
                  pl.BlockSpec((SEQ, 2 * SEQ), const),
                  pl.BlockSpec((1, 3, CONV_W), lambda b: (layer, 0, 0))],
        out_specs=pl.BlockSpec((SEQ, D_MODEL), row),
        out_shape=jax.ShapeDtypeStruct((T_CTX, D_MODEL), BF16),
        compiler_params=_cparams(("arbitrary",)),
        name="ctx_mix",
    )(sink, q, kv, fm, ccsc, csl, conv_w)


def _lat_mix_kernel(layer, sink_ref, q_ref, kv_ref, fm_ref, ck_ref, cv_ref, ccsc_ref, csl_ref, cw_ref, o_ref):
    n = pl.program_id(1)
    nb = DEC_SEQ // BLOCK
    rows = Q_PER_KV * BLOCK
    band = 3 * BLOCK

    @pl.when(n == 0)
    def _():
        _fourier_conv(fm_ref, ccsc_ref, csl_ref, cw_ref, o_ref)

    q0 = pl.multiple_of(n * BLOCK, BLOCK)
    k0 = pl.multiple_of(jnp.clip(n - 1, 0, nb - 3) * BLOCK, BLOCK)
    qpos = q0 + lax.broadcasted_iota(I32, (BLOCK, band), 0)
    kpos = k0 + lax.broadcasted_iota(I32, (BLOCK, band), 1)
    valid = jnp.abs(qpos - kpos) <= WINDOW
    for g in range(N_KV_HEADS):
        kb = kv_ref[pl.ds(k0, band), g * HEAD_DIM:(g + 1) * HEAD_DIM].astype(BF16)
        vb = kv_ref[pl.ds(k0, band), KV_W + g * HEAD_DIM:KV_W + (g + 1) * HEAD_DIM].astype(BF16)
        k_ctx = ck_ref[0, 0, :, g * HEAD_DIM:(g + 1) * HEAD_DIM].astype(BF16)
        v_ctx = cv_ref[0, 0, :, g * HEAD_DIM:(g + 1) * HEAD_DIM].astype(BF16)
        for h in range(g * Q_PER_KV, (g + 1) * Q_PER_KV):
            sink = sink_ref[layer, h]
            q = q_ref[pl.ds(q0, BLOCK), h * HEAD_DIM:(h + 1) * HEAD_DIM]
            s_loc = lax.dot_general(q, kb, (((1,), (1,)), ((), ())), preferred_element_type=F32) * SCALE
            s_loc = jnp.where(valid, s_loc, NEG)
            s_ctx = lax.dot_general(q, k_ctx, (((1,), (1,)), ((), ())), preferred_element_type=F32) * SCALE
            m = jnp.maximum(jnp.maximum(jnp.max(s_loc, axis=-1, keepdims=True),
                                        jnp.max(s_ctx, axis=-1, keepdims=True)), sink)
            e_loc, e_ctx = jnp.exp(s_loc - m), jnp.exp(s_ctx - m)
            den = (jnp.sum(e_loc, axis=-1, keepdims=True) + jnp.sum(e_ctx, axis=-1, keepdims=True)
                   + jnp.exp(sink - m))
            o = (jnp.dot(e_loc.astype(BF16), vb, preferred_element_type=F32)
                 + jnp.dot(e_ctx.astype(BF16), v_ctx, preferred_element_type=F32)) / den
            o_ref[pl.ds(q0, BLOCK), h * HEAD_DIM:(h + 1) * HEAD_DIM] = o.astype(BF16)


def _lat_mix(layer, sink, q, kv, fm, cache_k, cache_v, ccsc, csl, conv_w):
    off = T_CTX // DEC_SEQ
    row = lambda b, n: (off + b, 0)
    const = lambda b, n: (0, 0)
    cache = lambda b, n: (b, layer, 0, 0)
    return pl.pallas_call(
        functools.partial(_lat_mix_kernel, layer),
        grid=(DEC_BATCH, DEC_SEQ // BLOCK),
        in_specs=[pl.BlockSpec(memory_space=pltpu.SMEM),
                  pl.BlockSpec((DEC_SEQ, ATTN_W), row),
                  pl.BlockSpec((DEC_SEQ, 2 * KV_W), row),
                  pl.BlockSpec((DEC_SEQ, FOURIER_W + 3 * CONV_W), row),
                  pl.BlockSpec((1, 1, SEQ, KV_W), cache),
                  pl.BlockSpec((1, 1, SEQ, KV_W), cache),
                  pl.BlockSpec((FOURIER_GROUP_W, 2 * FOURIER_GROUP_W), const),
                  pl.BlockSpec((DEC_SEQ, 2 * DEC_SEQ), const, pipeline_mode=pl.Buffered(1)),
                  pl.BlockSpec((1, 3, CONV_W), lambda b, n: (layer, 0, 0))],
        out_specs=pl.BlockSpec((DEC_SEQ, D_MODEL), lambda b, n: (b, 0)),
        out_shape=jax.ShapeDtypeStruct((T_LAT, D_MODEL), BF16),
        compiler_params=_cparams(("arbitrary", "arbitrary")),
        name="lat_mix",
    )(sink, q, kv, fm, cache_k, cache_v, ccsc, csl, conv_w)


def _first_index_of_max(vals, iota, sentinel):
    mx = jnp.max(vals, axis=0, keepdims=True)
    idx = jnp.min(jnp.where(vals == mx, iota, sentinel), axis=0, keepdims=True)
    return mx, idx


def _outproj_kernel(mixc_ref, mixl_ref, xc_ref, xl_ref, mod_ref, g_ref, w_ref, wr_ref, rb_ref, tri_ref,
                    x1_ref, h2_ref, ids_ref, wts_ref, rank_ref, cnt_ref, run_ref):
    i = pl.program_id(0)

    @pl.when(i == 0)
    def _():
        run_ref[...] = jnp.zeros_like(run_ref)

    gate1, shift2, scale2 = mod_ref[0, 0, 2:3, :], mod_ref[0, 0, 3:4, :], mod_ref[0, 0, 4:5, :]
    mix = jnp.where(i < N_CTX_TILES, mixc_ref[...], mixl_ref[...])
    x = jnp.where(i < N_CTX_TILES, xc_ref[...], xl_ref[...])
    x1 = x + gate1 * jnp.dot(mix, w_ref[0], preferred_element_type=F32)
    x1_ref[...] = x1
    h2 = _rmsnorm(x1, g_ref[0]) * (1.0 + scale2) + shift2
    _store_slabs(h2_ref, _pack_halves(h2))

    nt = (((1,), (1,)), ((), ()))
    h_hi = h2.astype(BF16)
    h_lo = (h2 - h_hi.astype(F32)).astype(BF16)
    part = lax.dot_general(wr_ref[0], h_hi, nt, preferred_element_type=F32)
    logits = (part[:N_EXPERTS] + part[N_EXPERTS:]
              + lax.dot_general(wr_ref[0, :N_EXPERTS, :], h_lo, nt, preferred_element_type=F32))
    scores = _sigmoid(logits)
    biased = scores + rb_ref[0]
    ninf = -jnp.inf
    mem = lax.broadcasted_iota(I32, (EXPERTS_PER_GROUP, TR), 0)
    grp_rows = []
    for g in range(N_GROUPS):
        bgp = biased[g * EXPERTS_PER_GROUP:(g + 1) * EXPERTS_PER_GROUP]
        m1, i1 = _first_index_of_max(bgp, mem, EXPERTS_PER_GROUP)
        m2 = jnp.max(jnp.where(mem == i1, ninf, bgp), axis=0, keepdims=True)
        grp_rows.append(m1 + m2)
    cur = jnp.concatenate(grp_rows, axis=0)
    gio = lax.broadcasted_iota(I32, (N_GROUPS, TR), 0)
    gsel = jnp.zeros((N_GROUPS, TR), F32)
    for _ in range(TOPK_GROUPS):
        _, gi = _first_index_of_max(cur, gio, N_GROUPS)
        hit = gio == gi
        gsel = jnp.where(hit, 1.0, gsel)
        cur = jnp.where(hit, ninf, cur)
    cur = jnp.concatenate(
        [jnp.where(gsel[g:g + 1] > 0.0, biased[g * EXPERTS_PER_GROUP:(g + 1) * EXPERTS_PER_GROUP], ninf)
         for g in range(N_GROUPS)], axis=0)
    eio = lax.broadcasted_iota(I32, (N_EXPERTS, TR), 0)
    chosen = jnp.zeros((N_EXPERTS, TR), F32)
    ids, sel = [], []
    for _ in range(TOP_K):
        _, ei = _first_index_of_max(cur, eio, N_EXPERTS)
        hit = eio == ei
        ids.append(ei)
        sel.append(jnp.sum(jnp.where(hit, scores, 0.0), axis=0, keepdims=True))
        chosen = jnp.where(hit, 1.0, chosen)
        cur = jnp.where(hit, ninf, cur)
    total = functools.reduce(lambda a, b: a + b, sel)
    ids_ref[...] = jnp.concatenate(ids, axis=0)
    wts_ref[...] = jnp.concatenate([s / total * ROUTE_SCALE for s in sel], axis=0)

    before = jnp.dot(chosen.astype(BF16), tri_ref[...], preferred_element_type=F32) + run_ref[:, 0:1]
    rank_ref[...] = jnp.concatenate(
        [jnp.sum(jnp.where(eio == ei, before, 0.0), axis=0, keepdims=True) for ei in ids], axis=0).astype(I32)
    run = run_ref[...] + jnp.sum(chosen, axis=1, keepdims=True)
    run_ref[...] = run
    cnt_ref[...] = run


def _outproj(layer, mix_ctx, mix_lat, x, mods, g_ffn, w_out_bf, wr_split, router_bias, tri):
    row = lambda i: (i, 0)
    col = lambda i: (0, i)
    const = lambda i: (0, 0)
    lay = lambda i: (layer, 0, 0)
    return pl.pallas_call(
        _outproj_kernel,
        grid=(N_TILES,),
        in_specs=[_CTX_ROWS, _LAT_ROWS, _CTX_ROWS, _LAT_ROWS,
                  _mod_spec(layer),
                  pl.BlockSpec((1, 1, D_MODEL), lay),
                  pl.BlockSpec((1, D_MODEL, D_MODEL), lay, pipeline_mode=pl.Buffered(1)),
                  pl.BlockSpec((1, 2 * N_EXPERTS, D_MODEL), lay),
                  pl.BlockSpec((1, N_EXPERTS, 1), lay),
                  pl.BlockSpec((TR, TR), const)],
        out_specs=[pl.BlockSpec((TR, D_MODEL), row),
                   pl.BlockSpec((TR * SLAB, LANES), row),
                   pl.BlockSpec((TOP_K, TR), col),
                   pl.BlockSpec((TOP_K, TR), col),
                   pl.BlockSpec((TOP_K, TR), col),
                   pl.BlockSpec((N_EXPERTS, META_LANES), const)],
        out_shape=[jax.ShapeDtypeStruct((T_ALL, D_MODEL), F32),
                   jax.ShapeDtypeStruct((T_ALL * SLAB, LANES), U32),
                   jax.ShapeDtypeStruct((TOP_K, T_ALL), I32),
                   jax.ShapeDtypeStruct((TOP_K, T_ALL), F32),
                   jax.ShapeDtypeStruct((TOP_K, T_ALL), I32),
                   jax.ShapeDtypeStruct((N_EXPERTS, META_LANES), F32)],
        scratch_shapes=[pltpu.VMEM((N_EXPERTS, META_LANES), F32)],
        compiler_params=_cparams(("arbitrary",)),
        name="outproj",
    )(mix_ctx, mix_lat, *x, mods, g_ffn.reshape(DEPTH, 1, D_MODEL), w_out_bf, wr_split,
      router_bias.reshape(DEPTH, N_EXPERTS, 1), tri)


def _positions_kernel(ids_ref, rank_ref, cnt_ref, pos_ref, te_ref, meta_ref):
    ids = ids_ref[...]
    pos = rank_ref[...]
    tile_io = lax.broadcasted_iota(I32, (1, TE_LANES), 1)
    lane_io = lax.broadcasted_iota(I32, (1, META_LANES), 1)
    te = jnp.zeros((1, TE_LANES), I32)
    starts = jnp.zeros((1, META_LANES), I32)
    counts = jnp.zeros((1, META_LANES), I32)
    tile_ends = jnp.zeros((1, META_LANES), I32)
    parities = jnp.zeros((1, META_LANES), I32)
    start = jnp.zeros((1, 1), I32)
    n_seg = jnp.zeros((1, 1), I32)
    for e in range(N_EXPERTS):
        c = cnt_ref[e:e + 1, 0:1].astype(I32)
        pos = pos + jnp.where(ids == e, start, 0)
        starts = starts + jnp.where(lane_io == e, start, 0)
        counts = counts + jnp.where(lane_io == e, c, 0)
        parities = parities + jnp.where(lane_io == e, n_seg % 2, 0)
        start = start + ((c + (TM - 1)) // TM) * TM
        n_seg = n_seg + jnp.where(c > 0, 1, 0)
        tile_ends = tile_ends + jnp.where(lane_io == e, start // TM, 0)
        te = te + jnp.where(tile_io * TM >= start, 1, 0)
    pos_ref[...] = pos
    te_ref[...] = jnp.minimum(te, N_EXPERTS - 1)
    meta_ref[0:1, :] = starts
    meta_ref[1:2, :] = counts
    meta_ref[2:3, :] = jnp.broadcast_to(start // TM, (1, META_LANES))
    meta_ref[3:4, :] = tile_ends
    meta_ref[4:5, :] = parities
    meta_ref[5:8, :] = jnp.zeros((3, META_LANES), I32)


def _positions(ids, rank, cnt):
    return pl.pallas_call(
        _positions_kernel,
        out_shape=[jax.ShapeDtypeStruct((TOP_K, T_ALL), I32),
                   jax.ShapeDtypeStruct((1, TE_LANES), I32),
                   jax.ShapeDtypeStruct((8, META_LANES), I32)],
        compiler_params=_cparams(),
        name="positions",
    )(ids, rank, cnt)


SC_CORES, SC_SUBCORES = 2, 16
SC_WORKERS = SC_CORES * SC_SUBCORES
SC_CHUNK = 32


def _sc_kernel(body, out_type, n_idx_rows, name):
    return pl.kernel(
        body,
        out_type=out_type,
        mesh=plsc.VectorSubcoreMesh(core_axis_name="c", subcore_axis_name="s",
                                    num_cores=SC_CORES, num_subcores=SC_SUBCORES),
        scratch_types=[pltpu.VMEM((n_idx_rows, SC_CHUNK), I32),
                       pltpu.VMEM((2, SC_CHUNK, SLAB, LANES), U32),
                       pltpu.SemaphoreType.DMA,
                       pltpu.SemaphoreType.DMA((2,)),
                       pltpu.SemaphoreType.DMA((2,))],
        compiler_params=pltpu.CompilerParams(use_tc_tiling_on_sc=True),
        name=name,
    )


def _sc_worker():
    return lax.axis_index("s") * SC_CORES + lax.axis_index("c")


def _sc_load_indices(n_rows, src_of_row, idx_v, sem):
    @pl.loop(0, n_rows)
    def _(r):
        src, dst = src_of_row(r), idx_v.at[r]
        pltpu.make_async_copy(src, dst, sem).start()

    @pl.loop(0, n_rows)
    def _(r):
        src, dst = src_of_row(r), idx_v.at[r]
        pltpu.make_async_copy(src, dst, sem).wait()


def _sc_two_slot_pipeline(n_chunks, fill, fill_wait, drain, drain_wait):
    assert n_chunks % 2 == 0 and n_chunks >= 4
    fill(0, 0)

    @pl.loop(0, n_chunks // 2)
    def _(pair):
        c = 2 * pair

        @pl.when(pair >= 1)
        def _():
            drain_wait(c - 1, 1)

        fill(c + 1, 1)
        fill_wait(c, 0)
        drain(c, 0)

        @pl.when(pair < n_chunks // 2 - 1)
        def _():
            drain_wait(c, 0)
            fill(c + 2, 0)

        fill_wait(c + 1, 1)
        drain(c + 1, 1)

    drain_wait(n_chunks - 2, 0)
    drain_wait(n_chunks - 1, 1)


def _dispatch(pos_flat, h3):
    n_chunks = T_ALL // SC_WORKERS // SC_CHUNK

    def body(pos_hbm, h_hbm, xs_hbm, idx_v, rows_v, idx_sem, fill_sem, drain_sem):
        t_base = _sc_worker() * (n_chunks * SC_CHUNK)

        def index_src(r):
            t0 = t_base + (r // TOP_K) * SC_CHUNK
            return pos_hbm.at[pl.ds(pl.multiple_of((r % TOP_K) * T_ALL + t0, SC_CHUNK), SC_CHUNK)]

        _sc_load_indices(n_chunks * TOP_K, index_src, idx_v, idx_sem)

        def load(c, slot):
            t0 = pl.multiple_of(t_base + c * SC_CHUNK, SC_CHUNK)
            return pltpu.make_async_copy(h_hbm.at[pl.ds(t0, SC_CHUNK)], rows_v.at[slot], fill_sem.at[slot])

        def scatter(c, slot, k):
            return pltpu.make_async_copy(rows_v.at[slot], xs_hbm.at[idx_v.at[c * TOP_K + k]], drain_sem.at[slot])

        def drain(c, slot):
            for k in range(TOP_K):
                scatter(c, slot, k).start()

        def drain_wait(c, slot):
            for k in range(TOP_K):
                scatter(c, slot, k).wait()

        _sc_two_slot_pipeline(n_chunks, lambda c, s: load(c, s).start(), lambda c, s: load(c, s).wait(),
                              drain, drain_wait)

    return _sc_kernel(body, jax.ShapeDtypeStruct((P_SORT, SLAB, LANES), U32), n_chunks * TOP_K,
                      "sc_dispatch")(pos_flat, h3)


def _gather(pos_flat, y3):
    n_items = pos_flat.shape[0]
    n_chunks = n_items // SC_WORKERS // SC_CHUNK

    def body(pos_hbm, y_hbm, out_hbm, idx_v, rows_v, idx_sem, fill_sem, drain_sem):
        base = _sc_worker() * (n_chunks * SC_CHUNK)

        def index_src(r):
            return pos_hbm.at[pl.ds(pl.multiple_of(base + r * SC_CHUNK, SC_CHUNK), SC_CHUNK)]

        _sc_load_indices(n_chunks, index_src, idx_v, idx_sem)

        def gather(c, slot):
            return pltpu.make_async_copy(y_hbm.at[idx_v.at[c]], rows_v.at[slot], fill_sem.at[slot])

        def store(c, slot):
            i0 = pl.multiple_of(base + c * SC_CHUNK, SC_CHUNK)
            return pltpu.make_async_copy(rows_v.at[slot], out_hbm.at[pl.ds(i0, SC_CHUNK)], drain_sem.at[slot])

        _sc_two_slot_pipeline(n_chunks, lambda c, s: gather(c, s).start(), lambda c, s: gather(c, s).wait(),
                              lambda c, s: store(c, s).start(), lambda c, s: store(c, s).wait())

    return _sc_kernel(body, jax.ShapeDtypeStruct((n_items, SLAB, LANES), U32), n_chunks,
                      "sc_gather")(pos_flat, y3)


def _experts_kernel(layer, te_ref, nt_ref, end_ref, tend_ref, par_ref, x_ref, wg_hbm, wu_hbm, wd_hbm, y_ref,
                    wg_f, wu_f, wd_f, wg_bf, wu_bf, wd_bf, sem):
    j = pl.program_id(0)
    used = j < nt_ref[0]
    e = te_ref[j]
    first = (j == 0) | (te_ref[jnp.maximum(j - 1, 0)] != e)
    slot = par_ref[e]

    def fetch(expert, s):
        return [pltpu.make_async_copy(src.at[layer, expert], dst.at[s], sem.at[s])
                for src, dst in ((wg_hbm, wg_f), (wu_hbm, wu_f), (wd_hbm, wd_f))]

    @pl.when(used & (j == 0))
    def _():
        for cp in fetch(e, slot):
            cp.start()

    @pl.when(used & first)
    def _():
        for cp in fetch(e, slot):
            cp.wait()
        wg_bf[...] = wg_f[slot].astype(BF16)
        wu_bf[...] = wu_f[slot].astype(BF16)
        wd_bf[...] = wd_f[slot].astype(BF16)
        nxt = tend_ref[e]

        @pl.when(nxt < nt_ref[0])
        def _():
            for cp in fetch(te_ref[jnp.minimum(nxt, pl.num_programs(0) - 1)], 1 - slot):
                cp.start()

    @pl.when(used)
    def _():
        valid = end_ref[te_ref[j]] - j * TM
        rows = lax.broadcasted_iota(I32, (TM, HALF), 0)
        words = jnp.where(rows < valid, _load_slabs(x_ref, TM), jnp.uint32(0))
        xa, xb = _unpack_halves(words)
        xa, xb = xa.astype(BF16), xb.astype(BF16)
        g = (jnp.dot(xa, wg_bf[:HALF], preferred_element_type=F32)
             + jnp.dot(xb, wg_bf[HALF:], preferred_element_type=F32))
        u = (jnp.dot(xa, wu_bf[:HALF], preferred_element_type=F32)
             + jnp.dot(xb, wu_bf[HALF:], preferred_element_type=F32))
        a = (_silu(g) * u).astype(BF16)
        _store_slabs(y_ref, _pack_halves(jnp.dot(a, wd_bf[...], preferred_element_type=F32)))


def _experts(layer, te, nt, end, tile_end, parity, xs, w_gate, w_up, w_down):
    n_tiles = xs.shape[0] // (TM * SLAB)
    tile = lambda j, te, nt, *_: (jnp.minimum(j, nt[0] - 1), 0)
    return pl.pallas_call(
        functools.partial(_experts_kernel, layer),
        grid_spec=pltpu.PrefetchScalarGridSpec(
            num_scalar_prefetch=5,
            grid=(n_tiles,),
            in_specs=[pl.BlockSpec((TM * SLAB, LANES), tile),
                      pl.BlockSpec(memory_space=pl.ANY),
                      pl.BlockSpec(memory_space=pl.ANY),
                      pl.BlockSpec(memory_space=pl.ANY)],
            out_specs=pl.BlockSpec((TM * SLAB, LANES), tile),
            scratch_shapes=[pltpu.VMEM((2, D_MODEL, EXPERT_FF), F32),
                            pltpu.VMEM((2, D_MODEL, EXPERT_FF), F32),
                            pltpu.VMEM((2, EXPERT_FF, D_MODEL), F32),
                            pltpu.VMEM((D_MODEL, EXPERT_FF), BF16),
                            pltpu.VMEM((D_MODEL, EXPERT_FF), BF16),
                            pltpu.VMEM((EXPERT_FF, D_MODEL), BF16),
                            pltpu.SemaphoreType.DMA((2,))]),
        out_shape=jax.ShapeDtypeStruct(xs.shape, U32),
        compiler_params=_cparams(("arbitrary",)),
        name="experts",
    )(te, nt, end, tile_end, parity, xs, w_gate, w_up, w_down)


def _combine_kernel(final, wts_ref, x1_ref, ysh_ref, mod_ref, g_ref, buf_ref, *rest):
    o_ref, x2_ref, wb_ref = rest
    w = jnp.transpose(wts_ref[...])
    for k in range(TOP_K):
        wb_ref[k] = jnp.broadcast_to(w[:, k:k + 1], (TC_ROWS, LANES))
    sq = jnp.zeros((TC_ROWS, LANES), F32)
    for s in range(SLAB):
        acc_a, acc_b = _unpack_halves(ysh_ref[pl.ds(s, TC_ROWS, stride=SLAB), :])
        for k in range(TOP_K):
            ya, yb = _unpack_halves(buf_ref[k, pl.ds(s, TC_ROWS, stride=SLAB), :])
            acc_a = acc_a + wb_ref[k] * ya
            acc_b = acc_b + wb_ref[k] * yb
        for half, acc in ((0, acc_a), (1, acc_b)):
            cols = slice(half * HALF + s * LANES, half * HALF + (s + 1) * LANES)
            x2 = x1_ref[:, cols] + mod_ref[0, 0, 5:6, cols] * acc
            x2_ref[:, cols] = x2
            if final:
                sq = sq + x2 * x2
    out = x2_ref[...]
    if final:
        ms = jnp.sum(sq, axis=-1, keepdims=True) * (1.0 / D_MODEL)
        out = out * lax.rsqrt(ms + EPS) * g_ref[...]

    o_ref[...] = out


def _combine(layer, row0, n_rows, wts, x1, ysh, mods, final_norm, yg):
    final = layer == DEPTH - 1
    t0 = row0 // TC_ROWS
    row = lambda i: (t0 + i, 0)
    return pl.pallas_call(
        functools.partial(_combine_kernel, final),
        grid=(n_rows // TC_ROWS,),
        in_specs=[pl.BlockSpec((TOP_K, TC_ROWS), lambda i: (0, t0 + i)),
                  pl.BlockSpec((TC_ROWS, D_MODEL), row),
                  pl.BlockSpec((TC_ROWS * SLAB, LANES), row),
                  pl.BlockSpec((1, 1, N_MOD, D_MODEL),
                               lambda i: (layer, _cond_row((t0 + i) // (TR // TC_ROWS)), 0, 0)),
                  pl.BlockSpec((1, D_MODEL), lambda i: (0, 0)),
                  pl.BlockSpec((TOP_K, TC_ROWS * SLAB, LANES), lambda i: (0, i, 0))],
        out_specs=pl.BlockSpec((TC_ROWS, D_MODEL), lambda i: (i, 0)),
        out_shape=jax.ShapeDtypeStruct((n_rows, D_MODEL), F32),
        scratch_shapes=[pltpu.VMEM((TC_ROWS, D_MODEL), F32),
                        pltpu.VMEM((TOP_K, TC_ROWS, LANES), F32)],
        compiler_params=_cparams(("arbitrary",)),
        name="combine",
    )(wts, x1, ysh, mods, final_norm.reshape(1, D_MODEL), yg)


def kernel(x_prompt, x_sample, cache_k, cache_v, c, c_ctx, w_mod, b_mod, g_mix, g_ffn, w_in, w_out, conv_w,
           attn_sink, w_router, router_bias, w_gate_e, w_up_e, w_down_e, w_gate_s, w_up_s, w_down_s, final_norm):
    ccsc = jnp.asarray(_channel_dft(), BF16)
    csl_ctx = jnp.asarray(_dft_tables(SEQ), BF16)
    csl_lat = jnp.asarray(_dft_tables(DEC_SEQ), BF16)
    rope = tuple(jnp.asarray(t, F32) for t in _rope_tables(DEC_SEQ, TR))
    tri = jnp.asarray(np.triu(np.ones((TR, TR)), 1), BF16)
    shared_te = jnp.zeros((T_ALL // TM,), I32)
    shared_nt = jnp.full((1,), T_ALL // TM, I32)
    shared_end = jnp.full((1,), T_ALL, I32)

    cond = jnp.concatenate([c_ctx[None], c, jnp.zeros((N_COND - 1 - DEC_BATCH, D_MODEL), F32)], axis=0)
    mods = _mods(cond, w_mod, b_mod).reshape(DEPTH, N_COND, N_MOD, D_MODEL)
    cache_k = cache_k.reshape(DEC_BATCH, DEPTH, SEQ, KV_W)
    cache_v = cache_v.reshape(DEC_BATCH, DEPTH, SEQ, KV_W)
    w_in_bf, w_out_bf = w_in.astype(BF16), w_out.astype(BF16)
    wr_t = jnp.swapaxes(w_router, 1, 2)
    wr_hi = wr_t.astype(BF16)
    wr_split = jnp.concatenate([wr_hi, (wr_t - wr_hi.astype(F32)).astype(BF16)], axis=1)
    ws_gate, ws_up, ws_down = w_gate_s[:, None], w_up_s[:, None], w_down_s[:, None]
    x = (x_prompt.reshape(T_CTX, D_MODEL), x_sample.reshape(T_LAT, D_MODEL))

    new_kv = ()
    for l in range(DEPTH):
        q, kv, fm, *new_kv = _inproj(l, x, mods, g_mix, w_in_bf, rope, tuple(new_kv))
        mix_ctx = _ctx_mix(l, attn_sink, q, kv, fm, ccsc, csl_ctx, conv_w)
        mix_lat = _lat_mix(l, attn_sink, q, kv, fm, cache_k, cache_v, ccsc, csl_lat, conv_w)
        x1, h2, ids, wts, rank, cnt = _outproj(l, mix_ctx, mix_lat, x, mods, g_ffn, w_out_bf, wr_split,
                                               router_bias, tri)
        pos, te, meta = _positions(ids, rank, cnt)
        pos_flat = pos.reshape(TOP_K * T_ALL)
        xs = _dispatch(pos_flat, h2.reshape(T_ALL, SLAB, LANES)).reshape(P_SORT * SLAB, LANES)
        y = _experts(l, te[0, :N_SORT_TILES], meta[2, :1], meta[0, :N_EXPERTS] + meta[1, :N_EXPERTS],
                     meta[3, :N_EXPERTS], meta[4, :N_EXPERTS], xs, w_gate_e, w_up_e, w_down_e)
        ysh = _experts(l, shared_te, shared_nt, shared_end, shared_nt, shared_te[:1], h2,
                       ws_gate, ws_up, ws_down)
        y3 = y.reshape(P_SORT, SLAB, LANES)
        x = []
        for row0, n_rows in ((0, T_CTX), (T_CTX, T_LAT)):
            part_pos = pos[:, row0:row0 + n_rows].reshape(TOP_K * n_rows)
            yg = _gather(part_pos, y3).reshape(TOP_K, n_rows * SLAB, LANES)
            x.append(_combine(l, row0, n_rows, wts, x1, ysh, mods, final_norm, yg))

    y_prompt = x[0].reshape(BATCH, SEQ, D_MODEL)
    y_sample = x[1].reshape(DEC_BATCH, DEC_SEQ, D_MODEL)
    new_k, new_v = new_kv
    return y_prompt, y_sample, new_k, new_v
```

```python
import functools

import numpy as np
import jax
import jax.numpy as jnp
from jax import lax
from jax.experimental import pallas as pl
from jax.experimental.pallas import tpu as pltpu
from jax.experimental.pallas import tpu_sc as plsc

F32 = jnp.float32
BF16 = jnp.bfloat16
I32 = jnp.int32
U32 = jnp.uint32

D_MODEL = 2048
HALF = D_MODEL // 2
BATCH, SEQ = 32, 256
DEC_BATCH, DEC_SEQ = 4, 1024
DEPTH = 2
GRID_W = 64
HEAD_DIM = 128
N_HEADS, N_KV_HEADS, Q_PER_KV = 8, 2, 4
ATTN_W, KV_W = 1024, 256
WINDOW, BLOCK = 128, 128
SCALE = HEAD_DIM ** -0.5
ROPE_BASE = 10000.0
ROPE_PAIRS = 32
FOURIER_W, FOURIER_GROUPS, FOURIER_GROUP_W = 512, 4, 128
CONV_W = 512
IN_W = 3584
N_EXPERTS, TOP_K, N_GROUPS, EXPERTS_PER_GROUP, TOPK_GROUPS = 64, 8, 8, 8, 4
EXPERT_FF = 512
ROUTE_SCALE = 2.5
N_MOD = 6
EPS = 1e-6
NEG = -1e30

T_CTX = BATCH * SEQ
T_LAT = DEC_BATCH * DEC_SEQ
T_ALL = T_CTX + T_LAT
TR = 512
N_CTX_TILES = T_CTX // TR
LAT_TILES_PER_SEQ = DEC_SEQ // TR
N_TILES = T_ALL // TR
N_COND = 8
MOD_TN = 1024
TM = 512
TC_ROWS = 256
N_SORT_TILES = T_ALL * TOP_K // TM + N_EXPERTS
P_SORT = N_SORT_TILES * TM
META_LANES = 128
TE_LANES = 512
VMEM_LIMIT = 56 * 1024 * 1024


def _cparams(sem=None):
    return pltpu.CompilerParams(dimension_semantics=sem, vmem_limit_bytes=VMEM_LIMIT)


def _silu(x):
    return x / (1.0 + jnp.exp(-x))


def _sigmoid(x):
    return 1.0 / (1.0 + jnp.exp(-x))


def _rmsnorm(x, g):
    return x * lax.rsqrt(jnp.mean(x * x, axis=-1, keepdims=True) + EPS) * g


def _cond_row(i):
    return jnp.where(i < N_CTX_TILES, 0, 1 + (i - N_CTX_TILES) // LAT_TILES_PER_SEQ)


def _pack_halves(v):
    n = v.shape[1] // 2
    hi = pltpu.bitcast(v[:, :n].astype(BF16).astype(F32), U32)
    lo = pltpu.bitcast(v[:, n:].astype(BF16).astype(F32), U32)
    return hi | (lo >> 16)


def _unpack_halves(w):
    return pltpu.bitcast(w & jnp.uint32(0xFFFF0000), F32), pltpu.bitcast(w << 16, F32)


SLAB, LANES = 8, 128


def _load_slabs(ref, n, row0=0):
    return jnp.concatenate([ref[pl.ds(row0 * SLAB + s, n, stride=SLAB), :] for s in range(SLAB)], axis=1)


def _store_slabs(ref, words, row0=0):
    n = words.shape[0]
    for s in range(SLAB):
        ref[pl.ds(row0 * SLAB + s, n, stride=SLAB), :] = words[:, s * LANES:(s + 1) * LANES]


def _dft_tables(L):
    j = np.arange(L, dtype=np.int64)
    ang = 2.0 * np.pi * ((j[:, None] * j[None, :]) % L).astype(np.float64) / L
    c, s = np.cos(ang) / np.sqrt(L), np.sin(ang) / np.sqrt(L)
    return np.concatenate([c, -s], axis=1)


def _channel_dft():
    n = FOURIER_GROUP_W
    j = np.arange(n, dtype=np.int64)
    ang = 2.0 * np.pi * ((j[:, None] * j[None, :]) % n).astype(np.float64) / n
    return np.concatenate([np.cos(ang), np.sin(ang)], axis=1) / np.sqrt(n)


def _rope_tables(S, n_identity):
    pos = np.arange(S + n_identity)
    pos[S:] = 0
    row, col = (pos // GRID_W).astype(np.float64), (pos % GRID_W).astype(np.float64)
    inv = ROPE_BASE ** (-np.arange(ROPE_PAIRS, dtype=np.float64) / ROPE_PAIRS)
    ar, ac = row[:, None] * inv, col[:, None] * inv
    z = np.zeros_like(ar)
    cos = np.concatenate([np.cos(ar), np.cos(ar), np.cos(ac), np.cos(ac)], axis=1)
    sin_hi = np.concatenate([-np.sin(ar), z, -np.sin(ac), z], axis=1)
    sin_lo = np.concatenate([z, np.sin(ar), z, np.sin(ac)], axis=1)
    return cos, sin_hi, sin_lo


def _mods_kernel(cond_ref, w_ref, b_ref, o_ref):
    s = _silu(cond_ref[...]).astype(BF16)
    o_ref[0] = jnp.dot(s, w_ref[0].astype(BF16), preferred_element_type=F32) + b_ref[0]


def _mods(layer, cond, w_mod, b_mod):
    n = N_MOD * D_MODEL
    return pl.pallas_call(
        _mods_kernel,
        grid=(n // MOD_TN,),
        in_specs=[pl.BlockSpec((N_COND, D_MODEL), lambda j: (0, 0)),
                  pl.BlockSpec((1, D_MODEL, MOD_TN), lambda j: (layer, 0, j)),
                  pl.BlockSpec((1, 1, MOD_TN), lambda j: (layer, 0, j))],
        out_specs=pl.BlockSpec((1, N_COND, MOD_TN), lambda j: (0, 0, j)),
        out_shape=jax.ShapeDtypeStruct((1, N_COND, n), F32),
        compiler_params=_cparams(("arbitrary",)),
        name="mods",
    )(cond, w_mod, b_mod.reshape(DEPTH, 1, n)).reshape(1, N_COND, N_MOD, D_MODEL)


_CTX_ROWS = pl.BlockSpec((TR, D_MODEL), lambda i: (jnp.minimum(i, N_CTX_TILES - 1), 0))
_LAT_ROWS = pl.BlockSpec((TR, D_MODEL), lambda i: (jnp.maximum(i - N_CTX_TILES, 0), 0))


def _mod_spec():
    return pl.BlockSpec((1, 1, N_MOD, D_MODEL), lambda i: (0, _cond_row(i), 0, 0))


def _inproj_kernel(n_alias, xc_ref, xl_ref, mod_ref, g_ref, w_ref, cos_ref, shi_ref, slo_ref, *rest):
    q_ref, kv_ref, fm_ref, nk_ref, nv_ref = rest[n_alias:]
    i = pl.program_id(0)
    shift, scale = mod_ref[0, 0, 0:1, :], mod_ref[0, 0, 1:2, :]
    for r0 in range(0, TR, SEQ):
        rows = slice(r0, r0 + SEQ)
        x = jnp.where(i < N_CTX_TILES, xc_ref[rows, :], xl_ref[rows, :])
        h = (_rmsnorm(x, g_ref[0]) * (1.0 + scale) + shift).astype(BF16)
        fm_ref[rows, :] = jnp.dot(h, w_ref[0, :, ATTN_W + 2 * KV_W:], preferred_element_type=F32).astype(BF16)
        qkv = jnp.dot(h, w_ref[0, :, :ATTN_W + 2 * KV_W], preferred_element_type=F32)
        kv_ref[rows, KV_W:] = qkv[:, ATTN_W + KV_W:]
        cos, shi, slo = cos_ref[rows, :], shi_ref[rows, :], slo_ref[rows, :]
        for hd in range(N_HEADS + N_KV_HEADS):
            xh = qkv[:, hd * HEAD_DIM:(hd + 1) * HEAD_DIM]
            r = xh * cos + pltpu.roll(xh, HEAD_DIM - 32, 1) * shi + pltpu.roll(xh, 32, 1) * slo
            if hd < N_HEADS:
                q_ref[rows, hd * HEAD_DIM:(hd + 1) * HEAD_DIM] = r.astype(BF16)
            else:
                kv_ref[rows, (hd - N_HEADS) * HEAD_DIM:(hd - N_HEADS + 1) * HEAD_DIM] = r

    @pl.when(i < N_CTX_TILES)
    def _():
        for b in range(TR // SEQ):
            for hd in range(N_KV_HEADS):
                c0 = hd * HEAD_DIM
                nk_ref[b, 0, :, hd, :] = kv_ref[b * SEQ:(b + 1) * SEQ, c0:c0 + HEAD_DIM]
                nv_ref[b, 0, :, hd, :] = kv_ref[b * SEQ:(b + 1) * SEQ, KV_W + c0:KV_W + c0 + HEAD_DIM]


def _inproj(layer, x, mods, g_mix, w_in_bf, rope, new_kv):
    lat_blk = lambda i: (jnp.where(i < N_CTX_TILES, LAT_TILES_PER_SEQ, (i - N_CTX_TILES) % LAT_TILES_PER_SEQ), 0)
    row = lambda i: (i, 0)
    kv_blk = pl.BlockSpec((TR // SEQ, 1, SEQ, N_KV_HEADS, HEAD_DIM),
                          lambda i: (jnp.minimum(i, N_CTX_TILES - 1), layer, 0, 0, 0))
    kv_shape = jax.ShapeDtypeStruct((BATCH, DEPTH, SEQ, N_KV_HEADS, HEAD_DIM), F32)
    n_in = 8
    return pl.pallas_call(
        functools.partial(_inproj_kernel, len(new_kv)),
        grid=(N_TILES,),
        in_specs=[_CTX_ROWS, _LAT_ROWS,
                  _mod_spec(),
                  pl.BlockSpec((1, 1, D_MODEL), lambda i: (layer, 0, 0)),
                  pl.BlockSpec((1, D_MODEL, IN_W), lambda i: (0, 0, 0), pipeline_mode=pl.Buffered(1)),
                  pl.BlockSpec((TR, HEAD_DIM), lat_blk),
                  pl.BlockSpec((TR, HEAD_DIM), lat_blk),
                  pl.BlockSpec((TR, HEAD_DIM), lat_blk)] + [pl.BlockSpec(memory_space=pl.ANY)] * len(new_kv),
        out_specs=[pl.BlockSpec((TR, ATTN_W), row),
                   pl.BlockSpec((TR, 2 * KV_W), row),
                   pl.BlockSpec((TR, FOURIER_W + 3 * CONV_W), row),
                   kv_blk, kv_blk],
        out_shape=[jax.ShapeDtypeStruct((T_ALL, ATTN_W), BF16),
                   jax.ShapeDtypeStruct((T_ALL, 2 * KV_W), F32),
                   jax.ShapeDtypeStruct((T_ALL, FOURIER_W + 3 * CONV_W), BF16),
                   kv_shape, kv_shape],
        input_output_aliases={n_in + a: 3 + a for a in range(len(new_kv))},
        compiler_params=_cparams(("arbitrary",)),
        name="inproj",
    )(*x, mods, g_mix.reshape(DEPTH, 1, D_MODEL), w_in_bf, *rope, *new_kv)


def _fourier_conv(fm_ref, ccsc_ref, csl_ref, cw_ref, o_ref):
    L = fm_ref.shape[0]
    for g in range(FOURIER_GROUPS):
        xg = fm_ref[:, g * FOURIER_GROUP_W:(g + 1) * FOURIER_GROUP_W]
        z = jnp.dot(xg, ccsc_ref[...], preferred_element_type=F32)
        zz = jnp.concatenate([z[:, :FOURIER_GROUP_W], z[:, FOURIER_GROUP_W:]], axis=0).astype(BF16)
        f = jnp.dot(csl_ref[...], zz, preferred_element_type=F32)
        o_ref[:, ATTN_W + g * FOURIER_GROUP_W:ATTN_W + (g + 1) * FOURIER_GROUP_W] = f.astype(BF16)
    xc = fm_ref[:, FOURIER_W:FOURIER_W + CONV_W].astype(F32)
    bg = fm_ref[:, FOURIER_W + CONV_W:FOURIER_W + 2 * CONV_W].astype(F32)
    cg = fm_ref[:, FOURIER_W + 2 * CONV_W:].astype(F32)
    u = cg * xc
    rows = lax.broadcasted_iota(I32, u.shape, 0)
    prev = jnp.where(rows == 0, 0.0, pltpu.roll(u, 1, 0))
    nxt = jnp.where(rows == L - 1, 0.0, pltpu.roll(u, L - 1, 0))
    y = prev * cw_ref[0, 0:1, :] + u * cw_ref[0, 1:2, :] + nxt * cw_ref[0, 2:3, :]
    o_ref[:, ATTN_W + FOURIER_W:] = (bg * y).astype(BF16)


def _ctx_mix_kernel(layer, sink_ref, q_ref, kv_ref, fm_ref, ccsc_ref, csl_ref, cw_ref, o_ref):
    for h in range(N_HEADS):
        g = h // Q_PER_KV
        q = q_ref[:, h * HEAD_DIM:(h + 1) * HEAD_DIM]
        k = kv_ref[:, g * HEAD_DIM:(g + 1) * HEAD_DIM].astype(BF16)
        v = kv_ref[:, KV_W + g * HEAD_DIM:KV_W + (g + 1) * HEAD_DIM].astype(BF16)
        s = lax.dot_general(q, k, (((1,), (1,)), ((), ())), preferred_element_type=F32) * SCALE
        sink = sink_ref[layer, h]
        m = jnp.maximum(jnp.max(s, axis=-1, keepdims=True), sink)
        e = jnp.exp(s - m)
        den = jnp.sum(e, axis=-1, keepdims=True) + jnp.exp(sink - m)
        o = jnp.dot(e.astype(BF16), v, preferred_element_type=F32) / den
        o_ref[:, h * HEAD_DIM:(h + 1) * HEAD_DIM] = o.astype(BF16)
    _fourier_conv(fm_ref, ccsc_ref, csl_ref, cw_ref, o_ref)


def _ctx_mix(layer, sink, q, kv, fm, ccsc, csl, conv_w):
    row = lambda b: (b, 0)
    const = lambda b: (0, 0)
    return pl.pallas_call(
        functools.partial(_ctx_mix_kernel, layer),
        grid=(BATCH,),
        in_specs=[pl.BlockSpec(memory_space=pltpu.SMEM),
                  pl.BlockSpec((SEQ, ATTN_W), row),
                  pl.BlockSpec((SEQ, 2 * KV_W), row),
                  pl.BlockSpec((SEQ, FOURIER_W + 3 * CONV_W), row),
                  pl.BlockSpec((FOURIER_GROUP_W, 2 * FOURIER_GROUP_W), const),
                  pl.BlockSpec((SEQ, 2 * SEQ), const),
                  pl.BlockSpec((1, 3, CONV_W), lambda b: (layer, 0, 0))],
        out_specs=pl.BlockSpec((SEQ, D_MODEL), row),
        out_shape=jax.ShapeDtypeStruct((T_CTX, D_MODEL), BF16),
        compiler_params=_cparams(("arbitrary",)),
        name="ctx_mix",
    )(sink, q, kv, fm, ccsc, csl, conv_w)


def _lat_mix_kernel(layer, sink_ref, q_ref, kv_ref, fm_ref, ck_ref, cv_ref, ccsc_ref, csl_ref, cw_ref, o_ref):
    n = pl.program_id(1)
    nb = DEC_SEQ // BLOCK
    rows = Q_PER_KV * BLOCK
    band = 3 * BLOCK

    @pl.when(n == 0)
    def _():
        _fourier_conv(fm_ref, ccsc_ref, csl_ref, cw_ref, o_ref)

    q0 = pl.multiple_of(n * BLOCK, BLOCK)
    k0 = pl.multiple_of(jnp.clip(n - 1, 0, nb - 3) * BLOCK, BLOCK)
    qpos = q0 + lax.broadcasted_iota(I32, (BLOCK, band), 0)
    kpos = k0 + lax.broadcasted_iota(I32, (BLOCK, band), 1)
    valid = jnp.abs(qpos - kpos) <= WINDOW
    for g in range(N_KV_HEADS):
        kb = kv_ref[pl.ds(k0, band), g * HEAD_DIM:(g + 1) * HEAD_DIM].astype(BF16)
        vb = kv_ref[pl.ds(k0, band), KV_W + g * HEAD_DIM:KV_W + (g + 1) * HEAD_DIM].astype(BF16)
        k_ctx = ck_ref[0, 0, :, g * HEAD_DIM:(g + 1) * HEAD_DIM].astype(BF16)
        v_ctx = cv_ref[0, 0, :, g * HEAD_DIM:(g + 1) * HEAD_DIM].astype(BF16)
        for h in range(g * Q_PER_KV, (g + 1) * Q_PER_KV):
            sink = sink_ref[layer, h]
            q = q_ref[pl.ds(q0, BLOCK), h * HEAD_DIM:(h + 1) * HEAD_DIM]
            s_loc = lax.dot_general(q, kb, (((1,), (1,)), ((), ())), preferred_element_type=F32) * SCALE
            s_loc = jnp.where(valid, s_loc, NEG)
            s_ctx = lax.dot_general(q, k_ctx, (((1,), (1,)), ((), ())), preferred_element_type=F32) * SCALE
            m = jnp.maximum(jnp.maximum(jnp.max(s_loc, axis=-1, keepdims=True),
                                        jnp.max(s_ctx, axis=-1, keepdims=True)), sink)
            e_loc, e_ctx = jnp.exp(s_loc - m), jnp.exp(s_ctx - m)
            den = (jnp.sum(e_loc, axis=-1, keepdims=True) + jnp.sum(e_ctx, axis=-1, keepdims=True)
                   + jnp.exp(sink - m))
            o = (jnp.dot(e_loc.astype(BF16), vb, preferred_element_type=F32)
                 + jnp.dot(e_ctx.astype(BF16), v_ctx, preferred_element_type=F32)) / den
            o_ref[pl.ds(q0, BLOCK), h * HEAD_DIM:(h + 1) * HEAD_DIM] = o.astype(BF16)


def _lat_mix(layer, sink, q, kv, fm, cache_k, cache_v, ccsc, csl, conv_w):
    off = T_CTX // DEC_SEQ
    row = lambda b, n: (off + b, 0)
    const = lambda b, n: (0, 0)
    cache = lambda b, n: (b, layer, 0, 0)
    return pl.pallas_call(
        functools.partial(_lat_mix_kernel, layer),
        grid=(DEC_BATCH, DEC_SEQ // BLOCK),
        in_specs=[pl.BlockSpec(memory_space=pltpu.SMEM),
                  pl.BlockSpec((DEC_SEQ, ATTN_W), row),
                  pl.BlockSpec((DEC_SEQ, 2 * KV_W), row),
                  pl.BlockSpec((DEC_SEQ, FOURIER_W + 3 * CONV_W), row),
                  pl.BlockSpec((1, 1, SEQ, KV_W), cache),
                  pl.BlockSpec((1, 1, SEQ, KV_W), cache),
                  pl.BlockSpec((FOURIER_GROUP_W, 2 * FOURIER_GROUP_W), const),
                  pl.BlockSpec((DEC_SEQ, 2 * DEC_SEQ), const, pipeline_mode=pl.Buffered(1)),
                  pl.BlockSpec((1, 3, CONV_W), lambda b, n: (layer, 0, 0))],
        out_specs=pl.BlockSpec((DEC_SEQ, D_MODEL), lambda b, n: (b, 0)),
        out_shape=jax.ShapeDtypeStruct((T_LAT, D_MODEL), BF16),
        compiler_params=_cparams(("arbitrary", "arbitrary")),
        name="lat_mix",
    )(sink, q, kv, fm, cache_k, cache_v, ccsc, csl, conv_w)


def _first_index_of_max(vals, iota, sentinel):
    mx = jnp.max(vals, axis=0, keepdims=True)
    idx = jnp.min(jnp.where(vals == mx, iota, sentinel), axis=0, keepdims=True)
    return mx, idx


def _outproj_kernel(mixc_ref, mixl_ref, xc_ref, xl_ref, mod_ref, g_ref, w_ref, wr_ref, rb_ref, tri_ref,
                    x1_ref, h2_ref, ids_ref, wts_ref, rank_ref, cnt_ref, run_ref):
    i = pl.program_id(0)

    @pl.when(i == 0)
    def _():
        run_ref[...] = jnp.zeros_like(run_ref)

    gate1, shift2, scale2 = mod_ref[0, 0, 2:3, :], mod_ref[0, 0, 3:4, :], mod_ref[0, 0, 4:5, :]
    mix = jnp.where(i < N_CTX_TILES, mixc_ref[...], mixl_ref[...])
    x = jnp.where(i < N_CTX_TILES, xc_ref[...], xl_ref[...])
    x1 = x + gate1 * jnp.dot(mix, w_ref[0], preferred_element_type=F32)
    x1_ref[...] = x1
    h2 = _rmsnorm(x1, g_ref[0]) * (1.0 + scale2) + shift2
    _store_slabs(h2_ref, _pack_halves(h2))

    nt = (((1,), (1,)), ((), ()))
    h_hi = h2.astype(BF16)
    h_lo = (h2 - h_hi.astype(F32)).astype(BF16)
    part = lax.dot_general(wr_ref[0], h_hi, nt, preferred_element_type=F32)
    logits = (part[:N_EXPERTS] + part[N_EXPERTS:]
              + lax.dot_general(wr_ref[0, :N_EXPERTS, :], h_lo, nt, preferred_element_type=F32))
    scores = _sigmoid(logits)
    biased = scores + rb_ref[0]
    ninf = -jnp.inf
    mem = lax.broadcasted_iota(I32, (EXPERTS_PER_GROUP, TR), 0)
    grp_rows = []
    for g in range(N_GROUPS):
        bgp = biased[g * EXPERTS_PER_GROUP:(g + 1) * EXPERTS_PER_GROUP]
        m1, i1 = _first_index_of_max(bgp, mem, EXPERTS_PER_GROUP)
        m2 = jnp.max(jnp.where(mem == i1, ninf, bgp), axis=0, keepdims=True)
        grp_rows.append(m1 + m2)
    cur = jnp.concatenate(grp_rows, axis=0)
    gio = lax.broadcasted_iota(I32, (N_GROUPS, TR), 0)
    gsel = jnp.zeros((N_GROUPS, TR), F32)
    for _ in range(TOPK_GROUPS):
        _, gi = _first_index_of_max(cur, gio, N_GROUPS)
        hit = gio == gi
        gsel = jnp.where(hit, 1.0, gsel)
        cur = jnp.where(hit, ninf, cur)
    cur = jnp.concatenate(
        [jnp.where(gsel[g:g + 1] > 0.0, biased[g * EXPERTS_PER_GROUP:(g + 1) * EXPERTS_PER_GROUP], ninf)
         for g in range(N_GROUPS)], axis=0)
    eio = lax.broadcasted_iota(I32, (N_EXPERTS, TR), 0)
    chosen = jnp.zeros((N_EXPERTS, TR), F32)
    ids, sel = [], []
    for _ in range(TOP_K):
        _, ei = _first_index_of_max(cur, eio, N_EXPERTS)
        hit = eio == ei
        ids.append(ei)
        sel.append(jnp.sum(jnp.where(hit, scores, 0.0), axis=0, keepdims=True))
        chosen = jnp.where(hit, 1.0, chosen)
        cur = jnp.where(hit, ninf, cur)
    total = functools.reduce(lambda a, b: a + b, sel)
    ids_ref[...] = jnp.concatenate(ids, axis=0)
    wts_ref[...] = jnp.concatenate([s / total * ROUTE_SCALE for s in sel], axis=0)

    before = jnp.dot(chosen.astype(BF16), tri_ref[...], preferred_element_type=F32) + run_ref[:, 0:1]
    rank_ref[...] = jnp.concatenate(
        [jnp.sum(jnp.where(eio == ei, before, 0.0), axis=0, keepdims=True) for ei in ids], axis=0).astype(I32)
    run = run_ref[...] + jnp.sum(chosen, axis=1, keepdims=True)
    run_ref[...] = run
    cnt_ref[...] = run


def _outproj(layer, mix_ctx, mix_lat, x, mods, g_ffn, w_out_bf, wr_split, router_bias, tri):
    row = lambda i: (i, 0)
    col = lambda i: (0, i)
    const = lambda i: (0, 0)
    lay = lambda i: (layer, 0, 0)
    return pl.pallas_call(
        _outproj_kernel,
        grid=(N_TILES,),
        in_specs=[_CTX_ROWS, _LAT_ROWS, _CTX_ROWS, _LAT_ROWS,
                  _mod_spec(),
                  pl.BlockSpec((1, 1, D_MODEL), lay),
                  pl.BlockSpec((1, D_MODEL, D_MODEL), lambda i: (0, 0, 0), pipeline_mode=pl.Buffered(1)),
                  pl.BlockSpec((1, 2 * N_EXPERTS, D_MODEL), lay),
                  pl.BlockSpec((1, N_EXPERTS, 1), lay),
                  pl.BlockSpec((TR, TR), const)],
        out_specs=[pl.BlockSpec((TR, D_MODEL), row),
                   pl.BlockSpec((TR * SLAB, LANES), row),
                   pl.BlockSpec((TOP_K, TR), col),
                   pl.BlockSpec((TOP_K, TR), col),
                   pl.BlockSpec((TOP_K, TR), col),
                   pl.BlockSpec((N_EXPERTS, META_LANES), const)],
        out_shape=[jax.ShapeDtypeStruct((T_ALL, D_MODEL), F32),
                   jax.ShapeDtypeStruct((T_ALL * SLAB, LANES), U32),
                   jax.ShapeDtypeStruct((TOP_K, T_ALL), I32),
                   jax.ShapeDtypeStruct((TOP_K, T_ALL), F32),
                   jax.ShapeDtypeStruct((TOP_K, T_ALL), I32),
                   jax.ShapeDtypeStruct((N_EXPERTS, META_LANES), F32)],
        scratch_shapes=[pltpu.VMEM((N_EXPERTS, META_LANES), F32)],
        compiler_params=_cparams(("arbitrary",)),
        name="outproj",
    )(mix_ctx, mix_lat, *x, mods, g_ffn.reshape(DEPTH, 1, D_MODEL), w_out_bf, wr_split,
      router_bias.reshape(DEPTH, N_EXPERTS, 1), tri)


def _positions_kernel(ids_ref, rank_ref, cnt_ref, pos_ref, te_ref, meta_ref):
    ids = ids_ref[...]
    pos = rank_ref[...]
    tile_io = lax.broadcasted_iota(I32, (1, TE_LANES), 1)
    lane_io = lax.broadcasted_iota(I32, (1, META_LANES), 1)
    te = jnp.zeros((1, TE_LANES), I32)
    starts = jnp.zeros((1, META_LANES), I32)
    counts = jnp.zeros((1, META_LANES), I32)
    tile_ends = jnp.zeros((1, META_LANES), I32)
    parities = jnp.zeros((1, META_LANES), I32)
    start = jnp.zeros((1, 1), I32)
    n_seg = jnp.zeros((1, 1), I32)
    for e in range(N_EXPERTS):
        c = cnt_ref[e:e + 1, 0:1].astype(I32)
        pos = pos + jnp.where(ids == e, start, 0)
        starts = starts + jnp.where(lane_io == e, start, 0)
        counts = counts + jnp.where(lane_io == e, c, 0)
        parities = parities + jnp.where(lane_io == e, n_seg % 2, 0)
        start = start + ((c + (TM - 1)) // TM) * TM
        n_seg = n_seg + jnp.where(c > 0, 1, 0)
        tile_ends = tile_ends + jnp.where(lane_io == e, start // TM, 0)
        te = te + jnp.where(tile_io * TM >= start, 1, 0)
    pos_ref[...] = pos
    te_ref[...] = jnp.minimum(te, N_EXPERTS - 1)
    meta_ref[0:1, :] = starts
    meta_ref[1:2, :] = counts
    meta_ref[2:3, :] = jnp.broadcast_to(start // TM, (1, META_LANES))
    meta_ref[3:4, :] = tile_ends
    meta_ref[4:5, :] = parities
    meta_ref[5:8, :] = jnp.zeros((3, META_LANES), I32)


def _positions(ids, rank, cnt):
    return pl.pallas_call(
        _positions_kernel,
        out_shape=[jax.ShapeDtypeStruct((TOP_K, T_ALL), I32),
                   jax.ShapeDtypeStruct((1, TE_LANES), I32),
                   jax.ShapeDtypeStruct((8, META_LANES), I32)],
        compiler_params=_cparams(),
        name="positions",
    )(ids, rank, cnt)


SC_CORES, SC_SUBCORES = 2, 16
SC_WORKERS = SC_CORES * SC_SUBCORES
SC_CHUNK = 32


def _sc_kernel(body, out_type, n_idx_rows, name):
    return pl.kernel(
        body,
        out_type=out_type,
        mesh=plsc.VectorSubcoreMesh(core_axis_name="c", subcore_axis_name="s",
                                    num_cores=SC_CORES, num_subcores=SC_SUBCORES),
        scratch_types=[pltpu.VMEM((n_idx_rows, SC_CHUNK), I32),
                       pltpu.VMEM((2, SC_CHUNK, SLAB, LANES), U32),
                       pltpu.SemaphoreType.DMA,
                       pltpu.SemaphoreType.DMA((2,)),
                       pltpu.SemaphoreType.DMA((2,))],
        compiler_params=pltpu.CompilerParams(use_tc_tiling_on_sc=True),
        name=name,
    )


def _sc_worker():
    return lax.axis_index("s") * SC_CORES + lax.axis_index("c")


def _sc_load_indices(n_rows, src_of_row, idx_v, sem):
    @pl.loop(0, n_rows)
    def _(r):
        src, dst = src_of_row(r), idx_v.at[r]
        pltpu.make_async_copy(src, dst, sem).start()

    @pl.loop(0, n_rows)
    def _(r):
        src, dst = src_of_row(r), idx_v.at[r]
        pltpu.make_async_copy(src, dst, sem).wait()


def _sc_two_slot_pipeline(n_chunks, fill, fill_wait, drain, drain_wait):
    assert n_chunks % 2 == 0 and n_chunks >= 4
    fill(0, 0)

    @pl.loop(0, n_chunks // 2)
    def _(pair):
        c = 2 * pair

        @pl.when(pair >= 1)
        def _():
            drain_wait(c - 1, 1)

        fill(c + 1, 1)
        fill_wait(c, 0)
        drain(c, 0)

        @pl.when(pair < n_chunks // 2 - 1)
        def _():
            drain_wait(c, 0)
            fill(c + 2, 0)

        fill_wait(c + 1, 1)
        drain(c + 1, 1)

    drain_wait(n_chunks - 2, 0)
    drain_wait(n_chunks - 1, 1)


def _dispatch(pos_flat, h3):
    n_chunks = T_ALL // SC_WORKERS // SC_CHUNK

    def body(pos_hbm, h_hbm, xs_hbm, idx_v, rows_v, idx_sem, fill_sem, drain_sem):
        t_base = _sc_worker() * (n_chunks * SC_CHUNK)

        def index_src(r):
            t0 = t_base + (r // TOP_K) * SC_CHUNK
            return pos_hbm.at[pl.ds(pl.multiple_of((r % TOP_K) * T_ALL + t0, SC_CHUNK), SC_CHUNK)]

        _sc_load_indices(n_chunks * TOP_K, index_src, idx_v, idx_sem)

        def load(c, slot):
            t0 = pl.multiple_of(t_base + c * SC_CHUNK, SC_CHUNK)
            return pltpu.make_async_copy(h_hbm.at[pl.ds(t0, SC_CHUNK)], rows_v.at[slot], fill_sem.at[slot])

        def scatter(c, slot, k):
            return pltpu.make_async_copy(rows_v.at[slot], xs_hbm.at[idx_v.at[c * TOP_K + k]], drain_sem.at[slot])

        def drain(c, slot):
            for k in range(TOP_K):
                scatter(c, slot, k).start()

        def drain_wait(c, slot):
            for k in range(TOP_K):
                scatter(c, slot, k).wait()

        _sc_two_slot_pipeline(n_chunks, lambda c, s: load(c, s).start(), lambda c, s: load(c, s).wait(),
                              drain, drain_wait)

    return _sc_kernel(body, jax.ShapeDtypeStruct((P_SORT, SLAB, LANES), U32), n_chunks * TOP_K,
                      "sc_dispatch")(pos_flat, h3)


def _gather(pos_flat, y3):
    n_items = pos_flat.shape[0]
    n_chunks = n_items // SC_WORKERS // SC_CHUNK

    def body(pos_hbm, y_hbm, out_hbm, idx_v, rows_v, idx_sem, fill_sem, drain_sem):
        base = _sc_worker() * (n_chunks * SC_CHUNK)

        def index_src(r):
            return pos_hbm.at[pl.ds(pl.multiple_of(base + r * SC_CHUNK, SC_CHUNK), SC_CHUNK)]

        _sc_load_indices(n_chunks, index_src, idx_v, idx_sem)

        def gather(c, slot):
            return pltpu.make_async_copy(y_hbm.at[idx_v.at[c]], rows_v.at[slot], fill_sem.at[slot])

        def store(c, slot):
            i0 = pl.multiple_of(base + c * SC_CHUNK, SC_CHUNK)
            return pltpu.make_async_copy(rows_v.at[slot], out_hbm.at[pl.ds(i0, SC_CHUNK)], drain_sem.at[slot])

        _sc_two_slot_pipeline(n_chunks, lambda c, s: gather(c, s).start(), lambda c, s: gather(c, s).wait(),
                              lambda c, s: store(c, s).start(), lambda c, s: store(c, s).wait())

    return _sc_kernel(body, jax.ShapeDtypeStruct((n_items, SLAB, LANES), U32), n_chunks,
                      "sc_gather")(pos_flat, y3)


def _experts_kernel(layer, te_ref, nt_ref, end_ref, tend_ref, par_ref, x_ref, wg_hbm, wu_hbm, wd_hbm, y_ref,
                    wg_f, wu_f, wd_f, wg_bf, wu_bf, wd_bf, sem):
    j = pl.program_id(0)
    used = j < nt_ref[0]
    e = te_ref[j]
    first = (j == 0) | (te_ref[jnp.maximum(j - 1, 0)] != e)
    slot = par_ref[e]

    def fetch(expert, s):
        return [pltpu.make_async_copy(src.at[layer, expert], dst.at[s], sem.at[s])
                for src, dst in ((wg_hbm, wg_f), (wu_hbm, wu_f), (wd_hbm, wd_f))]

    @pl.when(used & (j == 0))
    def _():
        for cp in fetch(e, slot):
            cp.start()

    @pl.when(used & first)
    def _():
        for cp in fetch(e, slot):
            cp.wait()
        wg_bf[...] = wg_f[slot].astype(BF16)
        wu_bf[...] = wu_f[slot].astype(BF16)
        wd_bf[...] = wd_f[slot].astype(BF16)
        nxt = tend_ref[e]

        @pl.when(nxt < nt_ref[0])
        def _():
            for cp in fetch(te_ref[jnp.minimum(nxt, pl.num_programs(0) - 1)], 1 - slot):
                cp.start()

    @pl.when(used)
    def _():
        valid = end_ref[te_ref[j]] - j * TM
        rows = lax.broadcasted_iota(I32, (TM, HALF), 0)
        words = jnp.where(rows < valid, _load_slabs(x_ref, TM), jnp.uint32(0))
        xa, xb = _unpack_halves(words)
        xa, xb = xa.astype(BF16), xb.astype(BF16)
        g = (jnp.dot(xa, wg_bf[:HALF], preferred_element_type=F32)
             + jnp.dot(xb, wg_bf[HALF:], preferred_element_type=F32))
        u = (jnp.dot(xa, wu_bf[:HALF], preferred_element_type=F32)
             + jnp.dot(xb, wu_bf[HALF:], preferred_element_type=F32))
        a = (_silu(g) * u).astype(BF16)
        _store_slabs(y_ref, _pack_halves(jnp.dot(a, wd_bf[...], preferred_element_type=F32)))


def _experts(layer, te, nt, end, tile_end, parity, xs, w_gate, w_up, w_down):
    n_tiles = xs.shape[0] // (TM * SLAB)
    tile = lambda j, te, nt, *_: (jnp.minimum(j, nt[0] - 1), 0)
    return pl.pallas_call(
        functools.partial(_experts_kernel, layer),
        grid_spec=pltpu.PrefetchScalarGridSpec(
            num_scalar_prefetch=5,
            grid=(n_tiles,),
            in_specs=[pl.BlockSpec((TM * SLAB, LANES), tile),
                      pl.BlockSpec(memory_space=pl.ANY),
                      pl.BlockSpec(memory_space=pl.ANY),
                      pl.BlockSpec(memory_space=pl.ANY)],
            out_specs=pl.BlockSpec((TM * SLAB, LANES), tile),
            scratch_shapes=[pltpu.VMEM((2, D_MODEL, EXPERT_FF), F32),
                            pltpu.VMEM((2, D_MODEL, EXPERT_FF), F32),
                            pltpu.VMEM((2, EXPERT_FF, D_MODEL), F32),
                            pltpu.VMEM((D_MODEL, EXPERT_FF), BF16),
                            pltpu.VMEM((D_MODEL, EXPERT_FF), BF16),
                            pltpu.VMEM((EXPERT_FF, D_MODEL), BF16),
                            pltpu.SemaphoreType.DMA((2,))]),
        out_shape=jax.ShapeDtypeStruct(xs.shape, U32),
        compiler_params=_cparams(("arbitrary",)),
        name="experts",
    )(te, nt, end, tile_end, parity, xs, w_gate, w_up, w_down)


def _combine_kernel(final, wts_ref, x1_ref, ysh_ref, mod_ref, g_ref, buf_ref, *rest):
    o_ref, x2_ref, wb_ref = rest
    w = jnp.transpose(wts_ref[...])
    for k in range(TOP_K):
        wb_ref[k] = jnp.broadcast_to(w[:, k:k + 1], (TC_ROWS, LANES))
    sq = jnp.zeros((TC_ROWS, LANES), F32)
    for s in range(SLAB):
        acc_a, acc_b = _unpack_halves(ysh_ref[pl.ds(s, TC_ROWS, stride=SLAB), :])
        for k in range(TOP_K):
            ya, yb = _unpack_halves(buf_ref[k, pl.ds(s, TC_ROWS, stride=SLAB), :])
            acc_a = acc_a + wb_ref[k] * ya
            acc_b = acc_b + wb_ref[k] * yb
        for half, acc in ((0, acc_a), (1, acc_b)):
            cols = slice(half * HALF + s * LANES, half * HALF + (s + 1) * LANES)
            x2 = x1_ref[:, cols] + mod_ref[0, 0, 5:6, cols] * acc
            x2_ref[:, cols] = x2
            if final:
                sq = sq + x2 * x2
    out = x2_ref[...]
    if final:
        ms = jnp.sum(sq, axis=-1, keepdims=True) * (1.0 / D_MODEL)
        out = out * lax.rsqrt(ms + EPS) * g_ref[...]

    o_ref[...] = out


def _combine(layer, row0, n_rows, wts, x1, ysh, mods, final_norm, yg):
    final = layer == DEPTH - 1
    t0 = row0 // TC_ROWS
    row = lambda i: (t0 + i, 0)
    return pl.pallas_call(
        functools.partial(_combine_kernel, final),
        grid=(n_rows // TC_ROWS,),
        in_specs=[pl.BlockSpec((TOP_K, TC_ROWS), lambda i: (0, t0 + i)),
                  pl.BlockSpec((TC_ROWS, D_MODEL), row),
                  pl.BlockSpec((TC_ROWS * SLAB, LANES), row),
                  pl.BlockSpec((1, 1, N_MOD, D_MODEL),
                               lambda i: (0, _cond_row((t0 + i) // (TR // TC_ROWS)), 0, 0)),
                  pl.BlockSpec((1, D_MODEL), lambda i: (0, 0)),
                  pl.BlockSpec((TOP_K, TC_ROWS * SLAB, LANES), lambda i: (0, i, 0))],
        out_specs=pl.BlockSpec((TC_ROWS, D_MODEL), lambda i: (i, 0)),
        out_shape=jax.ShapeDtypeStruct((n_rows, D_MODEL), F32),
        scratch_shapes=[pltpu.VMEM((TC_ROWS, D_MODEL), F32),
                        pltpu.VMEM((TOP_K, TC_ROWS, LANES), F32)],
        compiler_params=_cparams(("arbitrary",)),
        name="combine",
    )(wts, x1, ysh, mods, final_norm.reshape(1, D_MODEL), yg)


def kernel(x_prompt, x_sample, cache_k, cache_v, c, c_ctx, w_mod, b_mod, g_mix, g_ffn, w_in, w_out, conv_w,
           attn_sink, w_router, router_bias, w_gate_e, w_up_e, w_down_e, w_gate_s, w_up_s, w_down_s, final_norm):
    ccsc = jnp.asarray(_channel_dft(), BF16)
    csl_ctx = jnp.asarray(_dft_tables(SEQ), BF16)
    csl_lat = jnp.asarray(_dft_tables(DEC_SEQ), BF16)
    rope = tuple(jnp.asarray(t, F32) for t in _rope_tables(DEC_SEQ, TR))
    tri = jnp.asarray(np.triu(np.ones((TR, TR)), 1), BF16)
    shared_te = jnp.zeros((T_ALL // TM,), I32)
    shared_nt = jnp.full((1,), T_ALL // TM, I32)
    shared_end = jnp.full((1,), T_ALL, I32)

    cond = jnp.concatenate([c_ctx[None], c, jnp.zeros((N_COND - 1 - DEC_BATCH, D_MODEL), F32)], axis=0)
    cache_k = cache_k.reshape(DEC_BATCH, DEPTH, SEQ, KV_W)
    cache_v = cache_v.reshape(DEC_BATCH, DEPTH, SEQ, KV_W)
    wr_t = jnp.swapaxes(w_router, 1, 2)
    wr_hi = wr_t.astype(BF16)
    wr_split = jnp.concatenate([wr_hi, (wr_t - wr_hi.astype(F32)).astype(BF16)], axis=1)
    ws_gate, ws_up, ws_down = w_gate_s[:, None], w_up_s[:, None], w_down_s[:, None]
    x = (x_prompt.reshape(T_CTX, D_MODEL), x_sample.reshape(T_LAT, D_MODEL))

    new_kv = ()
    for l in range(DEPTH):
        mods = _mods(l, cond, w_mod, b_mod)
        w_in_bf, w_out_bf = w_in[l].astype(BF16)[None], w_out[l].astype(BF16)[None]
        q, kv, fm, *new_kv = _inproj(l, x, mods, g_mix, w_in_bf, rope, tuple(new_kv))
        mix_ctx = _ctx_mix(l, attn_sink, q, kv, fm, ccsc, csl_ctx, conv_w)
        mix_lat = _lat_mix(l, attn_sink, q, kv, fm, cache_k, cache_v, ccsc, csl_lat, conv_w)
        x1, h2, ids, wts, rank, cnt = _outproj(l, mix_ctx, mix_lat, x, mods, g_ffn, w_out_bf, wr_split,
                                               router_bias, tri)
        pos, te, meta = _positions(ids, rank, cnt)
        pos_flat = pos.reshape(TOP_K * T_ALL)
        xs = _dispatch(pos_flat, h2.reshape(T_ALL, SLAB, LANES)).reshape(P_SORT * SLAB, LANES)
        y = _experts(l, te[0, :N_SORT_TILES], meta[2, :1], meta[0, :N_EXPERTS] + meta[1, :N_EXPERTS],
                     meta[3, :N_EXPERTS], meta[4, :N_EXPERTS], xs, w_gate_e, w_up_e, w_down_e)
        ysh = _experts(l, shared_te, shared_nt, shared_end, shared_nt, shared_te[:1], h2,
                       ws_gate, ws_up, ws_down)
        y3 = y.reshape(P_SORT, SLAB, LANES)
        x = []
        for row0, n_rows in ((0, T_CTX), (T_CTX, T_LAT)):
            part_pos = pos[:, row0:row0 + n_rows].reshape(TOP_K * n_rows)
            yg = _gather(part_pos, y3).reshape(TOP_K, n_rows * SLAB, LANES)
            x.append(_combine(l, row0, n_rows, wts, x1, ysh, mods, final_norm, yg))

    y_prompt = x[0].reshape(BATCH, SEQ, D_MODEL)
    y_sample = x[1].reshape(DEC_BATCH, DEC_SEQ, D_MODEL)
    new_k, new_v = new_kv
    return y_prompt, y_sample, new_k, new_v
```

```python
import functools

import numpy as np
import jax
import jax.numpy as jnp
from jax import lax
from jax.experimental import pallas as pl
from jax.experimental.pallas import tpu as pltpu
from jax.experimental.pallas import tpu_sc as plsc

F32 = jnp.float32
BF16 = jnp.bfloat16
I32 = jnp.int32
U32 = jnp.uint32

D_MODEL = 2048
HALF = D_MODEL // 2
BATCH, SEQ = 32, 256
DEC_BATCH, DEC_SEQ = 4, 1024
DEPTH = 2
GRID_W = 64
HEAD_DIM = 128
N_HEADS, N_KV_HEADS, Q_PER_KV = 8, 2, 4
ATTN_W, KV_W = 1024, 256
WINDOW, BLOCK = 128, 128
SCALE = HEAD_DIM ** -0.5
ROPE_BASE = 10000.0
ROPE_PAIRS = 32
FOURIER_W, FOURIER_GROUPS, FOURIER_GROUP_W = 512, 4, 128
CONV_W = 512
IN_W = 3584
N_EXPERTS, TOP_K, N_GROUPS, EXPERTS_PER_GROUP, TOPK_GROUPS = 64, 8, 8, 8, 4
EXPERT_FF = 512
ROUTE_SCALE = 2.5
N_MOD = 6
EPS = 1e-6
NEG = -1e30

T_CTX = BATCH * SEQ
T_LAT = DEC_BATCH * DEC_SEQ
T_ALL = T_CTX + T_LAT
TR = 512
N_CTX_TILES = T_CTX // TR
LAT_TILES_PER_SEQ = DEC_SEQ // TR
N_TILES = T_ALL // TR
N_COND = 8
MOD_TN = 1024
TM = 512
TC_ROWS = 256
N_SORT_TILES = T_ALL * TOP_K // TM + N_EXPERTS
P_SORT = N_SORT_TILES * TM
META_LANES = 128
TE_LANES = 512
VMEM_LIMIT = 56 * 1024 * 1024


def _cparams(sem=None):
    return pltpu.CompilerParams(dimension_semantics=sem, vmem_limit_bytes=VMEM_LIMIT)


def _silu(x):
    return x / (1.0 + jnp.exp(-x))


def _sigmoid(x):
    return 1.0 / (1.0 + jnp.exp(-x))


def _rmsnorm(x, g):
    return x * lax.rsqrt(jnp.mean(x * x, axis=-1, keepdims=True) + EPS) * g


def _cond_row(i):
    return jnp.where(i < N_CTX_TILES, 0, 1 + (i - N_CTX_TILES) // LAT_TILES_PER_SEQ)


def _pack_halves(v):
    n = v.shape[1] // 2
    hi = pltpu.bitcast(v[:, :n].astype(BF16).astype(F32), U32)
    lo = pltpu.bitcast(v[:, n:].astype(BF16).astype(F32), U32)
    return hi | (lo >> 16)


def _unpack_halves(w):
    return pltpu.bitcast(w & jnp.uint32(0xFFFF0000), F32), pltpu.bitcast(w << 16, F32)


SLAB, LANES = 8, 128


def _load_slabs(ref, n, row0=0):
    return jnp.concatenate([ref[pl.ds(row0 * SLAB + s, n, stride=SLAB), :] for s in range(SLAB)], axis=1)


def _store_slabs(ref, words, row0=0):
    n = words.shape[0]
    for s in range(SLAB):
        ref[pl.ds(row0 * SLAB + s, n, stride=SLAB), :] = words[:, s * LANES:(s + 1) * LANES]


def _dft_tables(L):
    j = np.arange(L, dtype=np.int64)
    ang = 2.0 * np.pi * ((j[:, None] * j[None, :]) % L).astype(np.float64) / L
    c, s = np.cos(ang) / np.sqrt(L), np.sin(ang) / np.sqrt(L)
    return np.concatenate([c, -s], axis=1)


def _channel_dft():
    n = FOURIER_GROUP_W
    j = np.arange(n, dtype=np.int64)
    ang = 2.0 * np.pi * ((j[:, None] * j[None, :]) % n).astype(np.float64) / n
    return np.concatenate([np.cos(ang), np.sin(ang)], axis=1) / np.sqrt(n)


def _rope_tables(S, n_identity):
    pos = np.arange(S + n_identity)
    pos[S:] = 0
    row, col = (pos // GRID_W).astype(np.float64), (pos % GRID_W).astype(np.float64)
    inv = ROPE_BASE ** (-np.arange(ROPE_PAIRS, dtype=np.float64) / ROPE_PAIRS)
    ar, ac = row[:, None] * inv, col[:, None] * inv
    z = np.zeros_like(ar)
    cos = np.concatenate([np.cos(ar), np.cos(ar), np.cos(ac), np.cos(ac)], axis=1)
    sin_hi = np.concatenate([-np.sin(ar), z, -np.sin(ac), z], axis=1)
    sin_lo = np.concatenate([z, np.sin(ar), z, np.sin(ac)], axis=1)
    return cos, sin_hi, sin_lo


def _mods_kernel(cond_ref, w_ref, b_ref, o_ref):
    s = _silu(cond_ref[...]).astype(BF16)
    o_ref[0] = jnp.dot(s, w_ref[0].astype(BF16), preferred_element_type=F32) + b_ref[0]


def _mods(cond, w_mod, b_mod):
    n = N_MOD * D_MODEL
    return pl.pallas_call(
        _mods_kernel,
        grid=(DEPTH, n // MOD_TN),
        in_specs=[pl.BlockSpec((N_COND, D_MODEL), lambda l, j: (0, 0)),
                  pl.BlockSpec((1, D_MODEL, MOD_TN), lambda l, j: (l, 0, j)),
                  pl.BlockSpec((1, 1, MOD_TN), lambda l, j: (l, 0, j))],
        out_specs=pl.BlockSpec((1, N_COND, MOD_TN), lambda l, j: (l, 0, j)),
        out_shape=jax.ShapeDtypeStruct((DEPTH, N_COND, n), F32),
        compiler_params=_cparams(("arbitrary", "arbitrary")),
        name="mods",
    )(cond, w_mod, b_mod.reshape(DEPTH, 1, n))


_CTX_ROWS = pl.BlockSpec((TR, D_MODEL), lambda i: (jnp.minimum(i, N_CTX_TILES - 1), 0))
_LAT_ROWS = pl.BlockSpec((TR, D_MODEL), lambda i: (jnp.maximum(i - N_CTX_TILES, 0), 0))


def _mod_spec(layer):
    return pl.BlockSpec((1, 1, N_MOD, D_MODEL), lambda i: (layer, _cond_row(i), 0, 0))


def _inproj_kernel(n_alias, xc_ref, xl_ref, mod_ref, g_ref, w_ref, cos_ref, shi_ref, slo_ref, *rest):
    q_ref, kv_ref, fm_ref, nk_ref, nv_ref = rest[n_alias:]
    i = pl.program_id(0)
    shift, scale = mod_ref[0, 0, 0:1, :], mod_ref[0, 0, 1:2, :]
    for r0 in range(0, TR, SEQ):
        rows = slice(r0, r0 + SEQ)
        x = jnp.where(i < N_CTX_TILES, xc_ref[rows, :], xl_ref[rows, :])
        h = (_rmsnorm(x, g_ref[0]) * (1.0 + scale) + shift).astype(BF16)
        fm_ref[rows, :] = jnp.dot(h, w_ref[0, :, ATTN_W + 2 * KV_W:], preferred_element_type=F32).astype(BF16)
        qkv = jnp.dot(h, w_ref[0, :, :ATTN_W + 2 * KV_W], preferred_element_type=F32)
        kv_ref[rows, KV_W:] = qkv[:, ATTN_W + KV_W:]
        cos, shi, slo = cos_ref[rows, :], shi_ref[rows, :], slo_ref[rows, :]
        for hd in range(N_HEADS + N_KV_HEADS):
            xh = qkv[:, hd * HEAD_DIM:(hd + 1) * HEAD_DIM]
            r = xh * cos + pltpu.roll(xh, HEAD_DIM - 32, 1) * shi + pltpu.roll(xh, 32, 1) * slo
            if hd < N_HEADS:
                q_ref[rows, hd * HEAD_DIM:(hd + 1) * HEAD_DIM] = r.astype(BF16)
            else:
                kv_ref[rows, (hd - N_HEADS) * HEAD_DIM:(hd - N_HEADS + 1) * HEAD_DIM] = r

    @pl.when(i < N_CTX_TILES)
    def _():
        for b in range(TR // SEQ):
            for hd in range(N_KV_HEADS):
                c0 = hd * HEAD_DIM
                nk_ref[b, 0, :, hd, :] = kv_ref[b * SEQ:(b + 1) * SEQ, c0:c0 + HEAD_DIM]
                nv_ref[b, 0, :, hd, :] = kv_ref[b * SEQ:(b + 1) * SEQ, KV_W + c0:KV_W + c0 + HEAD_DIM]


def _inproj(layer, x, mods, g_mix, w_in_bf, rope, new_kv):
    lat_blk = lambda i: (jnp.where(i < N_CTX_TILES, LAT_TILES_PER_SEQ, (i - N_CTX_TILES) % LAT_TILES_PER_SEQ), 0)
    row = lambda i: (i, 0)
    kv_blk = pl.BlockSpec((TR // SEQ, 1, SEQ, N_KV_HEADS, HEAD_DIM),
                          lambda i: (jnp.minimum(i, N_CTX_TILES - 1), layer, 0, 0, 0))
    kv_shape = jax.ShapeDtypeStruct((BATCH, DEPTH, SEQ, N_KV_HEADS, HEAD_DIM), F32)
    n_in = 8
    return pl.pallas_call(
        functools.partial(_inproj_kernel, len(new_kv)),
        grid=(N_TILES,),
        in_specs=[_CTX_ROWS, _LAT_ROWS,
                  _mod_spec(layer),
                  pl.BlockSpec((1, 1, D_MODEL), lambda i: (layer, 0, 0)),
                  pl.BlockSpec((1, D_MODEL, IN_W), lambda i: (layer, 0, 0), pipeline_mode=pl.Buffered(1)),
                  pl.BlockSpec((TR, HEAD_DIM), lat_blk),
                  pl.BlockSpec((TR, HEAD_DIM), lat_blk),
                  pl.BlockSpec((TR, HEAD_DIM), lat_blk)] + [pl.BlockSpec(memory_space=pl.ANY)] * len(new_kv),
        out_specs=[pl.BlockSpec((TR, ATTN_W), row),
                   pl.BlockSpec((TR, 2 * KV_W), row),
                   pl.BlockSpec((TR, FOURIER_W + 3 * CONV_W), row),
                   kv_blk, kv_blk],
        out_shape=[jax.ShapeDtypeStruct((T_ALL, ATTN_W), BF16),
                   jax.ShapeDtypeStruct((T_ALL, 2 * KV_W), F32),
                   jax.ShapeDtypeStruct((T_ALL, FOURIER_W + 3 * CONV_W), BF16),
                   kv_shape, kv_shape],
        input_output_aliases={n_in + a: 3 + a for a in range(len(new_kv))},
        compiler_params=_cparams(("arbitrary",)),
        name="inproj",
    )(*x, mods, g_mix.reshape(DEPTH, 1, D_MODEL), w_in_bf, *rope, *new_kv)


def _fourier_conv(fm_ref, ccsc_ref, csl_ref, cw_ref, o_ref):
    L = fm_ref.shape[0]
    for g in range(FOURIER_GROUPS):
        xg = fm_ref[:, g * FOURIER_GROUP_W:(g + 1) * FOURIER_GROUP_W]
        z = jnp.dot(xg, ccsc_ref[...], preferred_element_type=F32)
        zz = jnp.concatenate([z[:, :FOURIER_GROUP_W], z[:, FOURIER_GROUP_W:]], axis=0).astype(BF16)
        f = jnp.dot(csl_ref[...], zz, preferred_element_type=F32)
        o_ref[:, ATTN_W + g * FOURIER_GROUP_W:ATTN_W + (g + 1) * FOURIER_GROUP_W] = f.astype(BF16)
    xc = fm_ref[:, FOURIER_W:FOURIER_W + CONV_W].astype(F32)
    bg = fm_ref[:, FOURIER_W + CONV_W:FOURIER_W + 2 * CONV_W].astype(F32)
    cg = fm_ref[:, FOURIER_W + 2 * CONV_W:].astype(F32)
    u = cg * xc
    rows = lax.broadcasted_iota(I32, u.shape, 0)
    prev = jnp.where(rows == 0, 0.0, pltpu.roll(u, 1, 0))
    nxt = jnp.where(rows == L - 1, 0.0, pltpu.roll(u, L - 1, 0))
    y = prev * cw_ref[0, 0:1, :] + u * cw_ref[0, 1:2, :] + nxt * cw_ref[0, 2:3, :]
    o_ref[:, ATTN_W + FOURIER_W:] = (bg * y).astype(BF16)


def _ctx_mix_kernel(layer, sink_ref, q_ref, kv_ref, fm_ref, ccsc_ref, csl_ref, cw_ref, o_ref):
    for h in range(N_HEADS):
        g = h // Q_PER_KV
        q = q_ref[:, h * HEAD_DIM:(h + 1) * HEAD_DIM]
        k = kv_ref[:, g * HEAD_DIM:(g + 1) * HEAD_DIM].astype(BF16)
        v = kv_ref[:, KV_W + g * HEAD_DIM:KV_W + (g + 1) * HEAD_DIM].astype(BF16)
        s = lax.dot_general(q, k, (((1,), (1,)), ((), ())), preferred_element_type=F32) * SCALE
        sink = sink_ref[layer, h]
        m = jnp.maximum(jnp.max(s, axis=-1, keepdims=True), sink)
        e = jnp.exp(s - m)
        den = jnp.sum(e, axis=-1, keepdims=True) + jnp.exp(sink - m)
        o = jnp.dot(e.astype(BF16), v, preferred_element_type=F32) / den
        o_ref[:, h * HEAD_DIM:(h + 1) * HEAD_DIM] = o.astype(BF16)
    _fourier_conv(fm_ref, ccsc_ref, csl_ref, cw_ref, o_ref)


def _ctx_mix(layer, sink, q, kv, fm, ccsc, csl, conv_w):
    row = lambda b: (b, 0)
    const = lambda b: (0, 0)
    return pl.pallas_call(
        functools.partial(_ctx_mix_kernel, layer),
        grid=(BATCH,),
        in_specs=[pl.BlockSpec(memory_space=pltpu.SMEM),
                  pl.BlockSpec((SEQ, ATTN_W), row),
                  pl.BlockSpec((SEQ, 2 * KV_W), row),
                  pl.BlockSpec((SEQ, FOURIER_W + 3 * CONV_W), row),
                  pl.BlockSpec((FOURIER_GROUP_W, 2 * FOURIER_GROUP_W), const),
                  pl.BlockSpec((SEQ, 2 * SEQ), const),
                  pl.BlockSpec((1, 3, CONV_W), lambda b: (layer, 0, 0))],
        out_specs=pl.BlockSpec((SEQ, D_MODEL), row),
        out_shape=jax.ShapeDtypeStruct((T_CTX, D_MODEL), BF16),
        compiler_params=_cparams(("arbitrary",)),
        name="ctx_mix",
    )(sink, q, kv, fm, ccsc, csl, conv_w)


def _lat_mix_kernel(layer, sink_ref, q_ref, kv_ref, fm_ref, ck_ref, cv_ref, ccsc_ref, csl_ref, cw_ref, o_ref):
    n = pl.program_id(1)
    nb = DEC_SEQ // BLOCK
    rows = Q_PER_KV * BLOCK
    band = 3 * BLOCK

    @pl.when(n == 0)
    def _():
        _fourier_conv(fm_ref, ccsc_ref, csl_ref, cw_ref, o_ref)

    q0 = pl.multiple_of(n * BLOCK, BLOCK)
    k0 = pl.multiple_of(jnp.clip(n - 1, 0, nb - 3) * BLOCK, BLOCK)
    qpos = q0 + lax.broadcasted_iota(I32, (BLOCK, band), 0)
    kpos = k0 + lax.broadcasted_iota(I32, (BLOCK, band), 1)
    valid = jnp.abs(qpos - kpos) <= WINDOW
    for g in range(N_KV_HEADS):
        kb = kv_ref[pl.ds(k0, band), g * HEAD_DIM:(g + 1) * HEAD_DIM].astype(BF16)
        vb = kv_ref[pl.ds(k0, band), KV_W + g * HEAD_DIM:KV_W + (g + 1) * HEAD_DIM].astype(BF16)
        k_ctx = ck_ref[0, 0, :, g * HEAD_DIM:(g + 1) * HEAD_DIM].astype(BF16)
        v_ctx = cv_ref[0, 0, :, g * HEAD_DIM:(g + 1) * HEAD_DIM].astype(BF16)
        for h in range(g * Q_PER_KV, (g + 1) * Q_PER_KV):
            sink = sink_ref[layer, h]
            q = q_ref[pl.ds(q0, BLOCK), h * HEAD_DIM:(h + 1) * HEAD_DIM]
            s_loc = lax.dot_general(q, kb, (((1,), (1,)), ((), ())), preferred_element_type=F32) * SCALE
            s_loc = jnp.where(valid, s_loc, NEG)
            s_ctx = lax.dot_general(q, k_ctx, (((1,), (1,)), ((), ())), preferred_element_type=F32) * SCALE
            m = jnp.maximum(jnp.maximum(jnp.max(s_loc, axis=-1, keepdims=True),
                                        jnp.max(s_ctx, axis=-1, keepdims=True)), sink)
            e_loc, e_ctx = jnp.exp(s_loc - m), jnp.exp(s_ctx - m)
            den = (jnp.sum(e_loc, axis=-1, keepdims=True) + jnp.sum(e_ctx, axis=-1, keepdims=True)
                   + jnp.exp(sink - m))
            o = (jnp.dot(e_loc.astype(BF16), vb, preferred_element_type=F32)
                 + jnp.dot(e_ctx.astype(BF16), v_ctx, preferred_element_type=F32)) / den
            o_ref[pl.ds(q0, BLOCK), h * HEAD_DIM:(h + 1) * HEAD_DIM] = o.astype(BF16)


def _lat_mix(layer, sink, q, kv, fm, cache_k, cache_v, ccsc, csl, conv_w):
    off = T_CTX // DEC_SEQ
    row = lambda b, n: (off + b, 0)
    const = lambda b, n: (0, 0)
    cache = lambda b, n: (b, layer, 0, 0)
    return pl.pallas_call(
        functools.partial(_lat_mix_kernel, layer),
        grid=(DEC_BATCH, DEC_SEQ // BLOCK),
        in_specs=[pl.BlockSpec(memory_space=pltpu.SMEM),
                  pl.BlockSpec((DEC_SEQ, ATTN_W), row),
                  pl.BlockSpec((DEC_SEQ, 2 * KV_W), row),
                  pl.BlockSpec((DEC_SEQ, FOURIER_W + 3 * CONV_W), row),
                  pl.BlockSpec((1, 1, SEQ, KV_W), cache),
                  pl.BlockSpec((1, 1, SEQ, KV_W), cache),
                  pl.BlockSpec((FOURIER_GROUP_W, 2 * FOURIER_GROUP_W), const),
                  pl.BlockSpec((DEC_SEQ, 2 * DEC_SEQ), const, pipeline_mode=pl.Buffered(1)),
                  pl.BlockSpec((1, 3, CONV_W), lambda b, n: (layer, 0, 0))],
        out_specs=pl.BlockSpec((DEC_SEQ, D_MODEL), lambda b, n: (b, 0)),
        out_shape=jax.ShapeDtypeStruct((T_LAT, D_MODEL), BF16),
        compiler_params=_cparams(("arbitrary", "arbitrary")),
        name="lat_mix",
    )(sink, q, kv, fm, cache_k, cache_v, ccsc, csl, conv_w)


def _first_index_of_max(vals, iota, sentinel):
    mx = jnp.max(vals, axis=0, keepdims=True)
    idx = jnp.min(jnp.where(vals == mx, iota, sentinel), axis=0, keepdims=True)
    return mx, idx


def _outproj_kernel(mixc_ref, mixl_ref, xc_ref, xl_ref, mod_ref, g_ref, w_ref, wr_ref, rb_ref, tri_ref,
                    x1_ref, h2_ref, ids_ref, wts_ref, rank_ref, cnt_ref, run_ref):
    i = pl.program_id(0)

    @pl.when(i == 0)
    def _():
        run_ref[...] = jnp.zeros_like(run_ref)

    gate1, shift2, scale2 = mod_ref[0, 0, 2:3, :], mod_ref[0, 0, 3:4, :], mod_ref[0, 0, 4:5, :]
    mix = jnp.where(i < N_CTX_TILES, mixc_ref[...], mixl_ref[...])
    x = jnp.where(i < N_CTX_TILES, xc_ref[...], xl_ref[...])
    x1 = x + gate1 * jnp.dot(mix, w_ref[0], preferred_element_type=F32)
    x1_ref[...] = x1
    h2 = _rmsnorm(x1, g_ref[0]) * (1.0 + scale2) + shift2
    _store_slabs(h2_ref, _pack_halves(h2))

    nt = (((1,), (1,)), ((), ()))
    h_hi = h2.astype(BF16)
    h_lo = (h2 - h_hi.astype(F32)).astype(BF16)
    part = lax.dot_general(wr_ref[0], h_hi, nt, preferred_element_type=F32)
    logits = (part[:N_EXPERTS] + part[N_EXPERTS:]
              + lax.dot_general(wr_ref[0, :N_EXPERTS, :], h_lo, nt, preferred_element_type=F32))
    scores = _sigmoid(logits)
    biased = scores + rb_ref[0]
    ninf = -jnp.inf
    mem = lax.broadcasted_iota(I32, (EXPERTS_PER_GROUP, TR), 0)
    grp_rows = []
    for g in range(N_GROUPS):
        bgp = biased[g * EXPERTS_PER_GROUP:(g + 1) * EXPERTS_PER_GROUP]
        m1, i1 = _first_index_of_max(bgp, mem, EXPERTS_PER_GROUP)
        m2 = jnp.max(jnp.where(mem == i1, ninf, bgp), axis=0, keepdims=True)
        grp_rows.append(m1 + m2)
    cur = jnp.concatenate(grp_rows, axis=0)
    gio = lax.broadcasted_iota(I32, (N_GROUPS, TR), 0)
    gsel = jnp.zeros((N_GROUPS, TR), F32)
    for _ in range(TOPK_GROUPS):
        _, gi = _first_index_of_max(cur, gio, N_GROUPS)
        hit = gio == gi
        gsel = jnp.where(hit, 1.0, gsel)
        cur = jnp.where(hit, ninf, cur)
    cur = jnp.concatenate(
        [jnp.where(gsel[g:g + 1] > 0.0, biased[g * EXPERTS_PER_GROUP:(g + 1) * EXPERTS_PER_GROUP], ninf)
         for g in range(N_GROUPS)], axis=0)
    eio = lax.broadcasted_iota(I32, (N_EXPERTS, TR), 0)
    chosen = jnp.zeros((N_EXPERTS, TR), F32)
    ids, sel = [], []
    for _ in range(TOP_K):
        _, ei = _first_index_of_max(cur, eio, N_EXPERTS)
        hit = eio == ei
        ids.append(ei)
        sel.append(jnp.sum(jnp.where(hit, scores, 0.0), axis=0, keepdims=True))
        chosen = jnp.where(hit, 1.0, chosen)
        cur = jnp.where(hit, ninf, cur)
    total = functools.reduce(lambda a, b: a + b, sel)
    ids_ref[...] = jnp.concatenate(ids, axis=0)
    wts_ref[...] = jnp.concatenate([s / total * ROUTE_SCALE for s in sel], axis=0)

    before = jnp.dot(chosen.astype(BF16), tri_ref[...], preferred_element_type=F32) + run_ref[:, 0:1]
    rank_ref[...] = jnp.concatenate(
        [jnp.sum(jnp.where(eio == ei, before, 0.0), axis=0, keepdims=True) for ei in ids], axis=0).astype(I32)
    run = run_ref[...] + jnp.sum(chosen, axis=1, keepdims=True)
    run_ref[...] = run
    cnt_ref[...] = run


def _outproj(layer, mix_ctx, mix_lat, x, mods, g_ffn, w_out_bf, wr_split, router_bias, tri):
    row = lambda i: (i, 0)
    col = lambda i: (0, i)
    const = lambda i: (0, 0)
    lay = lambda i: (layer, 0, 0)
    return pl.pallas_call(
        _outproj_kernel,
        grid=(N_TILES,),
        in_specs=[_CTX_ROWS, _LAT_ROWS, _CTX_ROWS, _LAT_ROWS,
                  _mod_spec(layer),
                  pl.BlockSpec((1, 1, D_MODEL), lay),
                  pl.BlockSpec((1, D_MODEL, D_MODEL), lay, pipeline_mode=pl.Buffered(1)),
                  pl.BlockSpec((1, 2 * N_EXPERTS, D_MODEL), lay),
                  pl.BlockSpec((1, N_EXPERTS, 1), lay),
                  pl.BlockSpec((TR, TR), const)],
        out_specs=[pl.BlockSpec((TR, D_MODEL), row),
                   pl.BlockSpec((TR * SLAB, LANES), row),
                   pl.BlockSpec((TOP_K, TR), col),
                   pl.BlockSpec((TOP_K, TR), col),
                   pl.BlockSpec((TOP_K, TR), col),
                   pl.BlockSpec((N_EXPERTS, META_LANES), const)],
        out_shape=[jax.ShapeDtypeStruct((T_ALL, D_MODEL), F32),
                   jax.ShapeDtypeStruct((T_ALL * SLAB, LANES), U32),
                   jax.ShapeDtypeStruct((TOP_K, T_ALL), I32),
                   jax.ShapeDtypeStruct((TOP_K, T_ALL), F32),
                   jax.ShapeDtypeStruct((TOP_K, T_ALL), I32),
                   jax.ShapeDtypeStruct((N_EXPERTS, META_LANES), F32)],
        scratch_shapes=[pltpu.VMEM((N_EXPERTS, META_LANES), F32)],
        compiler_params=_cparams(("arbitrary",)),
        name="outproj",
    )(mix_ctx, mix_lat, *x, mods, g_ffn.reshape(DEPTH, 1, D_MODEL), w_out_bf, wr_split,
      router_bias.reshape(DEPTH, N_EXPERTS, 1), tri)


def _positions_kernel(ids_ref, rank_ref, cnt_ref, pos_ref, te_ref, meta_ref):
    ids = ids_ref[...]
    pos = rank_ref[...]
    tile_io = lax.broadcasted_iota(I32, (1, TE_LANES), 1)
    lane_io = lax.broadcasted_iota(I32, (1, META_LANES), 1)
    te = jnp.zeros((1, TE_LANES), I32)
    starts = jnp.zeros((1, META_LANES), I32)
    counts = jnp.zeros((1, META_LANES), I32)
    tile_ends = jnp.zeros((1, META_LANES), I32)
    parities = jnp.zeros((1, META_LANES), I32)
    start = jnp.zeros((1, 1), I32)
    n_seg = jnp.zeros((1, 1), I32)
    for e in range(N_EXPERTS):
        c = cnt_ref[e:e + 1, 0:1].astype(I32)
        pos = pos + jnp.where(ids == e, start, 0)
        starts = starts + jnp.where(lane_io == e, start, 0)
        counts = counts + jnp.where(lane_io == e, c, 0)
        parities = parities + jnp.where(lane_io == e, n_seg % 2, 0)
        start = start + ((c + (TM - 1)) // TM) * TM
        n_seg = n_seg + jnp.where(c > 0, 1, 0)
        tile_ends = tile_ends + jnp.where(lane_io == e, start // TM, 0)
        te = te + jnp.where(tile_io * TM >= start, 1, 0)
    pos_ref[...] = pos
    te_ref[...] = jnp.minimum(te, N_EXPERTS - 1)
    meta_ref[0:1, :] = starts
    meta_ref[1:2, :] = counts
    meta_ref[2:3, :] = jnp.broadcast_to(start // TM, (1, META_LANES))
    meta_ref[3:4, :] = tile_ends
    meta_ref[4:5, :] = parities
    meta_ref[5:8, :] = jnp.zeros((3, META_LANES), I32)


def _positions(ids, rank, cnt):
    return pl.pallas_call(
        _positions_kernel,
        out_shape=[jax.ShapeDtypeStruct((TOP_K, T_ALL), I32),
                   jax.ShapeDtypeStruct((1, TE_LANES), I32),
                   jax.ShapeDtypeStruct((8, META_LANES), I32)],
        compiler_params=_cparams(),
        name="positions",
    )(ids, rank, cnt)


SC_CORES, SC_SUBCORES = 2, 16
SC_WORKERS = SC_CORES * SC_SUBCORES
SC_CHUNK = 32


def _sc_kernel(body, out_type, n_idx_rows, name):
    return pl.kernel(
        body,
        out_type=out_type,
        mesh=plsc.VectorSubcoreMesh(core_axis_name="c", subcore_axis_name="s",
                                    num_cores=SC_CORES, num_subcores=SC_SUBCORES),
        scratch_types=[pltpu.VMEM((n_idx_rows, SC_CHUNK), I32),
                       pltpu.VMEM((2, SC_CHUNK, SLAB, LANES), U32),
                       pltpu.SemaphoreType.DMA,
                       pltpu.SemaphoreType.DMA((2,)),
                       pltpu.SemaphoreType.DMA((2,))],
        compiler_params=pltpu.CompilerParams(use_tc_tiling_on_sc=True),
        name=name,
    )


def _sc_worker():
    return lax.axis_index("s") * SC_CORES + lax.axis_index("c")


def _sc_load_indices(n_rows, src_of_row, idx_v, sem):
    @pl.loop(0, n_rows)
    def _(r):
        src, dst = src_of_row(r), idx_v.at[r]
        pltpu.make_async_copy(src, dst, sem).start()

    @pl.loop(0, n_rows)
    def _(r):
        src, dst = src_of_row(r), idx_v.at[r]
        pltpu.make_async_copy(src, dst, sem).wait()


def _sc_two_slot_pipeline(n_chunks, fill, fill_wait, drain, drain_wait):
    assert n_chunks % 2 == 0 and n_chunks >= 4
    fill(0, 0)

    @pl.loop(0, n_chunks // 2)
    def _(pair):
        c = 2 * pair

        @pl.when(pair >= 1)
        def _():
            drain_wait(c - 1, 1)

        fill(c + 1, 1)
        fill_wait(c, 0)
        drain(c, 0)

        @pl.when(pair < n_chunks // 2 - 1)
        def _():
            drain_wait(c, 0)
            fill(c + 2, 0)

        fill_wait(c + 1, 1)
        drain(c + 1, 1)

    drain_wait(n_chunks - 2, 0)
    drain_wait(n_chunks - 1, 1)


def _dispatch(pos_flat, h3):
    n_chunks = T_ALL // SC_WORKERS // SC_CHUNK

    def body(pos_hbm, h_hbm, xs_hbm, idx_v, rows_v, idx_sem, fill_sem, drain_sem):
        t_base = _sc_worker() * (n_chunks * SC_CHUNK)

        def index_src(r):
            t0 = t_base + (r // TOP_K) * SC_CHUNK
            return pos_hbm.at[pl.ds(pl.multiple_of((r % TOP_K) * T_ALL + t0, SC_CHUNK), SC_CHUNK)]

        _sc_load_indices(n_chunks * TOP_K, index_src, idx_v, idx_sem)

        def load(c, slot):
            t0 = pl.multiple_of(t_base + c * SC_CHUNK, SC_CHUNK)
            return pltpu.make_async_copy(h_hbm.at[pl.ds(t0, SC_CHUNK)], rows_v.at[slot], fill_sem.at[slot])

        def scatter(c, slot, k):
            return pltpu.make_async_copy(rows_v.at[slot], xs_hbm.at[idx_v.at[c * TOP_K + k]], drain_sem.at[slot])

        def drain(c, slot):
            for k in range(TOP_K):
                scatter(c, slot, k).start()

        def drain_wait(c, slot):
            for k in range(TOP_K):
                scatter(c, slot, k).wait()

        _sc_two_slot_pipeline(n_chunks, lambda c, s: load(c, s).start(), lambda c, s: load(c, s).wait(),
                              drain, drain_wait)

    return _sc_kernel(body, jax.ShapeDtypeStruct((P_SORT, SLAB, LANES), U32), n_chunks * TOP_K,
                      "sc_dispatch")(pos_flat, h3)


def _gather(pos_flat, y3):
    n_items = pos_flat.shape[0]
    n_chunks = n_items // SC_WORKERS // SC_CHUNK

    def body(pos_hbm, y_hbm, out_hbm, idx_v, rows_v, idx_sem, fill_sem, drain_sem):
        base = _sc_worker() * (n_chunks * SC_CHUNK)

        def index_src(r):
            return pos_hbm.at[pl.ds(pl.multiple_of(base + r * SC_CHUNK, SC_CHUNK), SC_CHUNK)]

        _sc_load_indices(n_chunks, index_src, idx_v, idx_sem)

        def gather(c, slot):
            return pltpu.make_async_copy(y_hbm.at[idx_v.at[c]], rows_v.at[slot], fill_sem.at[slot])

        def store(c, slot):
            i0 = pl.multiple_of(base + c * SC_CHUNK, SC_CHUNK)
            return pltpu.make_async_copy(rows_v.at[slot], out_hbm.at[pl.ds(i0, SC_CHUNK)], drain_sem.at[slot])

        _sc_two_slot_pipeline(n_chunks, lambda c, s: gather(c, s).start(), lambda c, s: gather(c, s).wait(),
                              lambda c, s: store(c, s).start(), lambda c, s: store(c, s).wait())

    return _sc_kernel(body, jax.ShapeDtypeStruct((n_items, SLAB, LANES), U32), n_chunks,
                      "sc_gather")(pos_flat, y3)


def _experts_kernel(layer, te_ref, nt_ref, end_ref, tend_ref, par_ref, x_ref, wg_hbm, wu_hbm, wd_hbm, y_ref,
                    wg_f, wu_f, wd_f, sem):
    j = pl.program_id(0)
    used = j < nt_ref[0]
    e = te_ref[j]
    first = (j == 0) | (te_ref[jnp.maximum(j - 1, 0)] != e)
    slot = par_ref[e]

    def fetch(expert, s):
        return [pltpu.make_async_copy(src.at[layer, expert], dst.at[s], sem.at[s])
                for src, dst in ((wg_hbm, wg_f), (wu_hbm, wu_f), (wd_hbm, wd_f))]

    @pl.when(used & (j == 0))
    def _():
        for cp in fetch(e, slot):
            cp.start()

    @pl.when(used & first)
    def _():
        for cp in fetch(e, slot):
            cp.wait()
        nxt = tend_ref[e]

        @pl.when(nxt < nt_ref[0])
        def _():
            for cp in fetch(te_ref[jnp.minimum(nxt, pl.num_programs(0) - 1)], 1 - slot):
                cp.start()

    @pl.when(used)
    def _():
        valid = end_ref[te_ref[j]] - j * TM
        rows = lax.broadcasted_iota(I32, (TM, HALF), 0)
        words = jnp.where(rows < valid, _load_slabs(x_ref, TM), jnp.uint32(0))
        xa, xb = _unpack_halves(words)
        xa, xb = xa.astype(BF16), xb.astype(BF16)
        g = (jnp.dot(xa, wg_f[slot, :HALF].astype(BF16), preferred_element_type=F32)
             + jnp.dot(xb, wg_f[slot, HALF:].astype(BF16), preferred_element_type=F32))
        u = (jnp.dot(xa, wu_f[slot, :HALF].astype(BF16), preferred_element_type=F32)
             + jnp.dot(xb, wu_f[slot, HALF:].astype(BF16), preferred_element_type=F32))
        a = (_silu(g) * u).astype(BF16)
        _store_slabs(y_ref, _pack_halves(jnp.dot(a, wd_f[slot].astype(BF16), preferred_element_type=F32)))


def _experts(layer, te, nt, end, tile_end, parity, xs, w_gate, w_up, w_down):
    n_tiles = xs.shape[0] // (TM * SLAB)
    tile = lambda j, te, nt, *_: (jnp.minimum(j, nt[0] - 1), 0)
    return pl.pallas_call(
        functools.partial(_experts_kernel, layer),
        grid_spec=pltpu.PrefetchScalarGridSpec(
            num_scalar_prefetch=5,
            grid=(n_tiles,),
            in_specs=[pl.BlockSpec((TM * SLAB, LANES), tile),
                      pl.BlockSpec(memory_space=pl.ANY),
                      pl.BlockSpec(memory_space=pl.ANY),
                      pl.BlockSpec(memory_space=pl.ANY)],
            out_specs=pl.BlockSpec((TM * SLAB, LANES), tile),
            scratch_shapes=[pltpu.VMEM((2, D_MODEL, EXPERT_FF), F32),
                            pltpu.VMEM((2, D_MODEL, EXPERT_FF), F32),
                            pltpu.VMEM((2, EXPERT_FF, D_MODEL), F32),
                            pltpu.SemaphoreType.DMA((2,))]),
        out_shape=jax.ShapeDtypeStruct(xs.shape, U32),
        compiler_params=_cparams(("arbitrary",)),
        name="experts",
    )(te, nt, end, tile_end, parity, xs, w_gate, w_up, w_down)


def _combine_kernel(final, wts_ref, x1_ref, ysh_ref, mod_ref, g_ref, buf_ref, *rest):
    o_ref, x2_ref, wb_ref = rest
    w = jnp.transpose(wts_ref[...])
    for k in range(TOP_K):
        wb_ref[k] = jnp.broadcast_to(w[:, k:k + 1], (TC_ROWS, LANES))
    sq = jnp.zeros((TC_ROWS, LANES), F32)
    for s in range(SLAB):
        acc_a, acc_b = _unpack_halves(ysh_ref[pl.ds(s, TC_ROWS, stride=SLAB), :])
        for k in range(TOP_K):
            ya, yb = _unpack_halves(buf_ref[k, pl.ds(s, TC_ROWS, stride=SLAB), :])
            acc_a = acc_a + wb_ref[k] * ya
            acc_b = acc_b + wb_ref[k] * yb
        for half, acc in ((0, acc_a), (1, acc_b)):
            cols = slice(half * HALF + s * LANES, half * HALF + (s + 1) * LANES)
            x2 = x1_ref[:, cols] + mod_ref[0, 0, 5:6, cols] * acc
            x2_ref[:, cols] = x2
            if final:
                sq = sq + x2 * x2
    out = x2_ref[...]
    if final:
        ms = jnp.sum(sq, axis=-1, keepdims=True) * (1.0 / D_MODEL)
        out = out * lax.rsqrt(ms + EPS) * g_ref[...]

    o_ref[...] = out


def _combine(layer, row0, n_rows, wts, x1, ysh, mods, final_norm, yg):
    final = layer == DEPTH - 1
    t0 = row0 // TC_ROWS
    row = lambda i: (t0 + i, 0)
    return pl.pallas_call(
        functools.partial(_combine_kernel, final),
        grid=(n_rows // TC_ROWS,),
        in_specs=[pl.BlockSpec((TOP_K, TC_ROWS), lambda i: (0, t0 + i)),
                  pl.BlockSpec((TC_ROWS, D_MODEL), row),
                  pl.BlockSpec((TC_ROWS * SLAB, LANES), row),
                  pl.BlockSpec((1, 1, N_MOD, D_MODEL),
                               lambda i: (layer, _cond_row((t0 + i) // (TR // TC_ROWS)), 0, 0)),
                  pl.BlockSpec((1, D_MODEL), lambda i: (0, 0)),
                  pl.BlockSpec((TOP_K, TC_ROWS * SLAB, LANES), lambda i: (0, i, 0))],
        out_specs=pl.BlockSpec((TC_ROWS, D_MODEL), lambda i: (i, 0)),
        out_shape=jax.ShapeDtypeStruct((n_rows, D_MODEL), F32),
        scratch_shapes=[pltpu.VMEM((TC_ROWS, D_MODEL), F32),
                        pltpu.VMEM((TOP_K, TC_ROWS, LANES), F32)],
        compiler_params=_cparams(("arbitrary",)),
        name="combine",
    )(wts, x1, ysh, mods, final_norm.reshape(1, D_MODEL), yg)


def kernel(x_prompt, x_sample, cache_k, cache_v, c, c_ctx, w_mod, b_mod, g_mix, g_ffn, w_in, w_out, conv_w,
           attn_sink, w_router, router_bias, w_gate_e, w_up_e, w_down_e, w_gate_s, w_up_s, w_down_s, final_norm):
    ccsc = jnp.asarray(_channel_dft(), BF16)
    csl_ctx = jnp.asarray(_dft_tables(SEQ), BF16)
    csl_lat = jnp.asarray(_dft_tables(DEC_SEQ), BF16)
    rope = tuple(jnp.asarray(t, F32) for t in _rope_tables(DEC_SEQ, TR))
    tri = jnp.asarray(np.triu(np.ones((TR, TR)), 1), BF16)
    shared_te = jnp.zeros((T_ALL // TM,), I32)
    shared_nt = jnp.full((1,), T_ALL // TM, I32)
    shared_end = jnp.full((1,), T_ALL, I32)

    cond = jnp.concatenate([c_ctx[None], c, jnp.zeros((N_COND - 1 - DEC_BATCH, D_MODEL), F32)], axis=0)
    mods = _mods(cond, w_mod, b_mod).reshape(DEPTH, N_COND, N_MOD, D_MODEL)
    cache_k = cache_k.reshape(DEC_BATCH, DEPTH, SEQ, KV_W)
    cache_v = cache_v.reshape(DEC_BATCH, DEPTH, SEQ, KV_W)
    w_in_bf, w_out_bf = w_in.astype(BF16), w_out.astype(BF16)
    wr_t = jnp.swapaxes(w_router, 1, 2)
    wr_hi = wr_t.astype(BF16)
    wr_split = jnp.concatenate([wr_hi, (wr_t - wr_hi.astype(F32)).astype(BF16)], axis=1)
    ws_gate, ws_up, ws_down = w_gate_s[:, None], w_up_s[:, None], w_down_s[:, None]
    x = (x_prompt.reshape(T_CTX, D_MODEL), x_sample.reshape(T_LAT, D_MODEL))

    new_kv = ()
    for l in range(DEPTH):
        q, kv, fm, *new_kv = _inproj(l, x, mods, g_mix, w_in_bf, rope, tuple(new_kv))
        mix_ctx = _ctx_mix(l, attn_sink, q, kv, fm, ccsc, csl_ctx, conv_w)
        mix_lat = _lat_mix(l, attn_sink, q, kv, fm, cache_k, cache_v, ccsc, csl_lat, conv_w)
        x1, h2, ids, wts, rank, cnt = _outproj(l, mix_ctx, mix_lat, x, mods, g_ffn, w_out_bf, wr_split,
                                               router_bias, tri)
        pos, te, meta = _positions(ids, rank, cnt)
        pos_flat = pos.reshape(TOP_K * T_ALL)
        xs = _dispatch(pos_flat, h2.reshape(T_ALL, SLAB, LANES)).reshape(P_SORT * SLAB, LANES)
        y = _experts(l, te[0, :N_SORT_TILES], meta[2, :1], meta[0, :N_EXPERTS] + meta[1, :N_EXPERTS],
                     meta[3, :N_EXPERTS], meta[4, :N_EXPERTS], xs, w_gate_e, w_up_e, w_down_e)
        ysh = _experts(l, shared_te, shared_nt, shared_end, shared_nt, shared_te[:1], h2,
                       ws_gate, ws_up, ws_down)
        y3 = y.reshape(P_SORT, SLAB, LANES)
        x = []
        for row0, n_rows in ((0, T_CTX), (T_CTX, T_LAT)):
            part_pos = pos[:, row0:row0 + n_rows].reshape(TOP_K * n_rows)
            yg = _gather(part_pos, y3).reshape(TOP_K, n_rows * SLAB, LANES)
            x.append(_combine(l, row0, n_rows, wts, x1, ysh, mods, final_norm, yg))

    y_prompt = x[0].reshape(BATCH, SEQ, D_MODEL)
    y_sample = x[1].reshape(DEC_BATCH, DEC_SEQ, D_MODEL)
    new_k, new_v = new_kv
    return y_prompt, y_sample, new_k, new_v
```

```python
import functools

import numpy as np
import jax
import jax.numpy as jnp
from jax import lax
from jax.experimental import pallas as pl
from jax.experimental.pallas import tpu as pltpu
from jax.experimental.pallas import tpu_sc as plsc

F32 = jnp.float32
BF16 = jnp.bfloat16
I32 = jnp.int32
U32 = jnp.uint32

D_MODEL = 2048
HALF = D_MODEL // 2
BATCH, SEQ = 32, 256
DEC_BATCH, DEC_SEQ = 4, 1024
DEPTH = 2
GRID_W = 64
HEAD_DIM = 128
N_HEADS, N_KV_HEADS, Q_PER_KV = 8, 2, 4
ATTN_W, KV_W = 1024, 256
WINDOW, BLOCK = 128, 128
SCALE = HEAD_DIM ** -0.5
ROPE_BASE = 10000.0
ROPE_PAIRS = 32
FOURIER_W, FOURIER_GROUPS, FOURIER_GROUP_W = 512, 4, 128
CONV_W = 512
IN_W = 3584
N_EXPERTS, TOP_K, N_GROUPS, EXPERTS_PER_GROUP, TOPK_GROUPS = 64, 8, 8, 8, 4
EXPERT_FF = 512
ROUTE_SCALE = 2.5
N_MOD = 6
EPS = 1e-6
NEG = -1e30

T_CTX = BATCH * SEQ
T_LAT = DEC_BATCH * DEC_SEQ
T_ALL = T_CTX + T_LAT
TR = 512
N_CTX_TILES = T_CTX // TR
LAT_TILES_PER_SEQ = DEC_SEQ // TR
N_TILES = T_ALL // TR
N_COND = 8
MOD_TN = 1024
TM = 512
TC_ROWS = 256
N_SORT_TILES = T_ALL * TOP_K // TM + N_EXPERTS
P_SORT = N_SORT_TILES * TM
META_LANES = 128
TE_LANES = 512
VMEM_LIMIT = 56 * 1024 * 1024


def _cparams(sem=None):
    return pltpu.CompilerParams(dimension_semantics=sem, vmem_limit_bytes=VMEM_LIMIT)


def _silu(x):
    return x / (1.0 + jnp.exp(-x))


def _sigmoid(x):
    return 1.0 / (1.0 + jnp.exp(-x))


def _rmsnorm(x, g):
    return x * lax.rsqrt(jnp.mean(x * x, axis=-1, keepdims=True) + EPS) * g


def _cond_row(i):
    return jnp.where(i < N_CTX_TILES, 0, 1 + (i - N_CTX_TILES) // LAT_TILES_PER_SEQ)


def _pack_halves(v):
    n = v.shape[1] // 2
    hi = pltpu.bitcast(v[:, :n].astype(BF16).astype(F32), U32)
    lo = pltpu.bitcast(v[:, n:].astype(BF16).astype(F32), U32)
    return hi | (lo >> 16)


def _unpack_halves(w):
    return pltpu.bitcast(w & jnp.uint32(0xFFFF0000), F32), pltpu.bitcast(w << 16, F32)


SLAB, LANES = 8, 128


def _load_slabs(ref, n, row0=0):
    return jnp.concatenate([ref[pl.ds(row0 * SLAB + s, n, stride=SLAB), :] for s in range(SLAB)], axis=1)


def _store_slabs(ref, words, row0=0):
    n = words.shape[0]
    for s in range(SLAB):
        ref[pl.ds(row0 * SLAB + s, n, stride=SLAB), :] = words[:, s * LANES:(s + 1) * LANES]


def _dft_tables(L):
    j = np.arange(L, dtype=np.int64)
    ang = 2.0 * np.pi * ((j[:, None] * j[None, :]) % L).astype(np.float64) / L
    c, s = np.cos(ang) / np.sqrt(L), np.sin(ang) / np.sqrt(L)
    return np.concatenate([c, -s], axis=1)


def _channel_dft():
    n = FOURIER_GROUP_W
    j = np.arange(n, dtype=np.int64)
    ang = 2.0 * np.pi * ((j[:, None] * j[None, :]) % n).astype(np.float64) / n
    return np.concatenate([np.cos(ang), np.sin(ang)], axis=1) / np.sqrt(n)


def _rope_tables(S, n_identity):
    pos = np.arange(S + n_identity)
    pos[S:] = 0
    row, col = (pos // GRID_W).astype(np.float64), (pos % GRID_W).astype(np.float64)
    inv = ROPE_BASE ** (-np.arange(ROPE_PAIRS, dtype=np.float64) / ROPE_PAIRS)
    ar, ac = row[:, None] * inv, col[:, None] * inv
    z = np.zeros_like(ar)
    cos = np.concatenate([np.cos(ar), np.cos(ar), np.cos(ac), np.cos(ac)], axis=1)
    sin_hi = np.concatenate([-np.sin(ar), z, -np.sin(ac), z], axis=1)
    sin_lo = np.concatenate([z, np.sin(ar), z, np.sin(ac)], axis=1)
    return cos, sin_hi, sin_lo


def _mods_kernel(cond_ref, w_ref, b_ref, o_ref):
    s = _silu(cond_ref[...]).astype(BF16)
    o_ref[0] = jnp.dot(s, w_ref[0].astype(BF16), preferred_element_type=F32) + b_ref[0]


def _mods(cond, w_mod, b_mod):
    n = N_MOD * D_MODEL
    return pl.pallas_call(
        _mods_kernel,
        grid=(DEPTH, n // MOD_TN),
        in_specs=[pl.BlockSpec((N_COND, D_MODEL), lambda l, j: (0, 0)),
                  pl.BlockSpec((1, D_MODEL, MOD_TN), lambda l, j: (l, 0, j)),
                  pl.BlockSpec((1, 1, MOD_TN), lambda l, j: (l, 0, j))],
        out_specs=pl.BlockSpec((1, N_COND, MOD_TN), lambda l, j: (l, 0, j)),
        out_shape=jax.ShapeDtypeStruct((DEPTH, N_COND, n), F32),
        compiler_params=_cparams(("arbitrary", "arbitrary")),
        name="mods",
    )(cond, w_mod, b_mod.reshape(DEPTH, 1, n))


_CTX_ROWS = pl.BlockSpec((TR, D_MODEL), lambda i: (jnp.minimum(i, N_CTX_TILES - 1), 0))
_LAT_ROWS = pl.BlockSpec((TR, D_MODEL), lambda i: (jnp.maximum(i - N_CTX_TILES, 0), 0))


def _mod_spec(layer):
    return pl.BlockSpec((1, 1, N_MOD, D_MODEL), lambda i: (layer, _cond_row(i), 0, 0))


def _inproj_kernel(n_alias, xc_ref, xl_ref, mod_ref, g_ref, w_ref, cos_ref, shi_ref, slo_ref, *rest):
    q_ref, kv_ref, fm_ref, nk_ref, nv_ref = rest[n_alias:]
    i = pl.program_id(0)
    shift, scale = mod_ref[0, 0, 0:1, :], mod_ref[0, 0, 1:2, :]
    for r0 in range(0, TR, SEQ):
        rows = slice(r0, r0 + SEQ)
        x = jnp.where(i < N_CTX_TILES, xc_ref[rows, :], xl_ref[rows, :])
        h = (_rmsnorm(x, g_ref[0]) * (1.0 + scale) + shift).astype(BF16)
        fm_ref[rows, :] = jnp.dot(h, w_ref[0, :, ATTN_W + 2 * KV_W:], preferred_element_type=F32).astype(BF16)
        qkv = jnp.dot(h, w_ref[0, :, :ATTN_W + 2 * KV_W], preferred_element_type=F32)
        kv_ref[rows, KV_W:] = qkv[:, ATTN_W + KV_W:]
        cos, shi, slo = cos_ref[rows, :], shi_ref[rows, :], slo_ref[rows, :]
        for hd in range(N_HEADS + N_KV_HEADS):
            xh = qkv[:, hd * HEAD_DIM:(hd + 1) * HEAD_DIM]
            r = xh * cos + pltpu.roll(xh, HEAD_DIM - 32, 1) * shi + pltpu.roll(xh, 32, 1) * slo
            if hd < N_HEADS:
                q_ref[rows, hd * HEAD_DIM:(hd + 1) * HEAD_DIM] = r.astype(BF16)
            else:
                kv_ref[rows, (hd - N_HEADS) * HEAD_DIM:(hd - N_HEADS + 1) * HEAD_DIM] = r

    @pl.when(i < N_CTX_TILES)
    def _():
        for b in range(TR // SEQ):
            for hd in range(N_KV_HEADS):
                c0 = hd * HEAD_DIM
                nk_ref[b, 0, :, hd, :] = kv_ref[b * SEQ:(b + 1) * SEQ, c0:c0 + HEAD_DIM]
                nv_ref[b, 0, :, hd, :] = kv_ref[b * SEQ:(b + 1) * SEQ, KV_W + c0:KV_W + c0 + HEAD_DIM]


def _inproj(layer, x, mods, g_mix, w_in_bf, rope, new_kv):
    lat_blk = lambda i: (jnp.where(i < N_CTX_TILES, LAT_TILES_PER_SEQ, (i - N_CTX_TILES) % LAT_TILES_PER_SEQ), 0)
    row = lambda i: (i, 0)
    kv_blk = pl.BlockSpec((TR // SEQ, 1, SEQ, N_KV_HEADS, HEAD_DIM),
                          lambda i: (jnp.minimum(i, N_CTX_TILES - 1), layer, 0, 0, 0))
    kv_shape = jax.ShapeDtypeStruct((BATCH, DEPTH, SEQ, N_KV_HEADS, HEAD_DIM), F32)
    n_in = 8
    return pl.pallas_call(
        functools.partial(_inproj_kernel, len(new_kv)),
        grid=(N_TILES,),
        in_specs=[_CTX_ROWS, _LAT_ROWS,
                  _mod_spec(layer),
                  pl.BlockSpec((1, 1, D_MODEL), lambda i: (layer, 0, 0)),
                  pl.BlockSpec((1, D_MODEL, IN_W), lambda i: (layer, 0, 0), pipeline_mode=pl.Buffered(1)),
                  pl.BlockSpec((TR, HEAD_DIM), lat_blk),
                  pl.BlockSpec((TR, HEAD_DIM), lat_blk),
                  pl.BlockSpec((TR, HEAD_DIM), lat_blk)] + [pl.BlockSpec(memory_space=pl.ANY)] * len(new_kv),
        out_specs=[pl.BlockSpec((TR, ATTN_W), row),
                   pl.BlockSpec((TR, 2 * KV_W), row),
                   pl.BlockSpec((TR, FOURIER_W + 3 * CONV_W), row),
                   kv_blk, kv_blk],
        out_shape=[jax.ShapeDtypeStruct((T_ALL, ATTN_W), BF16),
                   jax.ShapeDtypeStruct((T_ALL, 2 * KV_W), F32),
                   jax.ShapeDtypeStruct((T_ALL, FOURIER_W + 3 * CONV_W), BF16),
                   kv_shape, kv_shape],
        input_output_aliases={n_in + a: 3 + a for a in range(len(new_kv))},
        compiler_params=_cparams(("arbitrary",)),
        name="inproj",
    )(*x, mods, g_mix.reshape(DEPTH, 1, D_MODEL), w_in_bf, *rope, *new_kv)


def _fourier_conv(fm_ref, ccsc_ref, csl_ref, cw_ref, o_ref):
    L = fm_ref.shape[0]
    for g in range(FOURIER_GROUPS):
        xg = fm_ref[:, g * FOURIER_GROUP_W:(g + 1) * FOURIER_GROUP_W]
        z = jnp.dot(xg, ccsc_ref[...], preferred_element_type=F32)
        zz = jnp.concatenate([z[:, :FOURIER_GROUP_W], z[:, FOURIER_GROUP_W:]], axis=0).astype(BF16)
        f = jnp.dot(csl_ref[...], zz, preferred_element_type=F32)
        o_ref[:, ATTN_W + g * FOURIER_GROUP_W:ATTN_W + (g + 1) * FOURIER_GROUP_W] = f.astype(BF16)
    xc = fm_ref[:, FOURIER_W:FOURIER_W + CONV_W].astype(F32)
    bg = fm_ref[:, FOURIER_W + CONV_W:FOURIER_W + 2 * CONV_W].astype(F32)
    cg = fm_ref[:, FOURIER_W + 2 * CONV_W:].astype(F32)
    u = cg * xc
    rows = lax.broadcasted_iota(I32, u.shape, 0)
    prev = jnp.where(rows == 0, 0.0, pltpu.roll(u, 1, 0))
    nxt = jnp.where(rows == L - 1, 0.0, pltpu.roll(u, L - 1, 0))
    y = prev * cw_ref[0, 0:1, :] + u * cw_ref[0, 1:2, :] + nxt * cw_ref[0, 2:3, :]
    o_ref[:, ATTN_W + FOURIER_W:] = (bg * y).astype(BF16)


CTX_SEQS_PER_STEP = 4


def _ctx_mix_kernel(layer, sink_ref, q_ref, kv_ref, fm_ref, ccsc_ref, csl_ref, cw_ref, o_ref):
    for s in range(CTX_SEQS_PER_STEP):
        seq = pl.ds(s * SEQ, SEQ)
        _ctx_mix_one(layer, sink_ref, q_ref.at[seq], kv_ref.at[seq], fm_ref.at[seq], ccsc_ref, csl_ref, cw_ref,
                     o_ref.at[seq])


def _ctx_mix_one(layer, sink_ref, q_ref, kv_ref, fm_ref, ccsc_ref, csl_ref, cw_ref, o_ref):
    for h in range(N_HEADS):
        g = h // Q_PER_KV
        q = q_ref[:, h * HEAD_DIM:(h + 1) * HEAD_DIM]
        k = kv_ref[:, g * HEAD_DIM:(g + 1) * HEAD_DIM].astype(BF16)
        v = kv_ref[:, KV_W + g * HEAD_DIM:KV_W + (g + 1) * HEAD_DIM].astype(BF16)
        s = lax.dot_general(q, k, (((1,), (1,)), ((), ())), preferred_element_type=F32) * SCALE
        sink = sink_ref[layer, h]
        m = jnp.maximum(jnp.max(s, axis=-1, keepdims=True), sink)
        e = jnp.exp(s - m)
        den = jnp.sum(e, axis=-1, keepdims=True) + jnp.exp(sink - m)
        o = jnp.dot(e.astype(BF16), v, preferred_element_type=F32) / den
        o_ref[:, h * HEAD_DIM:(h + 1) * HEAD_DIM] = o.astype(BF16)
    _fourier_conv(fm_ref, ccsc_ref, csl_ref, cw_ref, o_ref)


def _ctx_mix(layer, sink, q, kv, fm, ccsc, csl, conv_w):
    row = lambda b: (b, 0)
    const = lambda b: (0, 0)
    return pl.pallas_call(
        functools.partial(_ctx_mix_kernel, layer),
        grid=(BATCH // CTX_SEQS_PER_STEP,),
        in_specs=[pl.BlockSpec(memory_space=pltpu.SMEM),
                  pl.BlockSpec((CTX_SEQS_PER_STEP * SEQ, ATTN_W), row),
                  pl.BlockSpec((CTX_SEQS_PER_STEP * SEQ, 2 * KV_W), row),
                  pl.BlockSpec((CTX_SEQS_PER_STEP * SEQ, FOURIER_W + 3 * CONV_W), row),
                  pl.BlockSpec((FOURIER_GROUP_W, 2 * FOURIER_GROUP_W), const),
                  pl.BlockSpec((SEQ, 2 * SEQ), const),
                  pl.BlockSpec((1, 3, CONV_W), lambda b: (layer, 0, 0))],
        out_specs=pl.BlockSpec((CTX_SEQS_PER_STEP * SEQ, D_MODEL), row),
        out_shape=jax.ShapeDtypeStruct((T_CTX, D_MODEL), BF16),
        compiler_params=_cparams(("arbitrary",)),
        name="ctx_mix",
    )(sink, q, kv, fm, ccsc, csl, conv_w)


def _lat_mix_kernel(layer, sink_ref, q_ref, kv_ref, fm_ref, ck_ref, cv_ref, ccsc_ref, csl_ref, cw_ref, o_ref):
    n = pl.program_id(1)
    nb = DEC_SEQ // BLOCK
    rows = Q_PER_KV * BLOCK
    band = 3 * BLOCK

    @pl.when(n == 0)
    def _():
        _fourier_conv(fm_ref, ccsc_ref, csl_ref, cw_ref, o_ref)

    q0 = pl.multiple_of(n * BLOCK, BLOCK)
    k0 = pl.multiple_of(jnp.clip(n - 1, 0, nb - 3) * BLOCK, BLOCK)
    qpos = q0 + lax.broadcasted_iota(I32, (BLOCK, band), 0)
    kpos = k0 + lax.broadcasted_iota(I32, (BLOCK, band), 1)
    valid = jnp.abs(qpos - kpos) <= WINDOW
    for g in range(N_KV_HEADS):
        kb = kv_ref[pl.ds(k0, band), g * HEAD_DIM:(g + 1) * HEAD_DIM].astype(BF16)
        vb = kv_ref[pl.ds(k0, band), KV_W + g * HEAD_DIM:KV_W + (g + 1) * HEAD_DIM].astype(BF16)
        k_ctx = ck_ref[0, 0, :, g * HEAD_DIM:(g + 1) * HEAD_DIM].astype(BF16)
        v_ctx = cv_ref[0, 0, :, g * HEAD_DIM:(g + 1) * HEAD_DIM].astype(BF16)
        for h in range(g * Q_PER_KV, (g + 1) * Q_PER_KV):
            sink = sink_ref[layer, h]
            q = q_ref[pl.ds(q0, BLOCK), h * HEAD_DIM:(h + 1) * HEAD_DIM]
            s_loc = lax.dot_general(q, kb, (((1,), (1,)), ((), ())), preferred_element_type=F32) * SCALE
            s_loc = jnp.where(valid, s_loc, NEG)
            s_ctx = lax.dot_general(q, k_ctx, (((1,), (1,)), ((), ())), preferred_element_type=F32) * SCALE
            m = jnp.maximum(jnp.maximum(jnp.max(s_loc, axis=-1, keepdims=True),
                                        jnp.max(s_ctx, axis=-1, keepdims=True)), sink)
            e_loc, e_ctx = jnp.exp(s_loc - m), jnp.exp(s_ctx - m)
            den = (jnp.sum(e_loc, axis=-1, keepdims=True) + jnp.sum(e_ctx, axis=-1, keepdims=True)
                   + jnp.exp(sink - m))
            o = (jnp.dot(e_loc.astype(BF16), vb, preferred_element_type=F32)
                 + jnp.dot(e_ctx.astype(BF16), v_ctx, preferred_element_type=F32)) / den
            o_ref[pl.ds(q0, BLOCK), h * HEAD_DIM:(h + 1) * HEAD_DIM] = o.astype(BF16)


def _lat_mix(layer, sink, q, kv, fm, cache_k, cache_v, ccsc, csl, conv_w):
    off = T_CTX // DEC_SEQ
    row = lambda b, n: (off + b, 0)
    const = lambda b, n: (0, 0)
    cache = lambda b, n: (b, layer, 0, 0)
    return pl.pallas_call(
        functools.partial(_lat_mix_kernel, layer),
        grid=(DEC_BATCH, DEC_SEQ // BLOCK),
        in_specs=[pl.BlockSpec(memory_space=pltpu.SMEM),
                  pl.BlockSpec((DEC_SEQ, ATTN_W), row),
                  pl.BlockSpec((DEC_SEQ, 2 * KV_W), row),
                  pl.BlockSpec((DEC_SEQ, FOURIER_W + 3 * CONV_W), row),
                  pl.BlockSpec((1, 1, SEQ, KV_W), cache),
                  pl.BlockSpec((1, 1, SEQ, KV_W), cache),
                  pl.BlockSpec((FOURIER_GROUP_W, 2 * FOURIER_GROUP_W), const),
                  pl.BlockSpec((DEC_SEQ, 2 * DEC_SEQ), const, pipeline_mode=pl.Buffered(1)),
                  pl.BlockSpec((1, 3, CONV_W), lambda b, n: (layer, 0, 0))],
        out_specs=pl.BlockSpec((DEC_SEQ, D_MODEL), lambda b, n: (b, 0)),
        out_shape=jax.ShapeDtypeStruct((T_LAT, D_MODEL), BF16),
        compiler_params=_cparams(("arbitrary", "arbitrary")),
        name="lat_mix",
    )(sink, q, kv, fm, cache_k, cache_v, ccsc, csl, conv_w)


def _first_index_of_max(vals, iota, sentinel):
    mx = jnp.max(vals, axis=0, keepdims=True)
    idx = jnp.min(jnp.where(vals == mx, iota, sentinel), axis=0, keepdims=True)
    return mx, idx


def _outproj_kernel(mixc_ref, mixl_ref, xc_ref, xl_ref, mod_ref, g_ref, w_ref, wr_ref, rb_ref, tri_ref,
                    x1_ref, h2_ref, ids_ref, wts_ref, rank_ref, cnt_ref, run_ref):
    i = pl.program_id(0)

    @pl.when(i == 0)
    def _():
        run_ref[...] = jnp.zeros_like(run_ref)

    gate1, shift2, scale2 = mod_ref[0, 0, 2:3, :], mod_ref[0, 0, 3:4, :], mod_ref[0, 0, 4:5, :]
    mix = jnp.where(i < N_CTX_TILES, mixc_ref[...], mixl_ref[...])
    x = jnp.where(i < N_CTX_TILES, xc_ref[...], xl_ref[...])
    x1 = x + gate1 * jnp.dot(mix, w_ref[0], preferred_element_type=F32)
    x1_ref[...] = x1
    h2 = _rmsnorm(x1, g_ref[0]) * (1.0 + scale2) + shift2
    _store_slabs(h2_ref, _pack_halves(h2))

    nt = (((1,), (1,)), ((), ()))
    h_hi = h2.astype(BF16)
    h_lo = (h2 - h_hi.astype(F32)).astype(BF16)
    part = lax.dot_general(wr_ref[0], h_hi, nt, preferred_element_type=F32)
    logits = (part[:N_EXPERTS] + part[N_EXPERTS:]
              + lax.dot_general(wr_ref[0, :N_EXPERTS, :], h_lo, nt, preferred_element_type=F32))
    scores = _sigmoid(logits)
    biased = scores + rb_ref[0]
    ninf = -jnp.inf
    mem = lax.broadcasted_iota(I32, (EXPERTS_PER_GROUP, TR), 0)
    grp_rows = []
    for g in range(N_GROUPS):
        bgp = biased[g * EXPERTS_PER_GROUP:(g + 1) * EXPERTS_PER_GROUP]
        m1, i1 = _first_index_of_max(bgp, mem, EXPERTS_PER_GROUP)
        m2 = jnp.max(jnp.where(mem == i1, ninf, bgp), axis=0, keepdims=True)
        grp_rows.append(m1 + m2)
    cur = jnp.concatenate(grp_rows, axis=0)
    gio = lax.broadcasted_iota(I32, (N_GROUPS, TR), 0)
    gsel = jnp.zeros((N_GROUPS, TR), F32)
    for _ in range(TOPK_GROUPS):
        _, gi = _first_index_of_max(cur, gio, N_GROUPS)
        hit = gio == gi
        gsel = jnp.where(hit, 1.0, gsel)
        cur = jnp.where(hit, ninf, cur)
    cur = jnp.concatenate(
        [jnp.where(gsel[g:g + 1] > 0.0, biased[g * EXPERTS_PER_GROUP:(g + 1) * EXPERTS_PER_GROUP], ninf)
         for g in range(N_GROUPS)], axis=0)
    eio = lax.broadcasted_iota(I32, (N_EXPERTS, TR), 0)
    chosen = jnp.zeros((N_EXPERTS, TR), F32)
    ids, sel = [], []
    for _ in range(TOP_K):
        _, ei = _first_index_of_max(cur, eio, N_EXPERTS)
        hit = eio == ei
        ids.append(ei)
        sel.append(jnp.sum(jnp.where(hit, scores, 0.0), axis=0, keepdims=True))
        chosen = jnp.where(hit, 1.0, chosen)
        cur = jnp.where(hit, ninf, cur)
    total = functools.reduce(lambda a, b: a + b, sel)
    ids_ref[...] = jnp.concatenate(ids, axis=0)
    wts_ref[...] = jnp.concatenate([s / total * ROUTE_SCALE for s in sel], axis=0)

    before = jnp.dot(chosen.astype(BF16), tri_ref[...], preferred_element_type=F32) + run_ref[:, 0:1]
    rank_ref[...] = jnp.concatenate(
        [jnp.sum(jnp.where(eio == ei, before, 0.0), axis=0, keepdims=True) for ei in ids], axis=0).astype(I32)
    run = run_ref[...] + jnp.sum(chosen, axis=1, keepdims=True)
    run_ref[...] = run
    cnt_ref[...] = run


def _outproj(layer, mix_ctx, mix_lat, x, mods, g_ffn, w_out_bf, wr_split, router_bias, tri):
    row = lambda i: (i, 0)
    col = lambda i: (0, i)
    const = lambda i: (0, 0)
    lay = lambda i: (layer, 0, 0)
    return pl.pallas_call(
        _outproj_kernel,
        grid=(N_TILES,),
        in_specs=[_CTX_ROWS, _LAT_ROWS, _CTX_ROWS, _LAT_ROWS,
                  _mod_spec(layer),
                  pl.BlockSpec((1, 1, D_MODEL), lay),
                  pl.BlockSpec((1, D_MODEL, D_MODEL), lay, pipeline_mode=pl.Buffered(1)),
                  pl.BlockSpec((1, 2 * N_EXPERTS, D_MODEL), lay),
                  pl.BlockSpec((1, N_EXPERTS, 1), lay),
                  pl.BlockSpec((TR, TR), const)],
        out_specs=[pl.BlockSpec((TR, D_MODEL), row),
                   pl.BlockSpec((TR * SLAB, LANES), row),
                   pl.BlockSpec((TOP_K, TR), col),
                   pl.BlockSpec((TOP_K, TR), col),
                   pl.BlockSpec((TOP_K, TR), col),
                   pl.BlockSpec((N_EXPERTS, META_LANES), const)],
        out_shape=[jax.ShapeDtypeStruct((T_ALL, D_MODEL), F32),
                   jax.ShapeDtypeStruct((T_ALL * SLAB, LANES), U32),
                   jax.ShapeDtypeStruct((TOP_K, T_ALL), I32),
                   jax.ShapeDtypeStruct((TOP_K, T_ALL), F32),
                   jax.ShapeDtypeStruct((TOP_K, T_ALL), I32),
                   jax.ShapeDtypeStruct((N_EXPERTS, META_LANES), F32)],
        scratch_shapes=[pltpu.VMEM((N_EXPERTS, META_LANES), F32)],
        compiler_params=_cparams(("arbitrary",)),
        name="outproj",
    )(mix_ctx, mix_lat, *x, mods, g_ffn.reshape(DEPTH, 1, D_MODEL), w_out_bf, wr_split,
      router_bias.reshape(DEPTH, N_EXPERTS, 1), tri)


def _positions_kernel(ids_ref, rank_ref, cnt_ref, pos_ref, te_ref, meta_ref):
    ids = ids_ref[...]
    pos = rank_ref[...]
    tile_io = lax.broadcasted_iota(I32, (1, TE_LANES), 1)
    lane_io = lax.broadcasted_iota(I32, (1, META_LANES), 1)
    te = jnp.zeros((1, TE_LANES), I32)
    starts = jnp.zeros((1, META_LANES), I32)
    counts = jnp.zeros((1, META_LANES), I32)
    tile_ends = jnp.zeros((1, META_LANES), I32)
    parities = jnp.zeros((1, META_LANES), I32)
    start = jnp.zeros((1, 1), I32)
    n_seg = jnp.zeros((1, 1), I32)
    for e in range(N_EXPERTS):
        c = cnt_ref[e:e + 1, 0:1].astype(I32)
        pos = pos + jnp.where(ids == e, start, 0)
        starts = starts + jnp.where(lane_io == e, start, 0)
        counts = counts + jnp.where(lane_io == e, c, 0)
        parities = parities + jnp.where(lane_io == e, n_seg % 2, 0)
        start = start + ((c + (TM - 1)) // TM) * TM
        n_seg = n_seg + jnp.where(c > 0, 1, 0)
        tile_ends = tile_ends + jnp.where(lane_io == e, start // TM, 0)
        te = te + jnp.where(tile_io * TM >= start, 1, 0)
    pos_ref[...] = pos
    te_ref[...] = jnp.minimum(te, N_EXPERTS - 1)
    meta_ref[0:1, :] = starts
    meta_ref[1:2, :] = counts
    meta_ref[2:3, :] = jnp.broadcast_to(start // TM, (1, META_LANES))
    meta_ref[3:4, :] = tile_ends
    meta_ref[4:5, :] = parities
    meta_ref[5:8, :] = jnp.zeros((3, META_LANES), I32)


def _positions(ids, rank, cnt):
    return pl.pallas_call(
        _positions_kernel,
        out_shape=[jax.ShapeDtypeStruct((TOP_K, T_ALL), I32),
                   jax.ShapeDtypeStruct((1, TE_LANES), I32),
                   jax.ShapeDtypeStruct((8, META_LANES), I32)],
        compiler_params=_cparams(),
        name="positions",
    )(ids, rank, cnt)


SC_CORES, SC_SUBCORES = 2, 16
SC_WORKERS = SC_CORES * SC_SUBCORES
SC_CHUNK = 32


def _sc_kernel(body, out_type, n_idx_rows, name):
    return pl.kernel(
        body,
        out_type=out_type,
        mesh=plsc.VectorSubcoreMesh(core_axis_name="c", subcore_axis_name="s",
                                    num_cores=SC_CORES, num_subcores=SC_SUBCORES),
        scratch_types=[pltpu.VMEM((n_idx_rows, SC_CHUNK), I32),
                       pltpu.VMEM((2, SC_CHUNK, SLAB, LANES), U32),
                       pltpu.SemaphoreType.DMA,
                       pltpu.SemaphoreType.DMA((2,)),
                       pltpu.SemaphoreType.DMA((2,))],
        compiler_params=pltpu.CompilerParams(use_tc_tiling_on_sc=True),
        name=name,
    )


def _sc_worker():
    return lax.axis_index("s") * SC_CORES + lax.axis_index("c")


def _sc_load_indices(n_rows, src_of_row, idx_v, sem):
    @pl.loop(0, n_rows)
    def _(r):
        src, dst = src_of_row(r), idx_v.at[r]
        pltpu.make_async_copy(src, dst, sem).start()

    @pl.loop(0, n_rows)
    def _(r):
        src, dst = src_of_row(r), idx_v.at[r]
        pltpu.make_async_copy(src, dst, sem).wait()


def _sc_two_slot_pipeline(n_chunks, fill, fill_wait, drain, drain_wait):
    assert n_chunks % 2 == 0 and n_chunks >= 4
    fill(0, 0)

    @pl.loop(0, n_chunks // 2)
    def _(pair):
        c = 2 * pair

        @pl.when(pair >= 1)
        def _():
            drain_wait(c - 1, 1)

        fill(c + 1, 1)
        fill_wait(c, 0)
        drain(c, 0)

        @pl.when(pair < n_chunks // 2 - 1)
        def _():
            drain_wait(c, 0)
            fill(c + 2, 0)

        fill_wait(c + 1, 1)
        drain(c + 1, 1)

    drain_wait(n_chunks - 2, 0)
    drain_wait(n_chunks - 1, 1)


def _dispatch(pos_flat, h3):
    n_chunks = T_ALL // SC_WORKERS // SC_CHUNK

    def body(pos_hbm, h_hbm, xs_hbm, idx_v, rows_v, idx_sem, fill_sem, drain_sem):
        t_base = _sc_worker() * (n_chunks * SC_CHUNK)

        def index_src(r):
            t0 = t_base + (r // TOP_K) * SC_CHUNK
            return pos_hbm.at[pl.ds(pl.multiple_of((r % TOP_K) * T_ALL + t0, SC_CHUNK), SC_CHUNK)]

        _sc_load_indices(n_chunks * TOP_K, index_src, idx_v, idx_sem)

        def load(c, slot):
            t0 = pl.multiple_of(t_base + c * SC_CHUNK, SC_CHUNK)
            return pltpu.make_async_copy(h_hbm.at[pl.ds(t0, SC_CHUNK)], rows_v.at[slot], fill_sem.at[slot])

        def scatter(c, slot, k):
            return pltpu.make_async_copy(rows_v.at[slot], xs_hbm.at[idx_v.at[c * TOP_K + k]], drain_sem.at[slot])

        def drain(c, slot):
            for k in range(TOP_K):
                scatter(c, slot, k).start()

        def drain_wait(c, slot):
            for k in range(TOP_K):
                scatter(c, slot, k).wait()

        _sc_two_slot_pipeline(n_chunks, lambda c, s: load(c, s).start(), lambda c, s: load(c, s).wait(),
                              drain, drain_wait)

    return _sc_kernel(body, jax.ShapeDtypeStruct((P_SORT, SLAB, LANES), U32), n_chunks * TOP_K,
                      "sc_dispatch")(pos_flat, h3)


def _gather(pos_flat, y3):
    n_items = pos_flat.shape[0]
    n_chunks = n_items // SC_WORKERS // SC_CHUNK

    def body(pos_hbm, y_hbm, out_hbm, idx_v, rows_v, idx_sem, fill_sem, drain_sem):
        base = _sc_worker() * (n_chunks * SC_CHUNK)

        def index_src(r):
            return pos_hbm.at[pl.ds(pl.multiple_of(base + r * SC_CHUNK, SC_CHUNK), SC_CHUNK)]

        _sc_load_indices(n_chunks, index_src, idx_v, idx_sem)

        def gather(c, slot):
            return pltpu.make_async_copy(y_hbm.at[idx_v.at[c]], rows_v.at[slot], fill_sem.at[slot])

        def store(c, slot):
            i0 = pl.multiple_of(base + c * SC_CHUNK, SC_CHUNK)
            return pltpu.make_async_copy(rows_v.at[slot], out_hbm.at[pl.ds(i0, SC_CHUNK)], drain_sem.at[slot])

        _sc_two_slot_pipeline(n_chunks, lambda c, s: gather(c, s).start(), lambda c, s: gather(c, s).wait(),
                              lambda c, s: store(c, s).start(), lambda c, s: store(c, s).wait())

    return _sc_kernel(body, jax.ShapeDtypeStruct((n_items, SLAB, LANES), U32), n_chunks,
                      "sc_gather")(pos_flat, y3)


def _experts_kernel(layer, te_ref, nt_ref, end_ref, tend_ref, par_ref, x_ref, wg_hbm, wu_hbm, wd_hbm, y_ref,
                    wg_f, wu_f, wd_f, sem):
    j = pl.program_id(0)
    used = j < nt_ref[0]
    e = te_ref[j]
    first = (j == 0) | (te_ref[jnp.maximum(j - 1, 0)] != e)
    slot = par_ref[e]

    def fetch(expert, s):
        return [pltpu.make_async_copy(src.at[layer, expert], dst.at[s], sem.at[s])
                for src, dst in ((wg_hbm, wg_f), (wu_hbm, wu_f), (wd_hbm, wd_f))]

    @pl.when(used & (j == 0))
    def _():
        for cp in fetch(e, slot):
            cp.start()

    @pl.when(used & first)
    def _():
        for cp in fetch(e, slot):
            cp.wait()
        nxt = tend_ref[e]

        @pl.when(nxt < nt_ref[0])
        def _():
            for cp in fetch(te_ref[jnp.minimum(nxt, pl.num_programs(0) - 1)], 1 - slot):
                cp.start()

    @pl.when(used)
    def _():
        valid = end_ref[te_ref[j]] - j * TM
        rows = lax.broadcasted_iota(I32, (TM, HALF), 0)
        words = jnp.where(rows < valid, _load_slabs(x_ref, TM), jnp.uint32(0))
        xa, xb = _unpack_halves(words)
        xa, xb = xa.astype(BF16), xb.astype(BF16)
        g = (jnp.dot(xa, wg_f[slot, :HALF].astype(BF16), preferred_element_type=F32)
             + jnp.dot(xb, wg_f[slot, HALF:].astype(BF16), preferred_element_type=F32))
        u = (jnp.dot(xa, wu_f[slot, :HALF].astype(BF16), preferred_element_type=F32)
             + jnp.dot(xb, wu_f[slot, HALF:].astype(BF16), preferred_element_type=F32))
        a = (_silu(g) * u).astype(BF16)
        _store_slabs(y_ref, _pack_halves(jnp.dot(a, wd_f[slot].astype(BF16), preferred_element_type=F32)))


def _experts(layer, te, nt, end, tile_end, parity, xs, w_gate, w_up, w_down):
    n_tiles = xs.shape[0] // (TM * SLAB)
    tile = lambda j, te, nt, *_: (jnp.minimum(j, nt[0] - 1), 0)
    return pl.pallas_call(
        functools.partial(_experts_kernel, layer),
        grid_spec=pltpu.PrefetchScalarGridSpec(
            num_scalar_prefetch=5,
            grid=(n_tiles,),
            in_specs=[pl.BlockSpec((TM * SLAB, LANES), tile),
                      pl.BlockSpec(memory_space=pl.ANY),
                      pl.BlockSpec(memory_space=pl.ANY),
                      pl.BlockSpec(memory_space=pl.ANY)],
            out_specs=pl.BlockSpec((TM * SLAB, LANES), tile),
            scratch_shapes=[pltpu.VMEM((2, D_MODEL, EXPERT_FF), F32),
                            pltpu.VMEM((2, D_MODEL, EXPERT_FF), F32),
                            pltpu.VMEM((2, EXPERT_FF, D_MODEL), F32),
                            pltpu.SemaphoreType.DMA((2,))]),
        out_shape=jax.ShapeDtypeStruct(xs.shape, U32),
        compiler_params=_cparams(("arbitrary",)),
        name="experts",
    )(te, nt, end, tile_end, parity, xs, w_gate, w_up, w_down)


def _combine_kernel(final, wts_ref, x1_ref, ysh_ref, mod_ref, g_ref, buf_ref, *rest):
    o_ref, x2_ref, wb_ref = rest
    w = jnp.transpose(wts_ref[...])
    for k in range(TOP_K):
        wb_ref[k] = jnp.broadcast_to(w[:, k:k + 1], (TC_ROWS, LANES))
    sq = jnp.zeros((TC_ROWS, LANES), F32)
    for s in range(SLAB):
        acc_a, acc_b = _unpack_halves(ysh_ref[pl.ds(s, TC_ROWS, stride=SLAB), :])
        for k in range(TOP_K):
            ya, yb = _unpack_halves(buf_ref[k, pl.ds(s, TC_ROWS, stride=SLAB), :])
            acc_a = acc_a + wb_ref[k] * ya
            acc_b = acc_b + wb_ref[k] * yb
        for half, acc in ((0, acc_a), (1, acc_b)):
            cols = slice(half * HALF + s * LANES, half * HALF + (s + 1) * LANES)
            x2 = x1_ref[:, cols] + mod_ref[0, 0, 5:6, cols] * acc
            x2_ref[:, cols] = x2
            if final:
                sq = sq + x2 * x2
    out = x2_ref[...]
    if final:
        ms = jnp.sum(sq, axis=-1, keepdims=True) * (1.0 / D_MODEL)
        out = out * lax.rsqrt(ms + EPS) * g_ref[...]

    o_ref[...] = out


def _combine(layer, row0, n_rows, wts, x1, ysh, mods, final_norm, yg):
    final = layer == DEPTH - 1
    t0 = row0 // TC_ROWS
    row = lambda i: (t0 + i, 0)
    return pl.pallas_call(
        functools.partial(_combine_kernel, final),
        grid=(n_rows // TC_ROWS,),
        in_specs=[pl.BlockSpec((TOP_K, TC_ROWS), lambda i: (0, t0 + i)),
                  pl.BlockSpec((TC_ROWS, D_MODEL), row),
                  pl.BlockSpec((TC_ROWS * SLAB, LANES), row),
                  pl.BlockSpec((1, 1, N_MOD, D_MODEL),
                               lambda i: (layer, _cond_row((t0 + i) // (TR // TC_ROWS)), 0, 0)),
                  pl.BlockSpec((1, D_MODEL), lambda i: (0, 0)),
                  pl.BlockSpec((TOP_K, TC_ROWS * SLAB, LANES), lambda i: (0, i, 0))],
        out_specs=pl.BlockSpec((TC_ROWS, D_MODEL), lambda i: (i, 0)),
        out_shape=jax.ShapeDtypeStruct((n_rows, D_MODEL), F32),
        scratch_shapes=[pltpu.VMEM((TC_ROWS, D_MODEL), F32),
                        pltpu.VMEM((TOP_K, TC_ROWS, LANES), F32)],
        compiler_params=_cparams(("arbitrary",)),
        name="combine",
    )(wts, x1, ysh, mods, final_norm.reshape(1, D_MODEL), yg)


def kernel(x_prompt, x_sample, cache_k, cache_v, c, c_ctx, w_mod, b_mod, g_mix, g_ffn, w_in, w_out, conv_w,
           attn_sink, w_router, router_bias, w_gate_e, w_up_e, w_down_e, w_gate_s, w_up_s, w_down_s, final_norm):
    ccsc = jnp.asarray(_channel_dft(), BF16)
    csl_ctx = jnp.asarray(_dft_tables(SEQ), BF16)
    csl_lat = jnp.asarray(_dft_tables(DEC_SEQ), BF16)
    rope = tuple(jnp.asarray(t, F32) for t in _rope_tables(DEC_SEQ, TR))
    tri = jnp.asarray(np.triu(np.ones((TR, TR)), 1), BF16)
    shared_te = jnp.zeros((T_ALL // TM,), I32)
    shared_nt = jnp.full((1,), T_ALL // TM, I32)
    shared_end = jnp.full((1,), T_ALL, I32)

    cond = jnp.concatenate([c_ctx[None], c, jnp.zeros((N_COND - 1 - DEC_BATCH, D_MODEL), F32)], axis=0)
    mods = _mods(cond, w_mod, b_mod).reshape(DEPTH, N_COND, N_MOD, D_MODEL)
    cache_k = cache_k.reshape(DEC_BATCH, DEPTH, SEQ, KV_W)
    cache_v = cache_v.reshape(DEC_BATCH, DEPTH, SEQ, KV_W)
    w_in_bf, w_out_bf = w_in.astype(BF16), w_out.astype(BF16)
    wr_t = jnp.swapaxes(w_router, 1, 2)
    wr_hi = wr_t.astype(BF16)
    wr_split = jnp.concatenate([wr_hi, (wr_t - wr_hi.astype(F32)).astype(BF16)], axis=1)
    ws_gate, ws_up, ws_down = w_gate_s[:, None], w_up_s[:, None], w_down_s[:, None]
    x = (x_prompt.reshape(T_CTX, D_MODEL), x_sample.reshape(T_LAT, D_MODEL))

    new_kv = ()
    for l in range(DEPTH):
        q, kv, fm, *new_kv = _inproj(l, x, mods, g_mix, w_in_bf, rope, tuple(new_kv))
        mix_ctx = _ctx_mix(l, attn_sink, q, kv, fm, ccsc, csl_ctx, conv_w)
        mix_lat = _lat_mix(l, attn_sink, q, kv, fm, cache_k, cache_v, ccsc, csl_lat, conv_w)
        x1, h2, ids, wts, rank, cnt = _outproj(l, mix_ctx, mix_lat, x, mods, g_ffn, w_out_bf, wr_split,
                                               router_bias, tri)
        pos, te, meta = _positions(ids, rank, cnt)
        pos_flat = pos.reshape(TOP_K * T_ALL)
        xs = _dispatch(pos_flat, h2.reshape(T_ALL, SLAB, LANES)).reshape(P_SORT * SLAB, LANES)
        y = _experts(l, te[0, :N_SORT_TILES], meta[2, :1], meta[0, :N_EXPERTS] + meta[1, :N_EXPERTS],
                     meta[3, :N_EXPERTS], meta[4, :N_EXPERTS], xs, w_gate_e, w_up_e, w_down_e)
        ysh = _experts(l, shared_te, shared_nt, shared_end, shared_nt, shared_te[:1], h2,
                       ws_gate, ws_up, ws_down)
        y3 = y.reshape(P_SORT, SLAB, LANES)
        x = []
        for row0, n_rows in ((0, T_CTX), (T_CTX, T_LAT)):
            part_pos = pos[:, row0:row0 + n_rows].reshape(TOP_K * n_rows)
            yg = _gather(part_pos, y3).reshape(TOP_K, n_rows * SLAB, LANES)
            x.append(_combine(l, row0, n_rows, wts, x1, ysh, mods, final_norm, yg))

    y_prompt = x[0].reshape(BATCH, SEQ, D_MODEL)
    y_sample = x[1].reshape(DEC_BATCH, DEC_SEQ, D_MODEL)
    new_k, new_v = new_kv
    return y_prompt, y_sample, new_k, new_v
```
